```python
import math
import jax, jax.numpy as jnp
from jax import lax
import numpy as np

D_MODEL = 2048
BATCH = 4
SEQ = 4096
DEPTH = 2

MEM_LEN = 256
MLA_HEADS = 16
Q_LORA = 512
KV_LORA = 512
QK_NOPE = 128
QK_ROPE = 64
V_HEAD = 128
ROPE_BASE = 10000.0
ATTN_BLOCK = 128
GDN_HEADS = 16
GDN_DK = 128
GDN_DV = 128
CONV_K = 4
CHUNK = 64
GDN_QK_W = GDN_HEADS * GDN_DK
GDN_V_W = GDN_HEADS * GDN_DV
X_HEADS = 4
X_HEAD_DIM = 128
X_W = X_HEADS * X_HEAD_DIM
N_EXPERTS = 64
N_GROUPS = 8
EXPERTS_PER_GROUP = N_EXPERTS // N_GROUPS
TOP_K = 2
GROUP_SCORE_TOPK = 2
D_EXPERT = 512
MOE_BLOCK = 128
DN_ALPHA = (2.0 * DEPTH) ** 0.25
DN_BETA = (8.0 * DEPTH) ** -0.25
LN_EPS = 1e-5
RMS_EPS = 1e-6
IN_SPLIT_SIZES = (Q_LORA, KV_LORA, QK_ROPE,
                  GDN_QK_W, GDN_QK_W, GDN_V_W, GDN_V_W,
                  GDN_HEADS, GDN_HEADS,
                  D_MODEL, D_MODEL)
D_IN = sum(IN_SPLIT_SIZES)

kernel_name = 'hybrid_mla_gdn_memx_groupmoe_deepnorm'


def layer_norm(x, g, b):
    xf = x.astype(jnp.float32)
    mu = jnp.mean(xf, -1, keepdims=True)
    var = jnp.mean(jnp.square(xf - mu), -1, keepdims=True)
    return ((xf - mu) * lax.rsqrt(var + LN_EPS) * g.astype(jnp.float32) + b.astype(jnp.float32)).astype(x.dtype)


def rms_norm(x, g):
    xf = x.astype(jnp.float32)
    y = xf * lax.rsqrt(jnp.mean(xf * xf, -1, keepdims=True) + RMS_EPS) * g.astype(jnp.float32)
    return y.astype(x.dtype)


def l2_normalize(x):
    xf = x.astype(jnp.float32)
    return xf * lax.rsqrt(jnp.sum(xf * xf, -1, keepdims=True) + RMS_EPS)


def rotary(x, positions):
    half = QK_ROPE // 2
    inv_freq = jnp.power(ROPE_BASE, -jnp.arange(half, dtype=jnp.float32) / half)
    ang = positions.astype(jnp.float32)[..., None] * inv_freq
    cos, sin = jnp.cos(ang)[:, :, None, :], jnp.sin(ang)[:, :, None, :]
    xf = x.astype(jnp.float32)
    x1, x2 = xf[..., :half], xf[..., half:]
    return jnp.concatenate([x1 * cos - x2 * sin, x1 * sin + x2 * cos], -1).astype(x.dtype)


def causal_block_attention(q, k, v):
    b, s, h, dqk = q.shape
    nq = s // ATTN_BLOCK
    scale = dqk ** -0.5
    q_blocks = q.reshape(b, nq, ATTN_BLOCK, h, dqk).transpose(1, 0, 2, 3, 4)
    key_pos = jnp.arange(s)

    def one_block(args):
        qb, i = args
        scores = jnp.einsum('bqhd,bkhd->bhqk', qb, k).astype(jnp.float32) * scale
        q_pos = i * ATTN_BLOCK + jnp.arange(ATTN_BLOCK)
        scores = jnp.where(key_pos[None, :] <= q_pos[:, None], scores, -jnp.inf)
        p = jax.nn.softmax(scores, axis=-1).astype(v.dtype)
        return jnp.einsum('bhqk,bkhd->bqhd', p, v)

    out = lax.map(one_block, (q_blocks, jnp.arange(nq)))
    return out.transpose(1, 0, 2, 3, 4).reshape(b, s, h, v.shape[-1])


def mla_branch(c_q, c_kv, k_rope, positions, q_norm, kv_norm, w_uq, w_ukv):
    b, s, _ = c_q.shape
    q = (rms_norm(c_q, q_norm) @ w_uq).reshape(b, s, MLA_HEADS, QK_NOPE + QK_ROPE)
    kv = (rms_norm(c_kv, kv_norm) @ w_ukv).reshape(b, s, MLA_HEADS, QK_NOPE + V_HEAD)
    q_nope, q_pe = q[..., :QK_NOPE], rotary(q[..., QK_NOPE:], positions)
    k_nope, v = kv[..., :QK_NOPE], kv[..., QK_NOPE:]
    k_pe = jnp.broadcast_to(rotary(k_rope[:, :, None, :], positions), (b, s, MLA_HEADS, QK_ROPE))
    q = jnp.concatenate([q_nope, q_pe], -1)
    k = jnp.concatenate([k_nope, k_pe], -1)
    o = causal_block_attention(q, k, v)
    return o.reshape(b, s, MLA_HEADS * V_HEAD)


def causal_depthwise_conv(x, w):
    return lax.conv_general_dilated(x, w[:, None, :], window_strides=(1,), padding=[(CONV_K - 1, 0)],
                                    dimension_numbers=('NWC', 'WIO', 'NWC'),
                                    feature_group_count=x.shape[-1])


def unit_lower_inverse(l):
    eye = jnp.eye(CHUNK, dtype=l.dtype)
    m = -l
    inv = eye + m
    for _ in range(int(round(math.log2(CHUNK))) - 1):
        m = m @ m
        inv = inv @ (eye + m)
    return inv


def gated_delta_rule(q, k, v, g, beta):
    b, h, s, dk = q.shape
    dv = v.shape[-1]
    n = s // CHUNK
    q = q.reshape(b, h, n, CHUNK, dk) * (dk ** -0.5)
    k = k.reshape(b, h, n, CHUNK, dk)
    v = v.reshape(b, h, n, CHUNK, dv)
    g_cum = jnp.cumsum(g.reshape(b, h, n, CHUNK), -1)
    beta = beta.reshape(b, h, n, CHUNK, 1)
    k_beta, v_beta = k * beta, v * beta
    idx = jnp.arange(CHUNK)
    causal = idx[:, None] >= idx[None, :]
    strict = idx[:, None] > idx[None, :]
    decay = jnp.exp(jnp.where(causal, g_cum[..., :, None] - g_cum[..., None, :], -jnp.inf))
    l = jnp.where(strict, jnp.einsum('bhnid,bhnjd->bhnij', k_beta, k) * decay, 0.0)
    t = unit_lower_inverse(l)
    u = jnp.einsum('bhnij,bhnje->bhnie', t, v_beta)
    w = jnp.einsum('bhnij,bhnjd->bhnid', t, k_beta * jnp.exp(g_cum)[..., None])
    a_intra = jnp.einsum('bhnid,bhnjd->bhnij', q, k) * decay
    q_dec = q * jnp.exp(g_cum)[..., None]
    k_dec = k * jnp.exp(g_cum[..., -1:] - g_cum)[..., None]
    g_last = jnp.exp(g_cum[..., -1])

    def step(state, inp):
        u_c, w_c, qd_c, kd_c, a_c, gl_c = inp
        v_new = u_c - jnp.einsum('bhcd,bhde->bhce', w_c, state)
        o_c = jnp.einsum('bhcd,bhde->bhce', qd_c, state) + jnp.einsum('bhij,bhje->bhie', a_c, v_new)
        state = state * gl_c[..., None, None] + jnp.einsum('bhcd,bhce->bhde', kd_c, v_new)
        return state, o_c

    xs = tuple(jnp.moveaxis(a, 2, 0) for a in (u, w, q_dec, k_dec, a_intra, g_last))
    state0 = jnp.zeros((b, h, dk, dv), jnp.float32)
    _, o = lax.scan(step, state0, xs)
    return jnp.moveaxis(o, 0, 2).reshape(b, h, s, dv)


def gdn_branch(q, k, v, z, b_logit, a_logit, w_conv, a_log, dt_bias, o_norm):
    bsz, s, _ = q.shape
    qkv = jax.nn.silu(causal_depthwise_conv(jnp.concatenate([q, k, v], -1), w_conv))
    q, k, v = jnp.split(qkv, [GDN_QK_W, 2 * GDN_QK_W], axis=-1)
    heads = lambda a, d: a.reshape(bsz, s, GDN_HEADS, d).transpose(0, 2, 1, 3)
    q = l2_normalize(heads(q, GDN_DK))
    k = l2_normalize(heads(k, GDN_DK))
    v = heads(v, GDN_DV).astype(jnp.float32)
    beta = jax.nn.sigmoid(b_logit.astype(jnp.float32)).transpose(0, 2, 1)
    g = (-jnp.exp(a_log.astype(jnp.float32))
         * jax.nn.softplus(a_logit.astype(jnp.float32) + dt_bias.astype(jnp.float32))).transpose(0, 2, 1)
    o = gated_delta_rule(q, k, v, g, beta).transpose(0, 2, 1, 3)
    o = rms_norm(o, o_norm) * jax.nn.silu(z.reshape(bsz, s, GDN_HEADS, GDN_DV).astype(jnp.float32))
    return o.reshape(bsz, s, GDN_V_W).astype(z.dtype)


def hybrid_mixer(x, positions, w_in, q_norm, kv_norm, w_uq, w_ukv, w_conv, a_log, dt_bias, o_norm,
                 w_mla_proj, w_gdn_proj, w_mix_out):
    points = np.cumsum(IN_SPLIT_SIZES)[:-1].tolist()
    (c_q, c_kv, k_rope, g_q, g_k, g_v, g_z, g_b, g_a, gate_mla, gate_gdn) = jnp.split(x @ w_in, points, axis=-1)
    y_mla = mla_branch(c_q, c_kv, k_rope, positions, q_norm, kv_norm, w_uq, w_ukv) @ w_mla_proj
    y_gdn = gdn_branch(g_q, g_k, g_v, g_z, g_b, g_a, w_conv, a_log, dt_bias, o_norm) @ w_gdn_proj
    merged = jax.nn.sigmoid(gate_mla) * y_mla + jax.nn.sigmoid(gate_gdn) * y_gdn
    return merged @ w_mix_out


def memory_cross_attention(x, mem, w_q, w_kv, w_o):
    b, s, _ = x.shape
    q = (x @ w_q).reshape(b, s, X_HEADS, X_HEAD_DIM)
    kv = (mem @ w_kv).reshape(b, mem.shape[1], 2, X_HEADS, X_HEAD_DIM)
    k, v = kv[:, :, 0], kv[:, :, 1]
    scores = jnp.einsum('bqhd,bmhd->bhqm', q, k).astype(jnp.float32) * (X_HEAD_DIM ** -0.5)
    p = jax.nn.softmax(scores, axis=-1).astype(v.dtype)
    o = jnp.einsum('bhqm,bmhd->bqhd', p, v).reshape(b, s, X_W)
    return o @ w_o


def route(xt, w_router, router_bias):
    t = xt.shape[0]
    scores = jax.nn.sigmoid((xt @ w_router).astype(jnp.float32))
    sel = (scores + router_bias.astype(jnp.float32)).reshape(t, N_GROUPS, EXPERTS_PER_GROUP)
    group_score = jnp.sum(lax.top_k(sel, GROUP_SCORE_TOPK)[0], -1)
    best = jnp.argmax(group_score, -1).astype(jnp.int32)
    in_group = sel[jnp.arange(t), best]
    _, local = lax.top_k(in_group, TOP_K)
    expert_idx = best[:, None] * EXPERTS_PER_GROUP + local.astype(jnp.int32)
    gate = jnp.take_along_axis(scores, expert_idx, axis=1)
    return expert_idx, gate / jnp.sum(gate, -1, keepdims=True)


def moe(x, w_router, router_bias, w_gate, w_up, w_down):
    b, s, d = x.shape
    xt = x.reshape(b * s, d)
    expert_idx, gate = route(xt, w_router, router_bias)
    n_assign = b * s * TOP_K
    n_blocks = n_assign // MOE_BLOCK + N_EXPERTS
    flat_e = expert_idx.reshape(-1)
    order = jnp.argsort(flat_e)
    sorted_e = flat_e[order]
    counts = jnp.bincount(flat_e, length=N_EXPERTS)
    padded = (counts + MOE_BLOCK - 1) // MOE_BLOCK * MOE_BLOCK
    start = jnp.cumsum(counts) - counts
    padded_end = jnp.cumsum(padded)
    padded_start = padded_end - padded
    dest = padded_start[sorted_e] + jnp.arange(n_assign, dtype=jnp.int32) - start[sorted_e]
    slot_token = jnp.zeros((n_blocks * MOE_BLOCK,), jnp.int32).at[dest].set((order // TOP_K).astype(jnp.int32))
    slot_valid = jnp.zeros((n_blocks * MOE_BLOCK,), bool).at[dest].set(True)
    xs = jnp.where(slot_valid[:, None], xt[slot_token], 0).reshape(n_blocks, MOE_BLOCK, d)
    block_expert = jnp.minimum(jnp.searchsorted(padded_end, jnp.arange(n_blocks) * MOE_BLOCK, side='right'),
                               N_EXPERTS - 1)

    def expert_block(args):
        xb, e = args
        h = jax.nn.silu(xb @ w_gate[e]) * (xb @ w_up[e])
        return h @ w_down[e]

    ys = lax.map(expert_block, (xs, block_expert)).reshape(n_blocks * MOE_BLOCK, d)
    y_assign = jnp.zeros((n_assign, d), ys.dtype).at[order].set(ys[dest]).reshape(b * s, TOP_K, d)
    y = jnp.einsum('tk,tkd->td', gate.astype(ys.dtype), y_assign)
    return y.reshape(b, s, d)


def _normal(k, shape, scale):
    return jax.random.normal(k, shape, jnp.float32) * scale


def _gain(k, shape):
    return 1.0 + 0.02 * jax.random.normal(k, shape, jnp.float32)


def setup_inputs(seed: int = 0) -> dict:
    key = jax.random.key(seed)
    ks = jax.random.split(key, 32)
    L, D = DEPTH, D_MODEL
    offsets = jax.random.randint(ks[2], (BATCH, 1), 0, 1024, dtype=jnp.int32)
    positions = offsets + jnp.arange(SEQ, dtype=jnp.int32)[None, :]
    dt = jnp.exp(jax.random.uniform(ks[10], (L, GDN_HEADS), jnp.float32, math.log(1e-3), math.log(1e-1)))
    return {
        'x': _normal(ks[0], (BATCH, SEQ, D), 1.0),
        'mem': _normal(ks[1], (BATCH, MEM_LEN, D), 1.0),
        'positions': positions,
        'w_in': _normal(ks[3], (L, D, D_IN), D ** -0.5),
        'mla_q_norm': _gain(ks[4], (L, Q_LORA)),
        'mla_kv_norm': _gain(ks[5], (L, KV_LORA)),
        'w_uq': _normal(ks[6], (L, Q_LORA, MLA_HEADS * (QK_NOPE + QK_ROPE)), Q_LORA ** -0.5),
        'w_ukv': _normal(ks[7], (L, KV_LORA, MLA_HEADS * (QK_NOPE + V_HEAD)), KV_LORA ** -0.5),
        'gdn_conv': _normal(ks[8], (L, CONV_K, 2 * GDN_QK_W + GDN_V_W), CONV_K ** -0.5),
        'gdn_a_log': jnp.log(jax.random.uniform(ks[9], (L, GDN_HEADS), jnp.float32, 1.0, 16.0)),
        'gdn_dt_bias': dt + jnp.log(-jnp.expm1(-dt)),
        'gdn_o_norm': _gain(ks[11], (L, GDN_DV)),
        'w_mla_proj': _normal(ks[12], (L, MLA_HEADS * V_HEAD, D), (MLA_HEADS * V_HEAD) ** -0.5),
        'w_gdn_proj': _normal(ks[13], (L, GDN_V_W, D), GDN_V_W ** -0.5),
        'w_mix_out': _normal(ks[14], (L, D, D), DN_BETA * D ** -0.5),
        'ln_mix_g': _gain(ks[15], (L, D)),
        'ln_mix_b': _normal(ks[16], (L, D), 0.02),
        'w_xq': _normal(ks[17], (L, D, X_W), D ** -0.5),
        'w_xkv': _normal(ks[18], (L, D, 2 * X_W), D ** -0.5),
        'w_xo': _normal(ks[19], (L, X_W, D), DN_BETA * X_W ** -0.5),
        'ln_x_g': _gain(ks[20], (L, D)),
        'ln_x_b': _normal(ks[21], (L, D), 0.02),
        'w_router': _normal(ks[22], (D, N_EXPERTS), D ** -0.5),
        'router_bias': _normal(ks[23], (N_EXPERTS,), 0.01),
        'w_gate': _normal(ks[24], (L, N_EXPERTS, D, D_EXPERT), D ** -0.5),
        'w_up': _normal(ks[25], (L, N_EXPERTS, D, D_EXPERT), D ** -0.5),
        'w_down': _normal(ks[26], (L, N_EXPERTS, D_EXPERT, D), DN_BETA * D_EXPERT ** -0.5),
        'ln_moe_g': _gain(ks[27], (L, D)),
        'ln_moe_b': _normal(ks[28], (L, D), 0.02),
    }


def reference(x, mem, positions, w_in, mla_q_norm, mla_kv_norm, w_uq, w_ukv, gdn_conv, gdn_a_log,
              gdn_dt_bias, gdn_o_norm, w_mla_proj, w_gdn_proj, w_mix_out, ln_mix_g, ln_mix_b,
              w_xq, w_xkv, w_xo, ln_x_g, ln_x_b, w_router, router_bias, w_gate, w_up, w_down,
              ln_moe_g, ln_moe_b):
    for l in range(DEPTH):
        mix = hybrid_mixer(x, positions, w_in[l], mla_q_norm[l], mla_kv_norm[l], w_uq[l], w_ukv[l],
                           gdn_conv[l], gdn_a_log[l], gdn_dt_bias[l], gdn_o_norm[l],
                           w_mla_proj[l], w_gdn_proj[l], w_mix_out[l])
        x = layer_norm(DN_ALPHA * x + mix, ln_mix_g[l], ln_mix_b[l])
        cross = memory_cross_attention(x, mem, w_xq[l], w_xkv[l], w_xo[l])
        x = layer_norm(DN_ALPHA * x + cross, ln_x_g[l], ln_x_b[l])
        ffn = moe(x, w_router, router_bias, w_gate[l], w_up[l], w_down[l])
        x = layer_norm(DN_ALPHA * x + ffn, ln_moe_g[l], ln_moe_b[l])
    return x
```

```python
import functools
import math

import jax
import jax.numpy as jnp
import numpy as np
from jax import lax
from jax.experimental import pallas as pl
from jax.experimental.pallas import tpu as pltpu

D_MODEL = 2048
DEPTH = 2
MLA_HEADS = 16
Q_LORA = 512
KV_LORA = 512
QK_NOPE = 128
QK_ROPE = 64
V_HEAD = 128
QK_DIM = QK_NOPE + QK_ROPE
ROPE_BASE = 10000.0
GDN_HEADS = 16
GDN_DK = 128
GDN_DV = 128
CONV_K = 4
CHUNK = 64
GDN_W = GDN_HEADS * GDN_DK
X_HEADS = 4
X_HEAD_DIM = 128
X_W = X_HEADS * X_HEAD_DIM
N_EXPERTS = 64
N_GROUPS = 8
EXPERTS_PER_GROUP = N_EXPERTS // N_GROUPS
TOP_K = 2
D_EXPERT = 512
DN_ALPHA = (2.0 * DEPTH) ** 0.25
LN_EPS = 1e-5
RMS_EPS = 1e-6

MAIN_W = 3 * GDN_W + GDN_W + 2 * D_MODEL
SMALL_W = 1280
SMALL_ROPE_BLOCK = (Q_LORA + KV_LORA) // 128
SMALL_GATE_COL = Q_LORA + KV_LORA + 2 * QK_ROPE
SMALL_GATE_BLOCK = SMALL_GATE_COL // 128

GDN_HEAD_BLOCK = 4
GDN_ROW_BLOCK = 256
MOE_BLOCK = 256
VMEM_LIMIT = 56 * 1024 * 1024

BF16 = jnp.bfloat16
F32 = jnp.float32


def _tile(n, pref):
    return pref if n % pref == 0 else n


def _params(*sem):
    return pltpu.CompilerParams(dimension_semantics=sem, vmem_limit_bytes=VMEM_LIMIT)


def _sigmoid(x):
    return 1.0 / (1.0 + jnp.exp(-x))


def _softplus(x):
    return jnp.maximum(x, 0.0) + jnp.log1p(jnp.exp(-jnp.abs(x)))


def _layer_norm(y, g, b):
    mu = jnp.mean(y, -1, keepdims=True)
    d = y - mu
    var = jnp.mean(d * d, -1, keepdims=True)
    return d * lax.rsqrt(var + LN_EPS) * g + b


def _dot(a, b):
    return jnp.dot(a, b, preferred_element_type=F32)


def _dot_nt(a, b):
    return lax.dot_general(a, b, (((1,), (1,)), ((), ())), preferred_element_type=F32)


def _dot_tn(a, b):
    return lax.dot_general(a, b, (((0,), (0,)), ((), ())), preferred_element_type=F32)


def _mm_kernel(a_ref, w_ref, o_ref):
    o_ref[...] = _dot(a_ref[...], w_ref[...]).astype(o_ref.dtype)


def _matmul(a, w, out_dtype, tm, tn, name):
    m, k = a.shape
    n = w.shape[1]
    tm, tn = _tile(m, tm), _tile(n, tn)
    return pl.pallas_call(
        _mm_kernel,
        grid=(m // tm, n // tn),
        in_specs=[pl.BlockSpec((tm, k), lambda i, j: (i, 0)),
                  pl.BlockSpec((k, tn), lambda i, j: (0, j))],
        out_specs=pl.BlockSpec((tm, tn), lambda i, j: (i, j)),
        out_shape=jax.ShapeDtypeStruct((m, n), out_dtype),
        compiler_params=_params("parallel", "parallel"),
        name=name,
    )(a, w)


def _mla_prep_kernel(cq_ref, ckv_ref, kr_ref, cs_ref, qn_ref, kvn_ref, wq_ref, wkv_ref,
                     q_ref, k_ref, v_ref, cqn_s, ckvn_s, kpe_s):
    h = pl.program_id(2)

    @pl.when(h == 0)
    def _():
        cq = cq_ref[0]
        cqn_s[...] = (cq * lax.rsqrt(jnp.mean(cq * cq, -1, keepdims=True) + RMS_EPS) * qn_ref[...]).astype(BF16)
        ckv = ckv_ref[0]
        ckvn_s[...] = (ckv * lax.rsqrt(jnp.mean(ckv * ckv, -1, keepdims=True) + RMS_EPS) * kvn_ref[...]).astype(BF16)
        u = kr_ref[0] * cs_ref[0]
        kpe_s[...] = (u + pltpu.roll(u, QK_ROPE, 1)).astype(BF16)

    rq = _dot(cqn_s[...], wq_ref[0])
    t = rq[:, QK_NOPE:] * cs_ref[0]
    rot = t + pltpu.roll(t, QK_ROPE, 1)
    q_ref[0, 0, :, :QK_NOPE] = rq[:, :QK_NOPE].astype(BF16)
    q_ref[0, 0, :, QK_NOPE:] = rot[:, :QK_ROPE].astype(BF16)
    rkv = _dot(ckvn_s[...], wkv_ref[0])
    k_ref[0, 0, :, :QK_NOPE] = rkv[:, :QK_NOPE].astype(BF16)
    k_ref[0, 0, :, QK_NOPE:] = kpe_s[:, :QK_ROPE]
    v_ref[0, 0] = rkv[:, QK_NOPE:].astype(BF16)


def _mla_prep(small, cs, q_norm, kv_norm, wq, wkv):
    b, s, _ = small.shape
    ts = _tile(s, 512)
    hq = MLA_HEADS
    return pl.pallas_call(
        _mla_prep_kernel,
        grid=(b, s // ts, hq),
        in_specs=[
            pl.BlockSpec((1, ts, Q_LORA), lambda bi, si, h: (bi, si, 0)),
            pl.BlockSpec((1, ts, KV_LORA), lambda bi, si, h: (bi, si, 1)),
            pl.BlockSpec((1, ts, 128), lambda bi, si, h: (bi, si, SMALL_ROPE_BLOCK)),
            pl.BlockSpec((1, ts, 128), lambda bi, si, h: (bi, si, 0)),
            pl.BlockSpec((1, Q_LORA), lambda bi, si, h: (0, 0)),
            pl.BlockSpec((1, KV_LORA), lambda bi, si, h: (0, 0)),
            pl.BlockSpec((1, Q_LORA, 256), lambda bi, si, h: (h, 0, 0)),
            pl.BlockSpec((1, KV_LORA, 256), lambda bi, si, h: (h, 0, 0)),
        ],
        out_specs=[
            pl.BlockSpec((1, 1, ts, QK_DIM), lambda bi, si, h: (bi, h, si, 0)),
            pl.BlockSpec((1, 1, ts, QK_DIM), lambda bi, si, h: (bi, h, si, 0)),
            pl.BlockSpec((1, 1, ts, V_HEAD), lambda bi, si, h: (bi, h, si, 0)),
        ],
        out_shape=[
            jax.ShapeDtypeStruct((b, hq, s, QK_DIM), BF16),
            jax.ShapeDtypeStruct((b, hq, s, QK_DIM), BF16),
            jax.ShapeDtypeStruct((b, hq, s, V_HEAD), BF16),
        ],
        scratch_shapes=[pltpu.VMEM((ts, Q_LORA), BF16), pltpu.VMEM((ts, KV_LORA), BF16),
                        pltpu.VMEM((ts, 128), BF16)],
        compiler_params=_params("parallel", "parallel", "arbitrary"),
        name="mla_prep",
    )(small, small, small, cs, q_norm, kv_norm, wq, wkv)


def _attn_kernel(q_ref, k_ref, v_ref, o_ref, m_s, l_s, acc_s, *, tq, scale):
    i = pl.program_id(2)
    q = q_ref[0, 0]
    m_s[...] = jnp.full(m_s.shape, -jnp.inf, F32)
    l_s[...] = jnp.zeros(l_s.shape, F32)
    acc_s[...] = jnp.zeros(acc_s.shape, F32)

    def block(j, masked):
        start = pl.multiple_of(j * tq, tq)
        kb = k_ref[0, 0, pl.ds(start, tq), :]
        vb = v_ref[0, 0, pl.ds(start, tq), :]
        s = _dot_nt(q, kb) * scale
        if masked:
            row = lax.broadcasted_iota(jnp.int32, (tq, tq), 0)
            col = lax.broadcasted_iota(jnp.int32, (tq, tq), 1)
            s = jnp.where(col <= row, s, -jnp.inf)
        m_prev = m_s[...]
        m_new = jnp.maximum(m_prev, jnp.max(s, -1, keepdims=True))
        alpha = jnp.exp(m_prev - m_new)
        p = jnp.exp(s - m_new)
        l_s[...] = alpha * l_s[...] + jnp.sum(p, -1, keepdims=True)
        acc_s[...] = alpha * acc_s[...] + _dot(p.astype(BF16), vb)
        m_s[...] = m_new

    def body(j, carry):
        block(j, False)
        return carry

    lax.fori_loop(0, i, body, 0)
    block(i, True)
    o_ref[0] = (acc_s[...] / l_s[...]).astype(o_ref.dtype)


def _attention(q, k, v):
    b, h, s, dqk = q.shape
    dv = v.shape[-1]
    tq = _tile(s, 512)
    kern = functools.partial(_attn_kernel, tq=tq, scale=dqk ** -0.5)
    return pl.pallas_call(
        kern,
        grid=(b, h, s // tq),
        in_specs=[
            pl.BlockSpec((1, 1, tq, dqk), lambda bi, hi, i: (bi, hi, i, 0)),
            pl.BlockSpec((1, 1, s, dqk), lambda bi, hi, i: (bi, hi, 0, 0)),
            pl.BlockSpec((1, 1, s, dv), lambda bi, hi, i: (bi, hi, 0, 0)),
        ],
        out_specs=pl.BlockSpec((1, tq, dv), lambda bi, hi, i: (bi, i, hi)),
        out_shape=jax.ShapeDtypeStruct((b, s, h * dv), BF16),
        scratch_shapes=[pltpu.VMEM((tq, 1), F32), pltpu.VMEM((tq, 1), F32), pltpu.VMEM((tq, dv), F32)],
        compiler_params=_params("parallel", "parallel", "arbitrary"),
        name="mla_attention",
    )(q, k, v)


def _conv_kernel(x_ref, w_ref, o_ref, xx_s, *, ts):
    c = pl.program_id(1)
    si = pl.program_id(2)

    @pl.when(si == 0)
    def _():
        xx_s[0:8, :] = jnp.zeros((8, 128), F32)

    @pl.when(si > 0)
    def _():
        xx_s[0:8, :] = xx_s[ts:ts + 8, :]

    xx_s[8:ts + 8, :] = x_ref[0]
    w = w_ref[...]
    y = w[CONV_K - 1:CONV_K, :] * xx_s[8:ts + 8, :]
    for j in range(CONV_K - 1):
        off = 8 - (CONV_K - 1) + j
        y = y + w[j:j + 1, :] * xx_s[off:off + ts, :]
    y = y * _sigmoid(y)
    yn = y * lax.rsqrt(jnp.sum(y * y, -1, keepdims=True) + RMS_EPS)
    yn = yn * jnp.where(c < GDN_HEADS, GDN_DK ** -0.5, 1.0)
    o_ref[0, 0] = jnp.where(c < 2 * GDN_HEADS, yn, y)


def _gdn_conv(main, w_conv):
    b, s, _ = main.shape
    ts = _tile(s, 1024)
    nc = 3 * GDN_HEADS
    return pl.pallas_call(
        functools.partial(_conv_kernel, ts=ts),
        grid=(b, nc, s // ts),
        in_specs=[pl.BlockSpec((1, ts, 128), lambda bi, c, si: (bi, si, c)),
                  pl.BlockSpec((CONV_K, 128), lambda bi, c, si: (0, c))],
        out_specs=pl.BlockSpec((1, 1, ts, 128), lambda bi, c, si: (bi, c, si, 0)),
        out_shape=jax.ShapeDtypeStruct((b, nc, s, 128), F32),
        scratch_shapes=[pltpu.VMEM((ts + 8, 128), F32)],
        compiler_params=_params("parallel", "parallel", "arbitrary"),
        name="gdn_conv",
    )(main, w_conv)


def _gdn_kernel(q_ref, k_ref, v_ref, z_ref, gate_ref, gatet_ref, arow_ref, dtrow_ref, acol_ref, dtcol_ref,
                onorm_ref, o_ref, state_s, *, hb, rb):
    hg = pl.program_id(1)
    r = pl.program_id(2)

    @pl.when(r == 0)
    def _():
        state_s[...] = jnp.zeros(state_s.shape, F32)

    blk = gate_ref[0]
    beta_tm = _sigmoid(blk)
    g_tm = -jnp.exp(arow_ref[...]) * _softplus(blk + dtrow_ref[...])
    g_t = -jnp.exp(acol_ref[...]) * _softplus(gatet_ref[0] + dtcol_ref[...])

    ri = lax.broadcasted_iota(jnp.int32, (CHUNK, CHUNK), 0)
    ci = lax.broadcasted_iota(jnp.int32, (CHUNK, CHUNK), 1)
    causal = ri >= ci
    strict = ri > ci
    tri = jnp.where(causal, 1.0, 0.0).astype(F32)
    tri_t = jnp.where(ri <= ci, 1.0, 0.0).astype(F32)
    eye = jnp.where(ri == ci, 1.0, 0.0).astype(F32)
    onorm = onorm_ref[...]

    for c in range(rb // CHUNK):
        rows = slice(c * CHUNK, (c + 1) * CHUNK)
        gc_tm = jnp.dot(tri, g_tm[rows, :], precision=lax.Precision.HIGHEST, preferred_element_type=F32)
        gc_t = jnp.dot(g_t[:, rows], tri_t, precision=lax.Precision.HIGHEST, preferred_element_type=F32)
        for h in range(hb):
            lane = GDN_HEADS + hg * hb + h
            hl = hg * hb + h
            sel_l = lax.broadcasted_iota(jnp.int32, (CHUNK, 128), 1)
            gcol = jnp.sum(jnp.where(sel_l == lane, gc_tm, 0.0), -1, keepdims=True)
            bcol = jnp.sum(jnp.where(sel_l == hl, beta_tm[rows, :], 0.0), -1, keepdims=True)
            sel_s = lax.broadcasted_iota(jnp.int32, (32, CHUNK), 0)
            grow = jnp.sum(jnp.where(sel_s == lane, gc_t, 0.0), 0, keepdims=True)
            glast = gcol[CHUNK - 1:CHUNK, :]

            q = q_ref[0, h, rows, :]
            k = k_ref[0, h, rows, :]
            v = v_ref[0, h, rows, :]
            egc = jnp.exp(gcol)
            kb = k * bcol
            vb = v * bcol
            decay = jnp.exp(jnp.where(causal, gcol - grow, -jnp.inf))
            kq = _dot_nt(jnp.concatenate([kb, q], 0).astype(BF16), k.astype(BF16))
            lmat = jnp.where(strict, kq[:CHUNK] * decay, 0.0)
            amat = kq[CHUNK:] * decay
            m = -lmat
            inv = eye + m
            for _ in range(int(round(math.log2(CHUNK))) - 1):
                mb = m.astype(BF16)
                m = _dot(mb, mb)
                inv = _dot(inv.astype(BF16), (eye + m).astype(BF16))
            uw = _dot(inv.astype(BF16), jnp.concatenate([vb, kb * egc], 1).astype(BF16))
            st = state_s[h]
            wq = _dot(jnp.concatenate([uw[:, GDN_DV:], q * egc], 0).astype(BF16), st.astype(BF16))
            v_new = uw[:, :GDN_DV] - wq[:CHUNK]
            vnb = v_new.astype(BF16)
            o = wq[CHUNK:] + _dot(amat.astype(BF16), vnb)
            kd = k * jnp.exp(glast - gcol)
            state_s[h] = st * jnp.exp(glast) + _dot_tn(kd.astype(BF16), vnb)
            on = o * lax.rsqrt(jnp.mean(o * o, -1, keepdims=True) + RMS_EPS) * onorm
            zz = z_ref[0, rows, h * GDN_DV:(h + 1) * GDN_DV]
            o_ref[0, rows, h * GDN_DV:(h + 1) * GDN_DV] = (on * (zz * _sigmoid(zz))).astype(o_ref.dtype)


def _gdn(qkv, main, small, small_t, a_log, dt_bias, o_norm):
    b, _, s, _ = qkv.shape
    hb = GDN_HEAD_BLOCK
    rb = _tile(s, GDN_ROW_BLOCK)
    ng = GDN_HEADS // hb
    zoff = 3 * GDN_W // (hb * 128)
    pad = lambda a, lo: jnp.zeros((128,), F32).at[lo:lo + GDN_HEADS].set(a.astype(F32))
    arow = pad(a_log, GDN_HEADS).reshape(1, 128)
    dtrow = pad(dt_bias, GDN_HEADS).reshape(1, 128)
    acol = arow[0, :32].reshape(32, 1)
    dtcol = dtrow[0, :32].reshape(32, 1)
    kern = functools.partial(_gdn_kernel, hb=hb, rb=rb)
    return pl.pallas_call(
        kern,
        grid=(b, ng, s // rb),
        in_specs=[
            pl.BlockSpec((1, hb, rb, 128), lambda bi, g, r: (bi, g, r, 0)),
            pl.BlockSpec((1, hb, rb, 128), lambda bi, g, r: (bi, ng + g, r, 0)),
            pl.BlockSpec((1, hb, rb, 128), lambda bi, g, r: (bi, 2 * ng + g, r, 0)),
            pl.BlockSpec((1, rb, hb * 128), lambda bi, g, r: (bi, r, zoff + g)),
            pl.BlockSpec((1, rb, 128), lambda bi, g, r: (bi, r, SMALL_GATE_BLOCK)),
            pl.BlockSpec((1, 32, rb), lambda bi, g, r: (bi, 0, r)),
            pl.BlockSpec((1, 128), lambda bi, g, r: (0, 0)),
            pl.BlockSpec((1, 128), lambda bi, g, r: (0, 0)),
            pl.BlockSpec((32, 1), lambda bi, g, r: (0, 0)),
            pl.BlockSpec((32, 1), lambda bi, g, r: (0, 0)),
            pl.BlockSpec((1, 128), lambda bi, g, r: (0, 0)),
        ],
        out_specs=pl.BlockSpec((1, rb, hb * 128), lambda bi, g, r: (bi, r, g)),
        out_shape=jax.ShapeDtypeStruct((b, s, GDN_W), BF16),
        scratch_shapes=[pltpu.VMEM((hb, GDN_DK, GDN_DV), F32)],
        compiler_params=_params("parallel", "parallel", "arbitrary"),
        name="gdn_delta_rule",
    )(qkv, qkv, qkv, main, small, small_t, arow, dtrow, acol, dtcol, o_norm.reshape(1, GDN_DV).astype(F32))


def _merge_kernel(om_ref, og_ref, gm_ref, gg_ref, wm_ref, wg_ref, o_ref):
    ym = _dot(om_ref[...], wm_ref[...])
    yg = _dot(og_ref[...], wg_ref[...])
    o_ref[...] = (_sigmoid(gm_ref[...]) * ym + _sigmoid(gg_ref[...]) * yg).astype(o_ref.dtype)


def _merge(o_mla, o_gdn, main, w_mla, w_gdn):
    t, d = o_mla.shape
    tm, tn = _tile(t, 512), _tile(d, 1024)
    goff = (4 * GDN_W) // tn
    nn = d // tn
    return pl.pallas_call(
        _merge_kernel,
        grid=(nn, t // tm),
        in_specs=[
            pl.BlockSpec((tm, d), lambda j, i: (i, 0)),
            pl.BlockSpec((tm, d), lambda j, i: (i, 0)),
            pl.BlockSpec((tm, tn), lambda j, i: (i, goff + j)),
            pl.BlockSpec((tm, tn), lambda j, i: (i, goff + nn + j)),
            pl.BlockSpec((d, tn), lambda j, i: (0, j)),
            pl.BlockSpec((d, tn), lambda j, i: (0, j)),
        ],
        out_specs=pl.BlockSpec((tm, tn), lambda j, i: (i, j)),
        out_shape=jax.ShapeDtypeStruct((t, d), BF16),
        compiler_params=_params("parallel", "parallel"),
        name="mixer_merge",
    )(o_mla, o_gdn, main, main, w_mla, w_gdn)


def _proj_ln_kernel(a_ref, w_ref, x_ref, g_ref, b_ref, o_ref, obf_ref):
    y = DN_ALPHA * x_ref[...] + _dot(a_ref[...], w_ref[...])
    out = _layer_norm(y, g_ref[...], b_ref[...])
    o_ref[...] = out
    obf_ref[...] = out.astype(BF16)


def _proj_ln(a, w, x, g, bias):
    t, k = a.shape
    d = w.shape[1]
    tm = _tile(t, 512)
    return pl.pallas_call(
        _proj_ln_kernel,
        grid=(t // tm,),
        in_specs=[
            pl.BlockSpec((tm, k), lambda i: (i, 0)),
            pl.BlockSpec((k, d), lambda i: (0, 0)),
            pl.BlockSpec((tm, d), lambda i: (i, 0)),
            pl.BlockSpec((1, d), lambda i: (0, 0)),
            pl.BlockSpec((1, d), lambda i: (0, 0)),
        ],
        out_specs=[pl.BlockSpec((tm, d), lambda i: (i, 0)), pl.BlockSpec((tm, d), lambda i: (i, 0))],
        out_shape=[jax.ShapeDtypeStruct((t, d), F32), jax.ShapeDtypeStruct((t, d), BF16)],
        compiler_params=_params("parallel"),
        name="proj_residual_ln",
    )(a, w, x, g.reshape(1, d), bias.reshape(1, d))


def _xattn_kernel(xbf_ref, x_ref, wq_ref, kv_ref, wo_ref, g_ref, b_ref, o_ref, obf_ref):
    q = _dot(xbf_ref[0], wq_ref[...])
    scale = X_HEAD_DIM ** -0.5
    outs = []
    for h in range(X_HEADS):
        qh = q[:, h * X_HEAD_DIM:(h + 1) * X_HEAD_DIM].astype(BF16)
        kh = kv_ref[0, :, h * X_HEAD_DIM:(h + 1) * X_HEAD_DIM]
        vh = kv_ref[0, :, X_W + h * X_HEAD_DIM:X_W + (h + 1) * X_HEAD_DIM]
        s = _dot_nt(qh, kh) * scale
        e = jnp.exp(s - jnp.max(s, -1, keepdims=True))
        p = e / jnp.sum(e, -1, keepdims=True)
        outs.append(_dot(p.astype(BF16), vh))
    o = jnp.concatenate(outs, -1).astype(BF16)
    y = DN_ALPHA * x_ref[0] + _dot(o, wo_ref[...])
    out = _layer_norm(y, g_ref[...], b_ref[...])
    o_ref[0] = out
    obf_ref[0] = out.astype(BF16)


def _xattn(x_bf, x, wq, kv, wo, g, bias):
    b, s, d = x.shape
    tm = _tile(s, 512)
    ml = kv.shape[1]
    return pl.pallas_call(
        _xattn_kernel,
        grid=(b, s // tm),
        in_specs=[
            pl.BlockSpec((1, tm, d), lambda bi, i: (bi, i, 0)),
            pl.BlockSpec((1, tm, d), lambda bi, i: (bi, i, 0)),
            pl.BlockSpec((d, X_W), lambda bi, i: (0, 0)),
            pl.BlockSpec((1, ml, 2 * X_W), lambda bi, i: (bi, 0, 0)),
            pl.BlockSpec((X_W, d), lambda bi, i: (0, 0)),
            pl.BlockSpec((1, d), lambda bi, i: (0, 0)),
            pl.BlockSpec((1, d), lambda bi, i: (0, 0)),
        ],
        out_specs=[pl.BlockSpec((1, tm, d), lambda bi, i: (bi, i, 0)),
                   pl.BlockSpec((1, tm, d), lambda bi, i: (bi, i, 0))],
        out_shape=[jax.ShapeDtypeStruct((b, s, d), F32), jax.ShapeDtypeStruct((b, s, d), BF16)],
        compiler_params=_params("parallel", "parallel"),
        name="memory_cross_attention",
    )(x_bf, x, wq, kv, wo, g.reshape(1, d), bias.reshape(1, d))


def _first_argmax(vals, iota, n):
    m = jnp.max(vals, 0, keepdims=True)
    idx = jnp.min(jnp.where(vals == m, iota, n), 0, keepdims=True)
    return m, idx


def _router_kernel(x_ref, wr_ref, bias_ref, idx_ref, gate_ref, rank_ref, cnt_ref, carry_s, *, tm):
    i = pl.program_id(0)

    @pl.when(i == 0)
    def _():
        carry_s[...] = jnp.zeros(carry_s.shape, F32)

    logits = lax.dot_general(wr_ref[...], x_ref[...], (((1,), (1,)), ((), ())),
                             precision=lax.Precision.HIGHEST, preferred_element_type=F32)
    scores = _sigmoid(logits)
    sel = scores + bias_ref[...]
    epg = EXPERTS_PER_GROUP
    io8 = lax.broadcasted_iota(jnp.int32, (epg, tm), 0)
    best_score = None
    for g in range(N_GROUPS):
        sg = sel[g * epg:(g + 1) * epg, :]
        m1, i1 = _first_argmax(sg, io8, epg)
        m2 = jnp.max(jnp.where(io8 == i1, -jnp.inf, sg), 0, keepdims=True)
        gs = m1 + m2
        if g == 0:
            best_score, best = gs, jnp.zeros((1, tm), jnp.int32)
            in_sel, in_sc = sg, scores[0:epg, :]
        else:
            better = gs > best_score
            best_score = jnp.where(better, gs, best_score)
            best = jnp.where(better, g, best)
            in_sel = jnp.where(better, sg, in_sel)
            in_sc = jnp.where(better, scores[g * epg:(g + 1) * epg, :], in_sc)
    _, l1 = _first_argmax(in_sel, io8, epg)
    _, l2 = _first_argmax(jnp.where(io8 == l1, -jnp.inf, in_sel), io8, epg)
    g1 = jnp.sum(jnp.where(io8 == l1, in_sc, 0.0), 0, keepdims=True)
    g2 = jnp.sum(jnp.where(io8 == l2, in_sc, 0.0), 0, keepdims=True)
    e1 = best * epg + l1
    e2 = best * epg + l2
    idx_ref[0:1, :] = e1
    idx_ref[1:2, :] = e2
    gsum = g1 + g2
    gate_ref[0:1, :] = g1 / gsum
    gate_ref[1:2, :] = g2 / gsum

    ioe = lax.broadcasted_iota(jnp.int32, (N_EXPERTS, tm), 0)
    oh1 = ioe == e1
    oh2 = ioe == e2
    onehot = jnp.where(oh1, 1.0, 0.0) + jnp.where(oh2, 1.0, 0.0)
    rr = lax.broadcasted_iota(jnp.int32, (tm, tm), 0)
    cc = lax.broadcasted_iota(jnp.int32, (tm, tm), 1)
    upper = jnp.where(rr < cc, 1.0, 0.0).astype(BF16)
    tot = carry_s[...] + _dot(onehot.astype(BF16), upper)
    rank_ref[0:1, :] = jnp.sum(jnp.where(oh1, tot, 0.0), 0, keepdims=True).astype(jnp.int32)
    rank_ref[1:2, :] = jnp.sum(jnp.where(oh2, tot, 0.0), 0, keepdims=True).astype(jnp.int32)
    new = carry_s[...] + jnp.sum(onehot, -1, keepdims=True)
    carry_s[...] = new
    cnt_ref[...] = jnp.broadcast_to(new, cnt_ref.shape)


def _router(x, w_router_t, bias):
    t, d = x.shape
    tm = _tile(t, 512)
    return pl.pallas_call(
        functools.partial(_router_kernel, tm=tm),
        grid=(t // tm,),
        in_specs=[pl.BlockSpec((tm, d), lambda i: (i, 0)),
                  pl.BlockSpec((N_EXPERTS, d), lambda i: (0, 0)),
                  pl.BlockSpec((N_EXPERTS, 1), lambda i: (0, 0))],
        out_specs=[pl.BlockSpec((TOP_K, tm), lambda i: (0, i)),
                   pl.BlockSpec((TOP_K, tm), lambda i: (0, i)),
                   pl.BlockSpec((TOP_K, tm), lambda i: (0, i)),
                   pl.BlockSpec((N_EXPERTS, 128), lambda i: (0, 0))],
        out_shape=[jax.ShapeDtypeStruct((TOP_K, t), jnp.int32),
                   jax.ShapeDtypeStruct((TOP_K, t), F32),
                   jax.ShapeDtypeStruct((TOP_K, t), jnp.int32),
                   jax.ShapeDtypeStruct((N_EXPERTS, 128), F32)],
        scratch_shapes=[pltpu.VMEM((N_EXPERTS, 1), F32)],
        compiler_params=_params("arbitrary"),
        name="moe_router",
    )(x, w_router_t, bias)


def _dispatch_kernel(dest_ref, x_ref, xs_in_ref, xs_ref, sem, *, tm, t_total):
    del xs_in_ref
    base = pl.program_id(0) * tm

    def issue(t, carry):
        for k in range(TOP_K):
            d = dest_ref[k * t_total + base + t]
            pltpu.make_async_copy(x_ref.at[pl.ds(t, 1), :], xs_ref.at[pl.ds(d, 1), :], sem).start()
        return carry

    lax.fori_loop(0, tm, issue, 0)

    def drain(t, carry):
        for k in range(TOP_K):
            pltpu.make_async_copy(x_ref.at[pl.ds(0, 1), :], xs_ref.at[pl.ds(0, 1), :], sem).wait()
        return carry

    lax.fori_loop(0, tm, drain, 0)


def _dispatch(dest_flat, x, n_slots):
    t, d = x.shape
    tm = _tile(t, 256)
    xs0 = jnp.zeros((n_slots, d), x.dtype)
    return pl.pallas_call(
        functools.partial(_dispatch_kernel, tm=tm, t_total=t),
        grid_spec=pltpu.PrefetchScalarGridSpec(
            num_scalar_prefetch=1,
            grid=(t // tm,),
            in_specs=[pl.BlockSpec((tm, d), lambda i, dest: (i, 0)),
                      pl.BlockSpec(memory_space=pl.ANY)],
            out_specs=pl.BlockSpec(memory_space=pl.ANY),
            scratch_shapes=[pltpu.SemaphoreType.DMA(())],
        ),
        out_shape=jax.ShapeDtypeStruct((n_slots, d), x.dtype),
        input_output_aliases={2: 0},
        compiler_params=_params("arbitrary"),
        name="moe_dispatch",
    )(dest_flat, x, xs0)


def _expert_kernel(be_ref, nu_ref, xs_ref, wg_ref, wu_ref, wd_ref, ys_ref):
    i = pl.program_id(0)

    @pl.when(i < nu_ref[0])
    def _():
        xb = xs_ref[...].astype(BF16)
        hg = _dot(xb, wg_ref[0])
        hu = _dot(xb, wu_ref[0])
        h = (hg * _sigmoid(hg)) * hu
        ys_ref[...] = _dot(h.astype(BF16), wd_ref[0])

    @pl.when(i >= nu_ref[0])
    def _():
        ys_ref[...] = jnp.zeros(ys_ref.shape, F32)


def _experts(block_expert, n_used, xs, wg, wu, wd):
    n_slots, d = xs.shape
    blk = MOE_BLOCK
    n_blocks = n_slots // blk
    de = wg.shape[-1]
    return pl.pallas_call(
        _expert_kernel,
        grid_spec=pltpu.PrefetchScalarGridSpec(
            num_scalar_prefetch=2,
            grid=(n_blocks,),
            in_specs=[
                pl.BlockSpec((blk, d), lambda i, be, nu: (jnp.minimum(i, nu[0] - 1), 0)),
                pl.BlockSpec((1, d, de), lambda i, be, nu: (be[i], 0, 0)),
                pl.BlockSpec((1, d, de), lambda i, be, nu: (be[i], 0, 0)),
                pl.BlockSpec((1, de, d), lambda i, be, nu: (be[i], 0, 0)),
            ],
            out_specs=pl.BlockSpec((blk, d), lambda i, be, nu: (i, 0)),
        ),
        out_shape=jax.ShapeDtypeStruct((n_slots, d), F32),
        compiler_params=_params("arbitrary"),
        name="moe_experts",
    )(block_expert, n_used, xs, wg, wu, wd)


def _combine_kernel(dest_ref, ys_ref, x_ref, gate_ref, g_ref, b_ref, o_ref, obf_ref, buf_s, sem, *, tm, t_total):
    base = pl.program_id(0) * tm

    def issue(t, carry):
        for k in range(TOP_K):
            d = dest_ref[k * t_total + base + t]
            pltpu.make_async_copy(ys_ref.at[pl.ds(d, 1), :], buf_s.at[k, pl.ds(t, 1), :], sem).start()
        return carry

    lax.fori_loop(0, tm, issue, 0)

    def drain(t, carry):
        for k in range(TOP_K):
            pltpu.make_async_copy(ys_ref.at[pl.ds(0, 1), :], buf_s.at[0, pl.ds(0, 1), :], sem).wait()
        return carry

    lax.fori_loop(0, tm, drain, 0)
    gate = gate_ref[...]
    y = gate[:, 0:1] * buf_s[0] + gate[:, 1:2] * buf_s[1]
    out = _layer_norm(DN_ALPHA * x_ref[...] + y, g_ref[...], b_ref[...])
    o_ref[...] = out
    obf_ref[...] = out.astype(BF16)


def _combine(dest_flat, ys, x, gate_tk, g, bias):
    t, d = x.shape
    tm = _tile(t, 256)
    return pl.pallas_call(
        functools.partial(_combine_kernel, tm=tm, t_total=t),
        grid_spec=pltpu.PrefetchScalarGridSpec(
            num_scalar_prefetch=1,
            grid=(t // tm,),
            in_specs=[
                pl.BlockSpec(memory_space=pl.ANY),
                pl.BlockSpec((tm, d), lambda i, dest: (i, 0)),
                pl.BlockSpec((tm, TOP_K), lambda i, dest: (i, 0)),
                pl.BlockSpec((1, d), lambda i, dest: (0, 0)),
                pl.BlockSpec((1, d), lambda i, dest: (0, 0)),
            ],
            out_specs=[pl.BlockSpec((tm, d), lambda i, dest: (i, 0)),
                       pl.BlockSpec((tm, d), lambda i, dest: (i, 0))],
            scratch_shapes=[pltpu.VMEM((TOP_K, tm, d), F32), pltpu.SemaphoreType.DMA(())],
        ),
        out_shape=[jax.ShapeDtypeStruct((t, d), F32), jax.ShapeDtypeStruct((t, d), BF16)],
        compiler_params=_params("arbitrary"),
        name="moe_combine_ln",
    )(dest_flat, ys, x, gate_tk, g.reshape(1, d), bias.reshape(1, d))


def _moe(x, w_router_t, router_bias, wg, wu, wd, g, bias):
    t, d = x.shape
    idx, gate, rank, cnt = _router(x, w_router_t, router_bias)
    blk = MOE_BLOCK
    n_blocks = t * TOP_K // blk + N_EXPERTS
    counts = cnt[:, 0].astype(jnp.int32)
    padded = (counts + blk - 1) // blk * blk
    padded_end = jnp.cumsum(padded)
    padded_start = padded_end - padded
    dest = (padded_start[idx] + rank).reshape(-1).astype(jnp.int32)
    n_used = (padded_end[-1] // blk).astype(jnp.int32)
    blocks = jnp.minimum(jnp.arange(n_blocks, dtype=jnp.int32), n_used - 1)
    block_expert = jnp.minimum(jnp.searchsorted(padded_end, blocks * blk, side='right'),
                               N_EXPERTS - 1).astype(jnp.int32)
    xs = _dispatch(dest, x, n_blocks * blk)
    ys = _experts(block_expert, n_used.reshape(1), xs, wg, wu, wd)
    return _combine(dest, ys, x, gate.T, g, bias)


def _in_proj_weights(w_in):
    sizes = (Q_LORA, KV_LORA, QK_ROPE, GDN_W, GDN_W, GDN_W, GDN_W, GDN_HEADS, GDN_HEADS, D_MODEL, D_MODEL)
    pts = np.cumsum(sizes)[:-1].tolist()
    c_q, c_kv, k_rope, g_q, g_k, g_v, g_z, g_b, g_a, gate_mla, gate_gdn = jnp.split(w_in, pts, axis=-1)
    half = QK_ROPE // 2
    k_rope_sw = jnp.concatenate([k_rope[:, half:], k_rope[:, :half]], -1)
    main = jnp.concatenate([g_q, g_k, g_v, g_z, gate_mla, gate_gdn], -1)
    used = Q_LORA + KV_LORA + 2 * QK_ROPE + 2 * GDN_HEADS
    small = jnp.concatenate([c_q, c_kv, k_rope, k_rope_sw, g_b, g_a,
                             jnp.zeros((w_in.shape[0], SMALL_W - used), w_in.dtype)], -1)
    return main.astype(BF16), small.astype(BF16)


def _mla_weights(w_uq, w_ukv):
    half = QK_ROPE // 2
    wq = w_uq.reshape(Q_LORA, MLA_HEADS, QK_DIM).transpose(1, 0, 2)
    pe = wq[..., QK_NOPE:]
    pe_sw = jnp.concatenate([pe[..., half:], pe[..., :half]], -1)
    wq = jnp.concatenate([wq, pe_sw], -1).astype(BF16)
    wkv = w_ukv.reshape(KV_LORA, MLA_HEADS, QK_NOPE + V_HEAD).transpose(1, 0, 2).astype(BF16)
    return wq, wkv


def _rope_table(positions):
    half = QK_ROPE // 2
    inv_freq = jnp.power(ROPE_BASE, -jnp.arange(half, dtype=F32) / half)
    ang = positions.astype(F32)[..., None] * inv_freq
    cos, sin = jnp.cos(ang), jnp.sin(ang)
    return jnp.concatenate([cos, cos, -sin, sin], -1)


def kernel(x, mem, positions, w_in, mla_q_norm, mla_kv_norm, w_uq, w_ukv, gdn_conv, gdn_a_log, gdn_dt_bias, gdn_o_norm, w_mla_proj, w_gdn_proj, w_mix_out, ln_mix_g, ln_mix_b, w_xq, w_xkv, w_xo, ln_x_g, ln_x_b, w_router, router_bias, w_gate, w_up, w_down, ln_moe_g, ln_moe_b):
    b, s, d = x.shape
    t = b * s
    cs = _rope_table(positions)
    mem_bf = mem.reshape(-1, d).astype(BF16)
    w_router_t = w_router.T.astype(F32)
    rbias = router_bias.reshape(N_EXPERTS, 1).astype(F32)
    x_bf = x.astype(BF16)
    for l in range(w_in.shape[0]):
        w_main, w_small = _in_proj_weights(w_in[l])
        wq, wkv = _mla_weights(w_uq[l], w_ukv[l])
        xt_bf = x_bf.reshape(t, d)
        main = _matmul(xt_bf, w_main, F32, 1024, 1024, "in_proj_main").reshape(b, s, MAIN_W)
        small = _matmul(xt_bf, w_small, F32, 1024, SMALL_W, "in_proj_small").reshape(b, s, SMALL_W)

        q, k, v = _mla_prep(small, cs, mla_q_norm[l].reshape(1, -1), mla_kv_norm[l].reshape(1, -1), wq, wkv)
        o_mla = _attention(q, k, v)

        qkv = _gdn_conv(main, gdn_conv[l])
        small_t = small[:, :, SMALL_GATE_COL:SMALL_GATE_COL + 2 * GDN_HEADS].transpose(0, 2, 1)
        o_gdn = _gdn(qkv, main, small, small_t, gdn_a_log[l], gdn_dt_bias[l], gdn_o_norm[l])

        merged = _merge(o_mla.reshape(t, d), o_gdn.reshape(t, d), main.reshape(t, MAIN_W),
                        w_mla_proj[l].astype(BF16), w_gdn_proj[l].astype(BF16))
        x1, x1_bf = _proj_ln(merged, w_mix_out[l].astype(BF16), x.reshape(t, d), ln_mix_g[l], ln_mix_b[l])

        kv = _matmul(mem_bf, w_xkv[l].astype(BF16), BF16, 1024, 1024, "mem_kv").reshape(b, -1, 2 * X_W)
        x2, x2_bf = _xattn(x1_bf.reshape(b, s, d), x1.reshape(b, s, d), w_xq[l].astype(BF16), kv,
                           w_xo[l].astype(BF16), ln_x_g[l], ln_x_b[l])

        x3, x3_bf = _moe(x2.reshape(t, d), w_router_t, rbias, w_gate[l].astype(BF16), w_up[l].astype(BF16),
                         w_down[l].astype(BF16), ln_moe_g[l], ln_moe_b[l])
        x, x_bf = x3.reshape(b, s, d), x3_bf.reshape(b, s, d)
    return x
```

```python
import functools
import math

import jax
import jax.numpy as jnp
import numpy as np
from jax import lax
from jax.experimental import pallas as pl
from jax.experimental.pallas import tpu as pltpu

D_MODEL = 2048
DEPTH = 2
MLA_HEADS = 16
Q_LORA = 512
KV_LORA = 512
QK_NOPE = 128
QK_ROPE = 64
V_HEAD = 128
QK_DIM = QK_NOPE + QK_ROPE
ROPE_BASE = 10000.0
GDN_HEADS = 16
GDN_DK = 128
GDN_DV = 128
CONV_K = 4
CHUNK = 64
GDN_W = GDN_HEADS * GDN_DK
X_HEADS = 4
X_HEAD_DIM = 128
X_W = X_HEADS * X_HEAD_DIM
N_EXPERTS = 64
N_GROUPS = 8
EXPERTS_PER_GROUP = N_EXPERTS // N_GROUPS
TOP_K = 2
D_EXPERT = 512
DN_ALPHA = (2.0 * DEPTH) ** 0.25
LN_EPS = 1e-5
RMS_EPS = 1e-6

MAIN_W = 3 * GDN_W + GDN_W + 2 * D_MODEL
SMALL_W = 1280
SMALL_ROPE_BLOCK = (Q_LORA + KV_LORA) // 128
SMALL_GATE_COL = Q_LORA + KV_LORA + 2 * QK_ROPE
SMALL_GATE_BLOCK = SMALL_GATE_COL // 128

GDN_HEAD_BLOCK = 4
GDN_ROW_BLOCK = 256
MOE_BLOCK = 256
VMEM_LIMIT = 56 * 1024 * 1024

BF16 = jnp.bfloat16
F32 = jnp.float32


def _tile(n, pref):
    return pref if n % pref == 0 else n


def _params(*sem):
    return pltpu.CompilerParams(dimension_semantics=sem, vmem_limit_bytes=VMEM_LIMIT)


def _sigmoid(x):
    return 1.0 / (1.0 + jnp.exp(-x))


def _softplus(x):
    return jnp.maximum(x, 0.0) + jnp.log1p(jnp.exp(-jnp.abs(x)))


def _layer_norm(y, g, b):
    mu = jnp.mean(y, -1, keepdims=True)
    d = y - mu
    var = jnp.mean(d * d, -1, keepdims=True)
    return d * lax.rsqrt(var + LN_EPS) * g + b


def _dot(a, b):
    return jnp.dot(a, b, preferred_element_type=F32)


def _dot_nt(a, b):
    return lax.dot_general(a, b, (((1,), (1,)), ((), ())), preferred_element_type=F32)


def _dot_tn(a, b):
    return lax.dot_general(a, b, (((0,), (0,)), ((), ())), preferred_element_type=F32)


def _mm_kernel(a_ref, w_ref, o_ref):
    o_ref[...] = _dot(a_ref[...], w_ref[...]).astype(o_ref.dtype)


def _matmul(a, w, out_dtype, tm, tn, name):
    m, k = a.shape
    n = w.shape[1]
    tm, tn = _tile(m, tm), _tile(n, tn)
    return pl.pallas_call(
        _mm_kernel,
        grid=(m // tm, n // tn),
        in_specs=[pl.BlockSpec((tm, k), lambda i, j: (i, 0)),
                  pl.BlockSpec((k, tn), lambda i, j: (0, j))],
        out_specs=pl.BlockSpec((tm, tn), lambda i, j: (i, j)),
        out_shape=jax.ShapeDtypeStruct((m, n), out_dtype),
        compiler_params=_params("parallel", "parallel"),
        name=name,
    )(a, w)


def _mla_prep_kernel(cq_ref, ckv_ref, kr_ref, cs_ref, qn_ref, kvn_ref, wq_ref, wkv_ref,
                     q_ref, k_ref, v_ref, cqn_s, ckvn_s, kpe_s):
    h = pl.program_id(2)

    @pl.when(h == 0)
    def _():
        cq = cq_ref[0]
        cqn_s[...] = (cq * lax.rsqrt(jnp.mean(cq * cq, -1, keepdims=True) + RMS_EPS) * qn_ref[...]).astype(BF16)
        ckv = ckv_ref[0]
        ckvn_s[...] = (ckv * lax.rsqrt(jnp.mean(ckv * ckv, -1, keepdims=True) + RMS_EPS) * kvn_ref[...]).astype(BF16)
        u = kr_ref[0] * cs_ref[0]
        kpe_s[...] = (u + pltpu.roll(u, QK_ROPE, 1)).astype(BF16)

    rq = _dot(cqn_s[...], wq_ref[0])
    t = rq[:, QK_NOPE:] * cs_ref[0]
    rot = t + pltpu.roll(t, QK_ROPE, 1)
    q_ref[0, 0, :, :QK_NOPE] = rq[:, :QK_NOPE].astype(BF16)
    q_ref[0, 0, :, QK_NOPE:] = rot[:, :QK_ROPE].astype(BF16)
    rkv = _dot(ckvn_s[...], wkv_ref[0])
    k_ref[0, 0, :, :QK_NOPE] = rkv[:, :QK_NOPE].astype(BF16)
    k_ref[0, 0, :, QK_NOPE:] = kpe_s[:, :QK_ROPE]
    v_ref[0, 0, :, :V_HEAD] = rkv[:, QK_NOPE:].astype(BF16)
    v_ref[0, 0, :, V_HEAD:] = jnp.ones((rkv.shape[0], V_HEAD), BF16)


def _mla_prep(small, cs, q_norm, kv_norm, wq, wkv):
    b, s, _ = small.shape
    ts = _tile(s, 512)
    hq = MLA_HEADS
    return pl.pallas_call(
        _mla_prep_kernel,
        grid=(b, s // ts, hq),
        in_specs=[
            pl.BlockSpec((1, ts, Q_LORA), lambda bi, si, h: (bi, si, 0)),
            pl.BlockSpec((1, ts, KV_LORA), lambda bi, si, h: (bi, si, 1)),
            pl.BlockSpec((1, ts, 128), lambda bi, si, h: (bi, si, SMALL_ROPE_BLOCK)),
            pl.BlockSpec((1, ts, 128), lambda bi, si, h: (bi, si, 0)),
            pl.BlockSpec((1, Q_LORA), lambda bi, si, h: (0, 0)),
            pl.BlockSpec((1, KV_LORA), lambda bi, si, h: (0, 0)),
            pl.BlockSpec((1, Q_LORA, 256), lambda bi, si, h: (h, 0, 0)),
            pl.BlockSpec((1, KV_LORA, 256), lambda bi, si, h: (h, 0, 0)),
        ],
        out_specs=[
            pl.BlockSpec((1, 1, ts, QK_DIM), lambda bi, si, h: (bi, h, si, 0)),
            pl.BlockSpec((1, 1, ts, QK_DIM), lambda bi, si, h: (bi, h, si, 0)),
            pl.BlockSpec((1, 1, ts, 2 * V_HEAD), lambda bi, si, h: (bi, h, si, 0)),
        ],
        out_shape=[
            jax.ShapeDtypeStruct((b, hq, s, QK_DIM), BF16),
            jax.ShapeDtypeStruct((b, hq, s, QK_DIM), BF16),
            jax.ShapeDtypeStruct((b, hq, s, 2 * V_HEAD), BF16),
        ],
        scratch_shapes=[pltpu.VMEM((ts, Q_LORA), BF16), pltpu.VMEM((ts, KV_LORA), BF16),
                        pltpu.VMEM((ts, 128), BF16)],
        compiler_params=_params("parallel", "parallel", "arbitrary"),
        name="mla_prep",
    )(small, small, small, cs, q_norm, kv_norm, wq, wkv)


def _attn_kernel(q_ref, k_ref, v_ref, o_ref, s_s, m_s, acc_s, *, tq, rc, scale):
    i = pl.program_id(2)
    dv2 = acc_s.shape[-1]
    m_s[...] = jnp.full(m_s.shape, -jnp.inf, F32)
    acc_s[...] = jnp.zeros(acc_s.shape, F32)

    def scores(j):
        start = pl.multiple_of(j * tq, tq)
        return _dot_nt(q_ref[0, 0], k_ref[0, 0, pl.ds(start, tq), :])

    def softmax_pv(c, j, nk, masked):
        rows = slice(c * rc, (c + 1) * rc)
        start = pl.multiple_of(j * tq, tq)
        s = s_s[rows, :nk] * scale
        if masked:
            row = lax.broadcasted_iota(jnp.int32, (rc, nk), 0) + c * rc
            col = lax.broadcasted_iota(jnp.int32, (rc, nk), 1)
            s = jnp.where(col <= row, s, -jnp.inf)
        m_prev = m_s[rows, :]
        m_new = jnp.maximum(m_prev, jnp.max(s, -1, keepdims=True))
        alpha = jnp.exp(m_prev - m_new)
        p = jnp.exp(s - jnp.concatenate([m_new] * (nk // 128), -1))
        acc_s[rows, :] = (jnp.concatenate([alpha] * (dv2 // 128), -1) * acc_s[rows, :]
                          + _dot(p.astype(BF16), v_ref[0, 0, pl.ds(start, nk), :]))
        m_s[rows, :] = m_new

    s_s[...] = scores(0)

    def body(j, carry):
        s_next = scores(j + 1)
        for c in range(tq // rc):
            softmax_pv(c, j, tq, False)
        s_s[...] = s_next
        return carry

    lax.fori_loop(0, i, body, 0)
    for c in range(tq // rc):
        softmax_pv(c, i, (c + 1) * rc, True)
    dv = dv2 // 2
    o_ref[0] = (acc_s[:, :dv] / acc_s[:, dv:]).astype(o_ref.dtype)


def _attention(q, k, v):
    b, h, s, dqk = q.shape
    dv2 = v.shape[-1]
    tq = _tile(s, 512)
    rc = _tile(tq, 128)
    kern = functools.partial(_attn_kernel, tq=tq, rc=rc, scale=dqk ** -0.5)
    return pl.pallas_call(
        kern,
        grid=(b, h, s // tq),
        in_specs=[
            pl.BlockSpec((1, 1, tq, dqk), lambda bi, hi, i: (bi, hi, i, 0)),
            pl.BlockSpec((1, 1, s, dqk), lambda bi, hi, i: (bi, hi, 0, 0)),
            pl.BlockSpec((1, 1, s, dv2), lambda bi, hi, i: (bi, hi, 0, 0)),
        ],
        out_specs=pl.BlockSpec((1, tq, dv2 // 2), lambda bi, hi, i: (bi, i, hi)),
        out_shape=jax.ShapeDtypeStruct((b, s, h * dv2 // 2), BF16),
        scratch_shapes=[pltpu.VMEM((tq, tq), F32), pltpu.VMEM((tq, 128), F32), pltpu.VMEM((tq, dv2), F32)],
        compiler_params=_params("parallel", "parallel", "arbitrary"),
        name="mla_attention",
    )(q, k, v)


def _conv_kernel(x_ref, w_ref, o_ref, xx_s, *, ts):
    c = pl.program_id(1)
    si = pl.program_id(2)

    @pl.when(si == 0)
    def _():
        xx_s[0:8, :] = jnp.zeros((8, 128), F32)

    @pl.when(si > 0)
    def _():
        xx_s[0:8, :] = xx_s[ts:ts + 8, :]

    xx_s[8:ts + 8, :] = x_ref[0]
    w = w_ref[...]
    y = w[CONV_K - 1:CONV_K, :] * xx_s[8:ts + 8, :]
    for j in range(CONV_K - 1):
        off = 8 - (CONV_K - 1) + j
        y = y + w[j:j + 1, :] * xx_s[off:off + ts, :]
    y = y * _sigmoid(y)
    yn = y * lax.rsqrt(jnp.sum(y * y, -1, keepdims=True) + RMS_EPS)
    yn = yn * jnp.where(c < GDN_HEADS, GDN_DK ** -0.5, 1.0)
    o_ref[0, 0] = jnp.where(c < 2 * GDN_HEADS, yn, y)


def _gdn_conv(main, w_conv):
    b, s, _ = main.shape
    ts = _tile(s, 1024)
    nc = 3 * GDN_HEADS
    return pl.pallas_call(
        functools.partial(_conv_kernel, ts=ts),
        grid=(b, nc, s // ts),
        in_specs=[pl.BlockSpec((1, ts, 128), lambda bi, c, si: (bi, si, c)),
                  pl.BlockSpec((CONV_K, 128), lambda bi, c, si: (0, c))],
        out_specs=pl.BlockSpec((1, 1, ts, 128), lambda bi, c, si: (bi, c, si, 0)),
        out_shape=jax.ShapeDtypeStruct((b, nc, s, 128), F32),
        scratch_shapes=[pltpu.VMEM((ts + 8, 128), F32)],
        compiler_params=_params("parallel", "parallel", "arbitrary"),
        name="gdn_conv",
    )(main, w_conv)


def _gdn_kernel(q_ref, k_ref, v_ref, z_ref, gate_ref, gatet_ref, arow_ref, dtrow_ref, acol_ref, dtcol_ref,
                onorm_ref, o_ref, state_s, *, hb, rb):
    hg = pl.program_id(1)
    r = pl.program_id(2)

    @pl.when(r == 0)
    def _():
        state_s[...] = jnp.zeros(state_s.shape, F32)

    blk = gate_ref[0]
    beta_tm = _sigmoid(blk)
    g_tm = -jnp.exp(arow_ref[...]) * _softplus(blk + dtrow_ref[...])
    g_t = -jnp.exp(acol_ref[...]) * _softplus(gatet_ref[0] + dtcol_ref[...])

    ri = lax.broadcasted_iota(jnp.int32, (CHUNK, CHUNK), 0)
    ci = lax.broadcasted_iota(jnp.int32, (CHUNK, CHUNK), 1)
    causal = ri >= ci
    strict = ri > ci
    tri = jnp.where(causal, 1.0, 0.0).astype(F32)
    tri_t = jnp.where(ri <= ci, 1.0, 0.0).astype(F32)
    eye = jnp.where(ri == ci, 1.0, 0.0).astype(F32)
    onorm = onorm_ref[...]

    nc = rb // CHUNK
    items = [(c, h) for c in range(nc) for h in range(hb)]
    rows = [slice(c * CHUNK, (c + 1) * CHUNK) for c in range(nc)]
    gc_tm = [jnp.dot(tri, g_tm[rows[c], :], precision=lax.Precision.HIGHEST, preferred_element_type=F32)
             for c in range(nc)]
    gc_t = [jnp.dot(g_t[:, rows[c]], tri_t, precision=lax.Precision.HIGHEST, preferred_element_type=F32)
            for c in range(nc)]
    sel_l = lax.broadcasted_iota(jnp.int32, (CHUNK, 128), 1)
    sel_s = lax.broadcasted_iota(jnp.int32, (32, CHUNK), 0)
    gcol, bcol, grow, glast, egc, kt, kb = {}, {}, {}, {}, {}, {}, {}
    for it in items:
        c, h = it
        lane = GDN_HEADS + hg * hb + h
        gcol[it] = jnp.sum(jnp.where(sel_l == lane, gc_tm[c], 0.0), -1, keepdims=True)
        bcol[it] = jnp.sum(jnp.where(sel_l == hg * hb + h, beta_tm[rows[c], :], 0.0), -1, keepdims=True)
        grow[it] = jnp.sum(jnp.where(sel_s == lane, gc_t[c], 0.0), 0, keepdims=True)
        glast[it] = gcol[it][CHUNK - 1:CHUNK, :]
        egc[it] = jnp.exp(gcol[it])
        k = k_ref[0, h, rows[c], :]
        kt[it] = k.T
        kb[it] = k * bcol[it]
    kq = {it: _dot(jnp.concatenate([kb[it], q_ref[0, it[1], rows[it[0]], :]], 0).astype(BF16),
                   kt[it].astype(BF16)) for it in items}
    amat, m, inv = {}, {}, {}
    for it in items:
        decay = jnp.exp(jnp.where(causal, gcol[it] - grow[it], -jnp.inf))
        amat[it] = (kq[it][CHUNK:] * decay).astype(BF16)
        m[it] = -jnp.where(strict, kq[it][:CHUNK] * decay, 0.0)
        inv[it] = eye + m[it]
    for it in items:
        mb = m[it].astype(BF16)
        m[it] = _dot(mb, mb)
    n_fac = int(round(math.log2(CHUNK))) - 1
    for t in range(n_fac):
        for it in items:
            mb = m[it].astype(BF16)
            if t < n_fac - 1:
                r2 = _dot(jnp.concatenate([inv[it], m[it]], 0).astype(BF16), mb)
                inv[it] = inv[it] + r2[:CHUNK]
                m[it] = r2[CHUNK:]
            else:
                inv[it] = inv[it] + _dot(inv[it].astype(BF16), mb)
    uw = {}
    for it in items:
        c, h = it
        vb = v_ref[0, h, rows[c], :] * bcol[it]
        uw[it] = _dot(inv[it].astype(BF16), jnp.concatenate([vb, kb[it] * egc[it]], 1).astype(BF16))
    for c in range(nc):
        its = [(c, h) for h in range(hb)]
        st = {it: state_s[it[1]] for it in its}
        wq = {it: _dot(jnp.concatenate([uw[it][:, GDN_DV:], q_ref[0, it[1], rows[c], :] * egc[it]], 0).astype(BF16),
                       st[it].astype(BF16)) for it in its}
        vnb = {it: (uw[it][:, :GDN_DV] - wq[it][:CHUNK]).astype(BF16) for it in its}
        o = {it: wq[it][CHUNK:] + _dot(amat[it], vnb[it]) for it in its}
        for it in its:
            h = it[1]
            kdt = kt[it] * jnp.exp(glast[it] - grow[it])
            state_s[h] = st[it] * jnp.exp(glast[it]) + _dot(kdt.astype(BF16), vnb[it])
        for it in its:
            h = it[1]
            on = o[it] * lax.rsqrt(jnp.mean(o[it] * o[it], -1, keepdims=True) + RMS_EPS) * onorm
            zz = z_ref[0, rows[c], h * GDN_DV:(h + 1) * GDN_DV]
            o_ref[0, rows[c], h * GDN_DV:(h + 1) * GDN_DV] = (on * (zz * _sigmoid(zz))).astype(o_ref.dtype)


def _gdn(qkv, main, small, small_t, a_log, dt_bias, o_norm):
    b, _, s, _ = qkv.shape
    hb = GDN_HEAD_BLOCK
    rb = _tile(s, GDN_ROW_BLOCK)
    ng = GDN_HEADS // hb
    zoff = 3 * GDN_W // (hb * 128)
    pad = lambda a, lo: jnp.zeros((128,), F32).at[lo:lo + GDN_HEADS].set(a.astype(F32))
    arow = pad(a_log, GDN_HEADS).reshape(1, 128)
    dtrow = pad(dt_bias, GDN_HEADS).reshape(1, 128)
    acol = arow[0, :32].reshape(32, 1)
    dtcol = dtrow[0, :32].reshape(32, 1)
    kern = functools.partial(_gdn_kernel, hb=hb, rb=rb)
    return pl.pallas_call(
        kern,
        grid=(b, ng, s // rb),
        in_specs=[
            pl.BlockSpec((1, hb, rb, 128), lambda bi, g, r: (bi, g, r, 0)),
            pl.BlockSpec((1, hb, rb, 128), lambda bi, g, r: (bi, ng + g, r, 0)),
            pl.BlockSpec((1, hb, rb, 128), lambda bi, g, r: (bi, 2 * ng + g, r, 0)),
            pl.BlockSpec((1, rb, hb * 128), lambda bi, g, r: (bi, r, zoff + g)),
            pl.BlockSpec((1, rb, 128), lambda bi, g, r: (bi, r, SMALL_GATE_BLOCK)),
            pl.BlockSpec((1, 32, rb), lambda bi, g, r: (bi, 0, r)),
            pl.BlockSpec((1, 128), lambda bi, g, r: (0, 0)),
            pl.BlockSpec((1, 128), lambda bi, g, r: (0, 0)),
            pl.BlockSpec((32, 1), lambda bi, g, r: (0, 0)),
            pl.BlockSpec((32, 1), lambda bi, g, r: (0, 0)),
            pl.BlockSpec((1, 128), lambda bi, g, r: (0, 0)),
        ],
        out_specs=pl.BlockSpec((1, rb, hb * 128), lambda bi, g, r: (bi, r, g)),
        out_shape=jax.ShapeDtypeStruct((b, s, GDN_W), BF16),
        scratch_shapes=[pltpu.VMEM((hb, GDN_DK, GDN_DV), F32)],
        compiler_params=_params("parallel", "parallel", "arbitrary"),
        name="gdn_delta_rule",
    )(qkv, qkv, qkv, main, small, small_t, arow, dtrow, acol, dtcol, o_norm.reshape(1, GDN_DV).astype(F32))


def _merge_kernel(om_ref, og_ref, gm_ref, gg_ref, wm_ref, wg_ref, o_ref):
    ym = _dot(om_ref[...], wm_ref[...])
    yg = _dot(og_ref[...], wg_ref[...])
    o_ref[...] = (_sigmoid(gm_ref[...]) * ym + _sigmoid(gg_ref[...]) * yg).astype(o_ref.dtype)


def _merge(o_mla, o_gdn, main, w_mla, w_gdn):
    t, d = o_mla.shape
    tm, tn = _tile(t, 512), _tile(d, 1024)
    goff = (4 * GDN_W) // tn
    nn = d // tn
    return pl.pallas_call(
        _merge_kernel,
        grid=(nn, t // tm),
        in_specs=[
            pl.BlockSpec((tm, d), lambda j, i: (i, 0)),
            pl.BlockSpec((tm, d), lambda j, i: (i, 0)),
            pl.BlockSpec((tm, tn), lambda j, i: (i, goff + j)),
            pl.BlockSpec((tm, tn), lambda j, i: (i, goff + nn + j)),
            pl.BlockSpec((d, tn), lambda j, i: (0, j)),
            pl.BlockSpec((d, tn), lambda j, i: (0, j)),
        ],
        out_specs=pl.BlockSpec((tm, tn), lambda j, i: (i, j)),
        out_shape=jax.ShapeDtypeStruct((t, d), BF16),
        compiler_params=_params("parallel", "parallel"),
        name="mixer_merge",
    )(o_mla, o_gdn, main, main, w_mla, w_gdn)


def _proj_ln_kernel(a_ref, w_ref, x_ref, g_ref, b_ref, o_ref, obf_ref):
    y = DN_ALPHA * x_ref[...] + _dot(a_ref[...], w_ref[...])
    out = _layer_norm(y, g_ref[...], b_ref[...])
    o_ref[...] = out
    obf_ref[...] = out.astype(BF16)


def _proj_ln(a, w, x, g, bias):
    t, k = a.shape
    d = w.shape[1]
    tm = _tile(t, 512)
    return pl.pallas_call(
        _proj_ln_kernel,
        grid=(t // tm,),
        in_specs=[
            pl.BlockSpec((tm, k), lambda i: (i, 0)),
            pl.BlockSpec((k, d), lambda i: (0, 0)),
            pl.BlockSpec((tm, d), lambda i: (i, 0)),
            pl.BlockSpec((1, d), lambda i: (0, 0)),
            pl.BlockSpec((1, d), lambda i: (0, 0)),
        ],
        out_specs=[pl.BlockSpec((tm, d), lambda i: (i, 0)), pl.BlockSpec((tm, d), lambda i: (i, 0))],
        out_shape=[jax.ShapeDtypeStruct((t, d), F32), jax.ShapeDtypeStruct((t, d), BF16)],
        compiler_params=_params("parallel"),
        name="proj_residual_ln",
    )(a, w, x, g.reshape(1, d), bias.reshape(1, d))


def _xattn_kernel(xbf_ref, x_ref, wq_ref, kv_ref, wo_ref, g_ref, b_ref, o_ref, obf_ref):
    q = _dot(xbf_ref[0], wq_ref[...])
    scale = X_HEAD_DIM ** -0.5
    outs = []
    for h in range(X_HEADS):
        qh = q[:, h * X_HEAD_DIM:(h + 1) * X_HEAD_DIM].astype(BF16)
        kh = kv_ref[0, :, h * X_HEAD_DIM:(h + 1) * X_HEAD_DIM]
        vh = kv_ref[0, :, X_W + h * X_HEAD_DIM:X_W + (h + 1) * X_HEAD_DIM]
        s = _dot_nt(qh, kh) * scale
        e = jnp.exp(s - jnp.max(s, -1, keepdims=True))
        p = e / jnp.sum(e, -1, keepdims=True)
        outs.append(_dot(p.astype(BF16), vh))
    o = jnp.concatenate(outs, -1).astype(BF16)
    y = DN_ALPHA * x_ref[0] + _dot(o, wo_ref[...])
    out = _layer_norm(y, g_ref[...], b_ref[...])
    o_ref[0] = out
    obf_ref[0] = out.astype(BF16)


def _xattn(x_bf, x, wq, kv, wo, g, bias):
    b, s, d = x.shape
    tm = _tile(s, 512)
    ml = kv.shape[1]
    return pl.pallas_call(
        _xattn_kernel,
        grid=(b, s // tm),
        in_specs=[
            pl.BlockSpec((1, tm, d), lambda bi, i: (bi, i, 0)),
            pl.BlockSpec((1, tm, d), lambda bi, i: (bi, i, 0)),
            pl.BlockSpec((d, X_W), lambda bi, i: (0, 0)),
            pl.BlockSpec((1, ml, 2 * X_W), lambda bi, i: (bi, 0, 0)),
            pl.BlockSpec((X_W, d), lambda bi, i: (0, 0)),
            pl.BlockSpec((1, d), lambda bi, i: (0, 0)),
            pl.BlockSpec((1, d), lambda bi, i: (0, 0)),
        ],
        out_specs=[pl.BlockSpec((1, tm, d), lambda bi, i: (bi, i, 0)),
                   pl.BlockSpec((1, tm, d), lambda bi, i: (bi, i, 0))],
        out_shape=[jax.ShapeDtypeStruct((b, s, d), F32), jax.ShapeDtypeStruct((b, s, d), BF16)],
        compiler_params=_params("parallel", "parallel"),
        name="memory_cross_attention",
    )(x_bf, x, wq, kv, wo, g.reshape(1, d), bias.reshape(1, d))


def _first_argmax(vals, iota, n):
    m = jnp.max(vals, 0, keepdims=True)
    idx = jnp.min(jnp.where(vals == m, iota, n), 0, keepdims=True)
    return m, idx


def _router_kernel(x_ref, wr_ref, bias_ref, idx_ref, gate_ref, rank_ref, cnt_ref, carry_s, *, tm):
    i = pl.program_id(0)

    @pl.when(i == 0)
    def _():
        carry_s[...] = jnp.zeros(carry_s.shape, F32)

    logits = lax.dot_general(wr_ref[...], x_ref[...], (((1,), (1,)), ((), ())),
                             precision=lax.Precision.HIGHEST, preferred_element_type=F32)
    scores = _sigmoid(logits)
    sel = scores + bias_ref[...]
    epg = EXPERTS_PER_GROUP
    io8 = lax.broadcasted_iota(jnp.int32, (epg, tm), 0)
    best_score = None
    for g in range(N_GROUPS):
        sg = sel[g * epg:(g + 1) * epg, :]
        m1, i1 = _first_argmax(sg, io8, epg)
        m2 = jnp.max(jnp.where(io8 == i1, -jnp.inf, sg), 0, keepdims=True)
        gs = m1 + m2
        if g == 0:
            best_score, best = gs, jnp.zeros((1, tm), jnp.int32)
            in_sel, in_sc = sg, scores[0:epg, :]
        else:
            better = gs > best_score
            best_score = jnp.where(better, gs, best_score)
            best = jnp.where(better, g, best)
            in_sel = jnp.where(better, sg, in_sel)
            in_sc = jnp.where(better, scores[g * epg:(g + 1) * epg, :], in_sc)
    _, l1 = _first_argmax(in_sel, io8, epg)
    _, l2 = _first_argmax(jnp.where(io8 == l1, -jnp.inf, in_sel), io8, epg)
    g1 = jnp.sum(jnp.where(io8 == l1, in_sc, 0.0), 0, keepdims=True)
    g2 = jnp.sum(jnp.where(io8 == l2, in_sc, 0.0), 0, keepdims=True)
    e1 = best * epg + l1
    e2 = best * epg + l2
    idx_ref[0:1, :] = e1
    idx_ref[1:2, :] = e2
    gsum = g1 + g2
    gate_ref[0:1, :] = g1 / gsum
    gate_ref[1:2, :] = g2 / gsum

    ioe = lax.broadcasted_iota(jnp.int32, (N_EXPERTS, tm), 0)
    oh1 = ioe == e1
    oh2 = ioe == e2
    onehot = jnp.where(oh1, 1.0, 0.0) + jnp.where(oh2, 1.0, 0.0)
    rr = lax.broadcasted_iota(jnp.int32, (tm, tm), 0)
    cc = lax.broadcasted_iota(jnp.int32, (tm, tm), 1)
    upper = jnp.where(rr < cc, 1.0, 0.0).astype(BF16)
    tot = carry_s[...] + _dot(onehot.astype(BF16), upper)
    rank_ref[0:1, :] = jnp.sum(jnp.where(oh1, tot, 0.0), 0, keepdims=True).astype(jnp.int32)
    rank_ref[1:2, :] = jnp.sum(jnp.where(oh2, tot, 0.0), 0, keepdims=True).astype(jnp.int32)
    new = carry_s[...] + jnp.sum(onehot, -1, keepdims=True)
    carry_s[...] = new
    cnt_ref[...] = jnp.broadcast_to(new, cnt_ref.shape)


def _router(x, w_router_t, bias):
    t, d = x.shape
    tm = _tile(t, 512)
    return pl.pallas_call(
        functools.partial(_router_kernel, tm=tm),
        grid=(t // tm,),
        in_specs=[pl.BlockSpec((tm, d), lambda i: (i, 0)),
                  pl.BlockSpec((N_EXPERTS, d), lambda i: (0, 0)),
                  pl.BlockSpec((N_EXPERTS, 1), lambda i: (0, 0))],
        out_specs=[pl.BlockSpec((TOP_K, tm), lambda i: (0, i)),
                   pl.BlockSpec((TOP_K, tm), lambda i: (0, i)),
                   pl.BlockSpec((TOP_K, tm), lambda i: (0, i)),
                   pl.BlockSpec((N_EXPERTS, 128), lambda i: (0, 0))],
        out_shape=[jax.ShapeDtypeStruct((TOP_K, t), jnp.int32),
                   jax.ShapeDtypeStruct((TOP_K, t), F32),
                   jax.ShapeDtypeStruct((TOP_K, t), jnp.int32),
                   jax.ShapeDtypeStruct((N_EXPERTS, 128), F32)],
        scratch_shapes=[pltpu.VMEM((N_EXPERTS, 1), F32)],
        compiler_params=_params("arbitrary"),
        name="moe_router",
    )(x, w_router_t, bias)


def _dispatch_kernel(dest_ref, x_ref, xs_in_ref, xs_ref, sem, *, tm, t_total):
    del xs_in_ref
    base = pl.program_id(0) * tm

    def issue(t, carry):
        for k in range(TOP_K):
            d = dest_ref[k * t_total + base + t]
            pltpu.make_async_copy(x_ref.at[pl.ds(t, 1), :], xs_ref.at[pl.ds(d, 1), :], sem).start()
        return carry

    lax.fori_loop(0, tm, issue, 0)

    def drain(t, carry):
        for k in range(TOP_K):
            pltpu.make_async_copy(x_ref.at[pl.ds(0, 1), :], xs_ref.at[pl.ds(0, 1), :], sem).wait()
        return carry

    lax.fori_loop(0, tm, drain, 0)


def _dispatch(dest_flat, x, n_slots):
    t, d = x.shape
    tm = _tile(t, 256)
    xs0 = jnp.zeros((n_slots, d), x.dtype)
    return pl.pallas_call(
        functools.partial(_dispatch_kernel, tm=tm, t_total=t),
        grid_spec=pltpu.PrefetchScalarGridSpec(
            num_scalar_prefetch=1,
            grid=(t // tm,),
            in_specs=[pl.BlockSpec((tm, d), lambda i, dest: (i, 0)),
                      pl.BlockSpec(memory_space=pl.ANY)],
            out_specs=pl.BlockSpec(memory_space=pl.ANY),
            scratch_shapes=[pltpu.SemaphoreType.DMA(())],
        ),
        out_shape=jax.ShapeDtypeStruct((n_slots, d), x.dtype),
        input_output_aliases={2: 0},
        compiler_params=_params("arbitrary"),
        name="moe_dispatch",
    )(dest_flat, x, xs0)


def _expert_kernel(be_ref, nu_ref, xs_ref, wg_ref, wu_ref, wd_ref, ys_ref):
    i = pl.program_id(0)

    @pl.when(i < nu_ref[0])
    def _():
        xb = xs_ref[...].astype(BF16)
        hg = _dot(xb, wg_ref[0])
        hu = _dot(xb, wu_ref[0])
        h = (hg * _sigmoid(hg)) * hu
        ys_ref[...] = _dot(h.astype(BF16), wd_ref[0])

    @pl.when(i >= nu_ref[0])
    def _():
        ys_ref[...] = jnp.zeros(ys_ref.shape, F32)


def _experts(block_expert, n_used, xs, wg, wu, wd):
    n_slots, d = xs.shape
    blk = MOE_BLOCK
    n_blocks = n_slots // blk
    de = wg.shape[-1]
    return pl.pallas_call(
        _expert_kernel,
        grid_spec=pltpu.PrefetchScalarGridSpec(
            num_scalar_prefetch=2,
            grid=(n_blocks,),
            in_specs=[
                pl.BlockSpec((blk, d), lambda i, be, nu: (jnp.minimum(i, nu[0] - 1), 0)),
                pl.BlockSpec((1, d, de), lambda i, be, nu: (be[i], 0, 0)),
                pl.BlockSpec((1, d, de), lambda i, be, nu: (be[i], 0, 0)),
                pl.BlockSpec((1, de, d), lambda i, be, nu: (be[i], 0, 0)),
            ],
            out_specs=pl.BlockSpec((blk, d), lambda i, be, nu: (i, 0)),
        ),
        out_shape=jax.ShapeDtypeStruct((n_slots, d), F32),
        compiler_params=_params("arbitrary"),
        name="moe_experts",
    )(block_expert, n_used, xs, wg, wu, wd)


def _combine_kernel(dest_ref, ys_ref, x_ref, gate_ref, g_ref, b_ref, o_ref, obf_ref, buf_s, sem, *, tm, t_total):
    base = pl.program_id(0) * tm

    def issue(t, carry):
        for k in range(TOP_K):
            d = dest_ref[k * t_total + base + t]
            pltpu.make_async_copy(ys_ref.at[pl.ds(d, 1), :], buf_s.at[k, pl.ds(t, 1), :], sem).start()
        return carry

    lax.fori_loop(0, tm, issue, 0)

    def drain(t, carry):
        for k in range(TOP_K):
            pltpu.make_async_copy(ys_ref.at[pl.ds(0, 1), :], buf_s.at[0, pl.ds(0, 1), :], sem).wait()
        return carry

    lax.fori_loop(0, tm, drain, 0)
    gate = gate_ref[...]
    y = gate[:, 0:1] * buf_s[0] + gate[:, 1:2] * buf_s[1]
    out = _layer_norm(DN_ALPHA * x_ref[...] + y, g_ref[...], b_ref[...])
    o_ref[...] = out
    obf_ref[...] = out.astype(BF16)


def _combine(dest_flat, ys, x, gate_tk, g, bias):
    t, d = x.shape
    tm = _tile(t, 256)
    return pl.pallas_call(
        functools.partial(_combine_kernel, tm=tm, t_total=t),
        grid_spec=pltpu.PrefetchScalarGridSpec(
            num_scalar_prefetch=1,
            grid=(t // tm,),
            in_specs=[
                pl.BlockSpec(memory_space=pl.ANY),
                pl.BlockSpec((tm, d), lambda i, dest: (i, 0)),
                pl.BlockSpec((tm, TOP_K), lambda i, dest: (i, 0)),
                pl.BlockSpec((1, d), lambda i, dest: (0, 0)),
                pl.BlockSpec((1, d), lambda i, dest: (0, 0)),
            ],
            out_specs=[pl.BlockSpec((tm, d), lambda i, dest: (i, 0)),
                       pl.BlockSpec((tm, d), lambda i, dest: (i, 0))],
            scratch_shapes=[pltpu.VMEM((TOP_K, tm, d), F32), pltpu.SemaphoreType.DMA(())],
        ),
        out_shape=[jax.ShapeDtypeStruct((t, d), F32), jax.ShapeDtypeStruct((t, d), BF16)],
        compiler_params=_params("arbitrary"),
        name="moe_combine_ln",
    )(dest_flat, ys, x, gate_tk, g.reshape(1, d), bias.reshape(1, d))


def _moe(x, w_router_t, router_bias, wg, wu, wd, g, bias):
    t, d = x.shape
    idx, gate, rank, cnt = _router(x, w_router_t, router_bias)
    blk = MOE_BLOCK
    n_blocks = t * TOP_K // blk + N_EXPERTS
    counts = cnt[:, 0].astype(jnp.int32)
    padded = (counts + blk - 1) // blk * blk
    padded_end = jnp.cumsum(padded)
    padded_start = padded_end - padded
    dest = (padded_start[idx] + rank).reshape(-1).astype(jnp.int32)
    n_used = (padded_end[-1] // blk).astype(jnp.int32)
    blocks = jnp.minimum(jnp.arange(n_blocks, dtype=jnp.int32), n_used - 1)
    block_expert = jnp.minimum(jnp.searchsorted(padded_end, blocks * blk, side='right'),
                               N_EXPERTS - 1).astype(jnp.int32)
    xs = _dispatch(dest, x, n_blocks * blk)
    ys = _experts(block_expert, n_used.reshape(1), xs, wg, wu, wd)
    return _combine(dest, ys, x, gate.T, g, bias)


def _in_proj_weights(w_in):
    sizes = (Q_LORA, KV_LORA, QK_ROPE, GDN_W, GDN_W, GDN_W, GDN_W, GDN_HEADS, GDN_HEADS, D_MODEL, D_MODEL)
    pts = np.cumsum(sizes)[:-1].tolist()
    c_q, c_kv, k_rope, g_q, g_k, g_v, g_z, g_b, g_a, gate_mla, gate_gdn = jnp.split(w_in, pts, axis=-1)
    half = QK_ROPE // 2
    k_rope_sw = jnp.concatenate([k_rope[:, half:], k_rope[:, :half]], -1)
    main = jnp.concatenate([g_q, g_k, g_v, g_z, gate_mla, gate_gdn], -1)
    used = Q_LORA + KV_LORA + 2 * QK_ROPE + 2 * GDN_HEADS
    small = jnp.concatenate([c_q, c_kv, k_rope, k_rope_sw, g_b, g_a,
                             jnp.zeros((w_in.shape[0], SMALL_W - used), w_in.dtype)], -1)
    return main.astype(BF16), small.astype(BF16)


def _mla_weights(w_uq, w_ukv):
    half = QK_ROPE // 2
    wq = w_uq.reshape(Q_LORA, MLA_HEADS, QK_DIM).transpose(1, 0, 2)
    pe = wq[..., QK_NOPE:]
    pe_sw = jnp.concatenate([pe[..., half:], pe[..., :half]], -1)
    wq = jnp.concatenate([wq, pe_sw], -1).astype(BF16)
    wkv = w_ukv.reshape(KV_LORA, MLA_HEADS, QK_NOPE + V_HEAD).transpose(1, 0, 2).astype(BF16)
    return wq, wkv


def _rope_table(positions):
    half = QK_ROPE // 2
    inv_freq = jnp.power(ROPE_BASE, -jnp.arange(half, dtype=F32) / half)
    ang = positions.astype(F32)[..., None] * inv_freq
    cos, sin = jnp.cos(ang), jnp.sin(ang)
    return jnp.concatenate([cos, cos, -sin, sin], -1)


def kernel(x, mem, positions, w_in, mla_q_norm, mla_kv_norm, w_uq, w_ukv, gdn_conv, gdn_a_log, gdn_dt_bias, gdn_o_norm, w_mla_proj, w_gdn_proj, w_mix_out, ln_mix_g, ln_mix_b, w_xq, w_xkv, w_xo, ln_x_g, ln_x_b, w_router, router_bias, w_gate, w_up, w_down, ln_moe_g, ln_moe_b):
    b, s, d = x.shape
    t = b * s
    cs = _rope_table(positions)
    mem_bf = mem.reshape(-1, d).astype(BF16)
    w_router_t = w_router.T.astype(F32)
    rbias = router_bias.reshape(N_EXPERTS, 1).astype(F32)
    x_bf = x.astype(BF16)
    for l in range(w_in.shape[0]):
        w_main, w_small = _in_proj_weights(w_in[l])
        wq, wkv = _mla_weights(w_uq[l], w_ukv[l])
        xt_bf = x_bf.reshape(t, d)
        main = _matmul(xt_bf, w_main, F32, 1024, 1024, "in_proj_main").reshape(b, s, MAIN_W)
        small = _matmul(xt_bf, w_small, F32, 1024, SMALL_W, "in_proj_small").reshape(b, s, SMALL_W)

        q, k, v = _mla_prep(small, cs, mla_q_norm[l].reshape(1, -1), mla_kv_norm[l].reshape(1, -1), wq, wkv)
        o_mla = _attention(q, k, v)

        qkv = _gdn_conv(main, gdn_conv[l])
        small_t = small[:, :, SMALL_GATE_COL:SMALL_GATE_COL + 2 * GDN_HEADS].transpose(0, 2, 1)
        o_gdn = _gdn(qkv, main, small, small_t, gdn_a_log[l], gdn_dt_bias[l], gdn_o_norm[l])

        merged = _merge(o_mla.reshape(t, d), o_gdn.reshape(t, d), main.reshape(t, MAIN_W),
                        w_mla_proj[l].astype(BF16), w_gdn_proj[l].astype(BF16))
        x1, x1_bf = _proj_ln(merged, w_mix_out[l].astype(BF16), x.reshape(t, d), ln_mix_g[l], ln_mix_b[l])

        kv = _matmul(mem_bf, w_xkv[l].astype(BF16), BF16, 1024, 1024, "mem_kv").reshape(b, -1, 2 * X_W)
        x2, x2_bf = _xattn(x1_bf.reshape(b, s, d), x1.reshape(b, s, d), w_xq[l].astype(BF16), kv,
                           w_xo[l].astype(BF16), ln_x_g[l], ln_x_b[l])

        x3, x3_bf = _moe(x2.reshape(t, d), w_router_t, rbias, w_gate[l].astype(BF16), w_up[l].astype(BF16),
                         w_down[l].astype(BF16), ln_moe_g[l], ln_moe_b[l])
        x, x_bf = x3.reshape(b, s, d), x3_bf.reshape(b, s, d)
    return x
```

```python
import functools
import math

import jax
import jax.numpy as jnp
import numpy as np
from jax import lax
from jax.experimental import pallas as pl
from jax.experimental.pallas import tpu as pltpu

D_MODEL = 2048
DEPTH = 2
MLA_HEADS = 16
Q_LORA = 512
KV_LORA = 512
QK_NOPE = 128
QK_ROPE = 64
V_HEAD = 128
QK_DIM = QK_NOPE + QK_ROPE
ROPE_BASE = 10000.0
GDN_HEADS = 16
GDN_DK = 128
GDN_DV = 128
CONV_K = 4
CHUNK = 64
GDN_W = GDN_HEADS * GDN_DK
X_HEADS = 4
X_HEAD_DIM = 128
X_W = X_HEADS * X_HEAD_DIM
N_EXPERTS = 64
N_GROUPS = 8
EXPERTS_PER_GROUP = N_EXPERTS // N_GROUPS
TOP_K = 2
D_EXPERT = 512
DN_ALPHA = (2.0 * DEPTH) ** 0.25
LN_EPS = 1e-5
RMS_EPS = 1e-6

MAIN_W = 3 * GDN_W + GDN_W + 2 * D_MODEL
SMALL_W = 1280
SMALL_ROPE_BLOCK = (Q_LORA + KV_LORA) // 128
SMALL_GATE_COL = Q_LORA + KV_LORA + 2 * QK_ROPE
SMALL_GATE_BLOCK = SMALL_GATE_COL // 128

GDN_HEAD_BLOCK = 4
GDN_ROW_BLOCK = 256
MOE_BLOCK = 256
VMEM_LIMIT = 56 * 1024 * 1024

BF16 = jnp.bfloat16
F32 = jnp.float32


def _tile(n, pref):
    return pref if n % pref == 0 else n


def _params(*sem):
    return pltpu.CompilerParams(dimension_semantics=sem, vmem_limit_bytes=VMEM_LIMIT)


def _sigmoid(x):
    return 1.0 / (1.0 + jnp.exp(-x))


def _softplus(x):
    return jnp.maximum(x, 0.0) + jnp.log1p(jnp.exp(-jnp.abs(x)))


def _layer_norm(y, g, b):
    mu = jnp.mean(y, -1, keepdims=True)
    d = y - mu
    var = jnp.mean(d * d, -1, keepdims=True)
    return d * lax.rsqrt(var + LN_EPS) * g + b


def _dot(a, b):
    return jnp.dot(a, b, preferred_element_type=F32)


def _dot_nt(a, b):
    return lax.dot_general(a, b, (((1,), (1,)), ((), ())), preferred_element_type=F32)


def _dot_tn(a, b):
    return lax.dot_general(a, b, (((0,), (0,)), ((), ())), preferred_element_type=F32)


def _mm_kernel(a_ref, w_ref, o_ref):
    o_ref[...] = _dot(a_ref[...], w_ref[...]).astype(o_ref.dtype)


def _matmul(a, w, out_dtype, tm, tn, name):
    m, k = a.shape
    n = w.shape[1]
    tm, tn = _tile(m, tm), _tile(n, tn)
    return pl.pallas_call(
        _mm_kernel,
        grid=(m // tm, n // tn),
        in_specs=[pl.BlockSpec((tm, k), lambda i, j: (i, 0)),
                  pl.BlockSpec((k, tn), lambda i, j: (0, j))],
        out_specs=pl.BlockSpec((tm, tn), lambda i, j: (i, j)),
        out_shape=jax.ShapeDtypeStruct((m, n), out_dtype),
        compiler_params=_params("parallel", "parallel"),
        name=name,
    )(a, w)


def _mla_prep_kernel(cq_ref, ckv_ref, kr_ref, cs_ref, qn_ref, kvn_ref, wq_ref, wkv_ref, q_ref, k_ref, v_ref):
    ts = cq_ref.shape[1]
    cq = cq_ref[0]
    cqn = (cq * lax.rsqrt(jnp.mean(cq * cq, -1, keepdims=True) + RMS_EPS) * qn_ref[...]).astype(BF16)
    ckv = ckv_ref[0]
    ckvn = (ckv * lax.rsqrt(jnp.mean(ckv * ckv, -1, keepdims=True) + RMS_EPS) * kvn_ref[...]).astype(BF16)
    cs = cs_ref[0]
    lane = lax.broadcasted_iota(jnp.int32, (ts, 128), 1)

    def rotate(t):
        u = t * cs
        return jnp.where(lane < QK_ROPE, u + pltpu.roll(u, QK_ROPE, 1), 0.0).astype(BF16)

    kpe = rotate(kr_ref[0])
    ones = jnp.ones((ts, V_HEAD), BF16)
    for h in range(MLA_HEADS):
        rq = _dot(cqn, wq_ref[h])
        q_ref[0, h, :, :QK_NOPE] = rq[:, :QK_NOPE].astype(BF16)
        q_ref[0, h, :, QK_NOPE:] = rotate(rq[:, QK_NOPE:])
        rkv = _dot(ckvn, wkv_ref[h])
        k_ref[0, h, :, :QK_NOPE] = rkv[:, :QK_NOPE].astype(BF16)
        k_ref[0, h, :, QK_NOPE:] = kpe
        v_ref[0, h, :, :V_HEAD] = rkv[:, QK_NOPE:].astype(BF16)
        v_ref[0, h, :, V_HEAD:] = ones


def _mla_prep(small, cs, q_norm, kv_norm, wq, wkv):
    b, s, _ = small.shape
    ts = _tile(s, 512)
    hq = MLA_HEADS
    out_spec = pl.BlockSpec((1, hq, ts, 256), lambda bi, si: (bi, 0, si, 0))
    out_shape = jax.ShapeDtypeStruct((b, hq, s, 256), BF16)
    return pl.pallas_call(
        _mla_prep_kernel,
        grid=(b, s // ts),
        in_specs=[
            pl.BlockSpec((1, ts, Q_LORA), lambda bi, si: (bi, si, 0)),
            pl.BlockSpec((1, ts, KV_LORA), lambda bi, si: (bi, si, 1)),
            pl.BlockSpec((1, ts, 128), lambda bi, si: (bi, si, SMALL_ROPE_BLOCK)),
            pl.BlockSpec((1, ts, 128), lambda bi, si: (bi, si, 0)),
            pl.BlockSpec((1, Q_LORA), lambda bi, si: (0, 0)),
            pl.BlockSpec((1, KV_LORA), lambda bi, si: (0, 0)),
            pl.BlockSpec((hq, Q_LORA, 256), lambda bi, si: (0, 0, 0)),
            pl.BlockSpec((hq, KV_LORA, 256), lambda bi, si: (0, 0, 0)),
        ],
        out_specs=[out_spec, out_spec, out_spec],
        out_shape=[out_shape, out_shape, out_shape],
        compiler_params=_params("parallel", "parallel"),
        name="mla_prep",
    )(small, small, small, cs, q_norm, kv_norm, wq, wkv)


def _attn_kernel(q_ref, k_ref, v_ref, o_ref, sa_s, sb_s, m_s, acc_s, *, tq, rc, scale):
    i = pl.program_id(2)
    dv2 = acc_s.shape[-1]
    m_s[...] = jnp.full(m_s.shape, -jnp.inf, F32)
    acc_s[...] = jnp.zeros(acc_s.shape, F32)

    def scores(j):
        start = pl.multiple_of(j * tq, tq)
        return _dot_nt(q_ref[0, 0], k_ref[0, 0, pl.ds(start, tq), :])

    def softmax_pv(s_ref, c, j, nk, masked):
        rows = slice(c * rc, (c + 1) * rc)
        start = pl.multiple_of(j * tq, tq)
        s = s_ref[rows, :nk] * scale
        if masked:
            row = lax.broadcasted_iota(jnp.int32, (rc, nk), 0) + c * rc
            col = lax.broadcasted_iota(jnp.int32, (rc, nk), 1)
            s = jnp.where(col <= row, s, -jnp.inf)
        m_prev = m_s[rows, :]
        m_new = jnp.maximum(m_prev, jnp.max(s, -1, keepdims=True))
        alpha = jnp.exp(m_prev - m_new)
        p = jnp.exp(s - jnp.concatenate([m_new] * (nk // 128), -1))
        acc_s[rows, :] = (jnp.concatenate([alpha] * (dv2 // 128), -1) * acc_s[rows, :]
                          + _dot(p.astype(BF16), v_ref[0, 0, pl.ds(start, nk), :]))
        m_s[rows, :] = m_new

    nch = tq // rc

    def step(cur_ref, nxt_ref, j):
        nxt_ref[...] = scores(j + 1)
        for c in range(nch):
            softmax_pv(cur_ref, c, j, tq, False)

    def diagonal(cur_ref):
        for c in range(nch):
            softmax_pv(cur_ref, c, i, (c + 1) * rc, True)
        dv = dv2 // 2
        o_ref[0] = (acc_s[:, :dv] / acc_s[:, dv:]).astype(o_ref.dtype)

    sa_s[...] = scores(0)

    def pair(jj, carry):
        step(sa_s, sb_s, 2 * jj)
        step(sb_s, sa_s, 2 * jj + 1)
        return carry

    lax.fori_loop(0, i // 2, pair, 0)

    @pl.when(i % 2 == 0)
    def _():
        diagonal(sa_s)

    @pl.when(i % 2 == 1)
    def _():
        step(sa_s, sb_s, i - 1)
        diagonal(sb_s)


def _attention(q, k, v):
    b, h, s, dqk = q.shape
    dv2 = v.shape[-1]
    tq = _tile(s, 1024)
    rc = _tile(tq, 256)
    kern = functools.partial(_attn_kernel, tq=tq, rc=rc, scale=QK_DIM ** -0.5)
    return pl.pallas_call(
        kern,
        grid=(b, h, s // tq),
        in_specs=[
            pl.BlockSpec((1, 1, tq, dqk), lambda bi, hi, i: (bi, hi, i, 0)),
            pl.BlockSpec((1, 1, s, dqk), lambda bi, hi, i: (bi, hi, 0, 0)),
            pl.BlockSpec((1, 1, s, dv2), lambda bi, hi, i: (bi, hi, 0, 0)),
        ],
        out_specs=pl.BlockSpec((1, tq, dv2 // 2), lambda bi, hi, i: (bi, i, hi)),
        out_shape=jax.ShapeDtypeStruct((b, s, h * dv2 // 2), BF16),
        scratch_shapes=[pltpu.VMEM((tq, tq), F32), pltpu.VMEM((tq, tq), F32), pltpu.VMEM((tq, 128), F32),
                        pltpu.VMEM((tq, dv2), F32)],
        compiler_params=_params("parallel", "parallel", "arbitrary"),
        name="mla_attention",
    )(q, k, v)


def _gdn_kernel(qr_ref, kr_ref, vr_ref, wq_ref, wk_ref, wv_ref, z_ref, gate_ref, gatet_ref, arow_ref, dtrow_ref,
                acol_ref, dtcol_ref, onorm_ref, o_ref, state_s, xx_s, qkv_s, *, hb, rb):
    hg = pl.program_id(1)
    r = pl.program_id(2)

    @pl.when(r == 0)
    def _():
        state_s[...] = jnp.zeros(state_s.shape, F32)
        xx_s[:, 0:8, :] = jnp.zeros((3, 8, hb * 128), F32)

    @pl.when(r > 0)
    def _():
        xx_s[:, 0:8, :] = xx_s[:, rb:rb + 8, :]

    for ti, (x_ref, w_ref) in enumerate(((qr_ref, wq_ref), (kr_ref, wk_ref), (vr_ref, wv_ref))):
        xx_s[ti, 8:rb + 8, :] = x_ref[0]
        wc = w_ref[...]
        y = wc[CONV_K - 1:CONV_K, :] * xx_s[ti, 8:rb + 8, :]
        for j in range(CONV_K - 1):
            off = 8 - (CONV_K - 1) + j
            y = y + wc[j:j + 1, :] * xx_s[ti, off:off + rb, :]
        y = y * _sigmoid(y)
        if ti == 2:
            qkv_s[ti] = y
        else:
            for h in range(hb):
                yh = y[:, h * 128:(h + 1) * 128]
                yn = yh * lax.rsqrt(jnp.sum(yh * yh, -1, keepdims=True) + RMS_EPS)
                qkv_s[ti, :, h * 128:(h + 1) * 128] = yn * (GDN_DK ** -0.5) if ti == 0 else yn

    def head_rows(ti, c, h):
        return qkv_s[ti, c * CHUNK:(c + 1) * CHUNK, h * 128:(h + 1) * 128]

    blk = gate_ref[0]
    beta_tm = _sigmoid(blk)
    g_tm = -jnp.exp(arow_ref[...]) * _softplus(blk + dtrow_ref[...])
    g_t = -jnp.exp(acol_ref[...]) * _softplus(gatet_ref[0] + dtcol_ref[...])

    ri = lax.broadcasted_iota(jnp.int32, (CHUNK, CHUNK), 0)
    ci = lax.broadcasted_iota(jnp.int32, (CHUNK, CHUNK), 1)
    causal = ri >= ci
    strict = ri > ci
    tri = jnp.where(causal, 1.0, 0.0).astype(F32)
    tri_t = jnp.where(ri <= ci, 1.0, 0.0).astype(F32)
    eye = jnp.where(ri == ci, 1.0, 0.0).astype(F32)
    onorm = onorm_ref[...]

    nc = rb // CHUNK
    items = [(c, h) for c in range(nc) for h in range(hb)]
    rows = [slice(c * CHUNK, (c + 1) * CHUNK) for c in range(nc)]
    gc_tm = [jnp.dot(tri, g_tm[rows[c], :], precision=lax.Precision.HIGHEST, preferred_element_type=F32)
             for c in range(nc)]
    gc_t = [jnp.dot(g_t[:, rows[c]], tri_t, precision=lax.Precision.HIGHEST, preferred_element_type=F32)
            for c in range(nc)]
    sel_l = lax.broadcasted_iota(jnp.int32, (CHUNK, 128), 1)
    sel_s = lax.broadcasted_iota(jnp.int32, (32, CHUNK), 0)
    gcol, bcol, grow, glast, egc, kt, kb = {}, {}, {}, {}, {}, {}, {}
    for it in items:
        c, h = it
        lane = GDN_HEADS + hg * hb + h
        gcol[it] = jnp.sum(jnp.where(sel_l == lane, gc_tm[c], 0.0), -1, keepdims=True)
        bcol[it] = jnp.sum(jnp.where(sel_l == hg * hb + h, beta_tm[rows[c], :], 0.0), -1, keepdims=True)
        grow[it] = jnp.sum(jnp.where(sel_s == lane, gc_t[c], 0.0), 0, keepdims=True)
        glast[it] = gcol[it][CHUNK - 1:CHUNK, :]
        egc[it] = jnp.exp(gcol[it])
        k = head_rows(1, c, h)
        kt[it] = k.T
        kb[it] = k * bcol[it]
    kq = {it: _dot(jnp.concatenate([kb[it], head_rows(0, *it)], 0).astype(BF16),
                   kt[it].astype(BF16)) for it in items}
    amat, m, inv = {}, {}, {}
    for it in items:
        decay = jnp.exp(jnp.where(causal, gcol[it] - grow[it], -jnp.inf))
        amat[it] = (kq[it][CHUNK:] * decay).astype(BF16)
        m[it] = -jnp.where(strict, kq[it][:CHUNK] * decay, 0.0)
        inv[it] = eye + m[it]
    for it in items:
        mb = m[it].astype(BF16)
        m[it] = _dot(mb, mb)
    n_fac = int(round(math.log2(CHUNK))) - 1
    for t in range(n_fac):
        for it in items:
            mb = m[it].astype(BF16)
            if t < n_fac - 1:
                r2 = _dot(jnp.concatenate([inv[it], m[it]], 0).astype(BF16), mb)
                inv[it] = inv[it] + r2[:CHUNK]
                m[it] = r2[CHUNK:]
            else:
                inv[it] = inv[it] + _dot(inv[it].astype(BF16), mb)
    uw = {}
    for it in items:
        c, h = it
        vb = head_rows(2, c, h) * bcol[it]
        uw[it] = _dot(inv[it].astype(BF16), jnp.concatenate([vb, kb[it] * egc[it]], 1).astype(BF16))
    for c in range(nc):
        its = [(c, h) for h in range(hb)]
        st = {it: state_s[it[1]] for it in its}
        wq = {it: _dot(jnp.concatenate([uw[it][:, GDN_DV:], head_rows(0, *it) * egc[it]], 0).astype(BF16),
                       st[it].astype(BF16)) for it in its}
        vnb = {it: (uw[it][:, :GDN_DV] - wq[it][:CHUNK]).astype(BF16) for it in its}
        o = {it: wq[it][CHUNK:] + _dot(amat[it], vnb[it]) for it in its}
        for it in its:
            h = it[1]
            kdt = kt[it] * jnp.exp(glast[it] - grow[it])
            state_s[h] = st[it] * jnp.exp(glast[it]) + _dot(kdt.astype(BF16), vnb[it])
        for it in its:
            h = it[1]
            on = o[it] * lax.rsqrt(jnp.mean(o[it] * o[it], -1, keepdims=True) + RMS_EPS) * onorm
            zz = z_ref[0, rows[c], h * GDN_DV:(h + 1) * GDN_DV]
            o_ref[0, rows[c], h * GDN_DV:(h + 1) * GDN_DV] = (on * (zz * _sigmoid(zz))).astype(o_ref.dtype)


def _gdn(main, w_conv, small, small_t, a_log, dt_bias, o_norm):
    b, s, _ = main.shape
    hb = GDN_HEAD_BLOCK
    rb = _tile(s, GDN_ROW_BLOCK)
    ng = GDN_HEADS // hb
    zoff = 3 * GDN_W // (hb * 128)
    pad = lambda a, lo: jnp.zeros((128,), F32).at[lo:lo + GDN_HEADS].set(a.astype(F32))
    arow = pad(a_log, GDN_HEADS).reshape(1, 128)
    dtrow = pad(dt_bias, GDN_HEADS).reshape(1, 128)
    acol = arow[0, :32].reshape(32, 1)
    dtcol = dtrow[0, :32].reshape(32, 1)
    kern = functools.partial(_gdn_kernel, hb=hb, rb=rb)
    return pl.pallas_call(
        kern,
        grid=(b, ng, s // rb),
        in_specs=[
            pl.BlockSpec((1, rb, hb * 128), lambda bi, g, r: (bi, r, g)),
            pl.BlockSpec((1, rb, hb * 128), lambda bi, g, r: (bi, r, ng + g)),
            pl.BlockSpec((1, rb, hb * 128), lambda bi, g, r: (bi, r, 2 * ng + g)),
            pl.BlockSpec((CONV_K, hb * 128), lambda bi, g, r: (0, g)),
            pl.BlockSpec((CONV_K, hb * 128), lambda bi, g, r: (0, ng + g)),
            pl.BlockSpec((CONV_K, hb * 128), lambda bi, g, r: (0, 2 * ng + g)),
            pl.BlockSpec((1, rb, hb * 128), lambda bi, g, r: (bi, r, zoff + g)),
            pl.BlockSpec((1, rb, 128), lambda bi, g, r: (bi, r, SMALL_GATE_BLOCK)),
            pl.BlockSpec((1, 32, rb), lambda bi, g, r: (bi, 0, r)),
            pl.BlockSpec((1, 128), lambda bi, g, r: (0, 0)),
            pl.BlockSpec((1, 128), lambda bi, g, r: (0, 0)),
            pl.BlockSpec((32, 1), lambda bi, g, r: (0, 0)),
            pl.BlockSpec((32, 1), lambda bi, g, r: (0, 0)),
            pl.BlockSpec((1, 128), lambda bi, g, r: (0, 0)),
        ],
        out_specs=pl.BlockSpec((1, rb, hb * 128), lambda bi, g, r: (bi, r, g)),
        out_shape=jax.ShapeDtypeStruct((b, s, GDN_W), BF16),
        scratch_shapes=[pltpu.VMEM((hb, GDN_DK, GDN_DV), F32), pltpu.VMEM((3, rb + 8, hb * 128), F32),
                        pltpu.VMEM((3, rb, hb * 128), F32)],
        compiler_params=_params("parallel", "parallel", "arbitrary"),
        name="gdn_delta_rule",
    )(main, main, main, w_conv, w_conv, w_conv, main, small, small_t, arow, dtrow, acol, dtcol,
      o_norm.reshape(1, GDN_DV).astype(F32))


def _merge_kernel(om_ref, og_ref, gm_ref, gg_ref, wm_ref, wg_ref, o_ref):
    ym = _dot(om_ref[...], wm_ref[...])
    yg = _dot(og_ref[...], wg_ref[...])
    o_ref[...] = (_sigmoid(gm_ref[...]) * ym + _sigmoid(gg_ref[...]) * yg).astype(o_ref.dtype)


def _merge(o_mla, o_gdn, main, w_mla, w_gdn):
    t, d = o_mla.shape
    tm, tn = _tile(t, 512), _tile(d, 1024)
    goff = (4 * GDN_W) // tn
    nn = d // tn
    return pl.pallas_call(
        _merge_kernel,
        grid=(nn, t // tm),
        in_specs=[
            pl.BlockSpec((tm, d), lambda j, i: (i, 0)),
            pl.BlockSpec((tm, d), lambda j, i: (i, 0)),
            pl.BlockSpec((tm, tn), lambda j, i: (i, goff + j)),
            pl.BlockSpec((tm, tn), lambda j, i: (i, goff + nn + j)),
            pl.BlockSpec((d, tn), lambda j, i: (0, j)),
            pl.BlockSpec((d, tn), lambda j, i: (0, j)),
        ],
        out_specs=pl.BlockSpec((tm, tn), lambda j, i: (i, j)),
        out_shape=jax.ShapeDtypeStruct((t, d), BF16),
        compiler_params=_params("parallel", "parallel"),
        name="mixer_merge",
    )(o_mla, o_gdn, main, main, w_mla, w_gdn)


def _proj_ln_kernel(a_ref, w_ref, x_ref, g_ref, b_ref, o_ref, obf_ref):
    y = DN_ALPHA * x_ref[...] + _dot(a_ref[...], w_ref[...])
    out = _layer_norm(y, g_ref[...], b_ref[...])
    o_ref[...] = out
    obf_ref[...] = out.astype(BF16)


def _proj_ln(a, w, x, g, bias):
    t, k = a.shape
    d = w.shape[1]
    tm = _tile(t, 512)
    return pl.pallas_call(
        _proj_ln_kernel,
        grid=(t // tm,),
        in_specs=[
            pl.BlockSpec((tm, k), lambda i: (i, 0)),
            pl.BlockSpec((k, d), lambda i: (0, 0)),
            pl.BlockSpec((tm, d), lambda i: (i, 0)),
            pl.BlockSpec((1, d), lambda i: (0, 0)),
            pl.BlockSpec((1, d), lambda i: (0, 0)),
        ],
        out_specs=[pl.BlockSpec((tm, d), lambda i: (i, 0)), pl.BlockSpec((tm, d), lambda i: (i, 0))],
        out_shape=[jax.ShapeDtypeStruct((t, d), F32), jax.ShapeDtypeStruct((t, d), BF16)],
        compiler_params=_params("parallel"),
        name="proj_residual_ln",
    )(a, w, x, g.reshape(1, d), bias.reshape(1, d))


def _xattn_kernel(xbf_ref, x_ref, wq_ref, kv_ref, wo_ref, g_ref, b_ref, o_ref, obf_ref):
    q = _dot(xbf_ref[0], wq_ref[...])
    scale = X_HEAD_DIM ** -0.5
    outs = []
    for h in range(X_HEADS):
        qh = q[:, h * X_HEAD_DIM:(h + 1) * X_HEAD_DIM].astype(BF16)
        kh = kv_ref[0, :, h * X_HEAD_DIM:(h + 1) * X_HEAD_DIM]
        vh = kv_ref[0, :, X_W + h * X_HEAD_DIM:X_W + (h + 1) * X_HEAD_DIM]
        s = _dot_nt(qh, kh) * scale
        e = jnp.exp(s - jnp.max(s, -1, keepdims=True))
        p = e / jnp.sum(e, -1, keepdims=True)
        outs.append(_dot(p.astype(BF16), vh))
    o = jnp.concatenate(outs, -1).astype(BF16)
    y = DN_ALPHA * x_ref[0] + _dot(o, wo_ref[...])
    out = _layer_norm(y, g_ref[...], b_ref[...])
    o_ref[0] = out
    obf_ref[0] = out.astype(BF16)


def _xattn(x_bf, x, wq, kv, wo, g, bias):
    b, s, d = x.shape
    tm = _tile(s, 512)
    ml = kv.shape[1]
    return pl.pallas_call(
        _xattn_kernel,
        grid=(b, s // tm),
        in_specs=[
            pl.BlockSpec((1, tm, d), lambda bi, i: (bi, i, 0)),
            pl.BlockSpec((1, tm, d), lambda bi, i: (bi, i, 0)),
            pl.BlockSpec((d, X_W), lambda bi, i: (0, 0)),
            pl.BlockSpec((1, ml, 2 * X_W), lambda bi, i: (bi, 0, 0)),
            pl.BlockSpec((X_W, d), lambda bi, i: (0, 0)),
            pl.BlockSpec((1, d), lambda bi, i: (0, 0)),
            pl.BlockSpec((1, d), lambda bi, i: (0, 0)),
        ],
        out_specs=[pl.BlockSpec((1, tm, d), lambda bi, i: (bi, i, 0)),
                   pl.BlockSpec((1, tm, d), lambda bi, i: (bi, i, 0))],
        out_shape=[jax.ShapeDtypeStruct((b, s, d), F32), jax.ShapeDtypeStruct((b, s, d), BF16)],
        compiler_params=_params("parallel", "parallel"),
        name="memory_cross_attention",
    )(x_bf, x, wq, kv, wo, g.reshape(1, d), bias.reshape(1, d))


def _first_argmax(vals, iota, n):
    m = jnp.max(vals, 0, keepdims=True)
    idx = jnp.min(jnp.where(vals == m, iota, n), 0, keepdims=True)
    return m, idx


def _router_kernel(x_ref, wr_ref, bias_ref, idx_ref, gate_ref, rank_ref, cnt_ref, carry_s, *, tm):
    i = pl.program_id(0)

    @pl.when(i == 0)
    def _():
        carry_s[...] = jnp.zeros(carry_s.shape, F32)

    logits = lax.dot_general(wr_ref[...], x_ref[...], (((1,), (1,)), ((), ())),
                             precision=lax.Precision.HIGHEST, preferred_element_type=F32)
    scores = _sigmoid(logits)
    sel = scores + bias_ref[...]
    epg = EXPERTS_PER_GROUP
    io8 = lax.broadcasted_iota(jnp.int32, (epg, tm), 0)
    best_score = None
    for g in range(N_GROUPS):
        sg = sel[g * epg:(g + 1) * epg, :]
        m1, i1 = _first_argmax(sg, io8, epg)
        m2 = jnp.max(jnp.where(io8 == i1, -jnp.inf, sg), 0, keepdims=True)
        gs = m1 + m2
        if g == 0:
            best_score, best = gs, jnp.zeros((1, tm), jnp.int32)
            in_sel, in_sc = sg, scores[0:epg, :]
        else:
            better = gs > best_score
            best_score = jnp.where(better, gs, best_score)
            best = jnp.where(better, g, best)
            in_sel = jnp.where(better, sg, in_sel)
            in_sc = jnp.where(better, scores[g * epg:(g + 1) * epg, :], in_sc)
    _, l1 = _first_argmax(in_sel, io8, epg)
    _, l2 = _first_argmax(jnp.where(io8 == l1, -jnp.inf, in_sel), io8, epg)
    g1 = jnp.sum(jnp.where(io8 == l1, in_sc, 0.0), 0, keepdims=True)
    g2 = jnp.sum(jnp.where(io8 == l2, in_sc, 0.0), 0, keepdims=True)
    e1 = best * epg + l1
    e2 = best * epg + l2
    idx_ref[0:1, :] = e1
    idx_ref[1:2, :] = e2
    gsum = g1 + g2
    gate_ref[0:1, :] = g1 / gsum
    gate_ref[1:2, :] = g2 / gsum

    ioe = lax.broadcasted_iota(jnp.int32, (N_EXPERTS, tm), 0)
    oh1 = ioe == e1
    oh2 = ioe == e2
    onehot = jnp.where(oh1, 1.0, 0.0) + jnp.where(oh2, 1.0, 0.0)
    rr = lax.broadcasted_iota(jnp.int32, (tm, tm), 0)
    cc = lax.broadcasted_iota(jnp.int32, (tm, tm), 1)
    upper = jnp.where(rr < cc, 1.0, 0.0).astype(BF16)
    tot = carry_s[...] + _dot(onehot.astype(BF16), upper)
    rank_ref[0:1, :] = jnp.sum(jnp.where(oh1, tot, 0.0), 0, keepdims=True).astype(jnp.int32)
    rank_ref[1:2, :] = jnp.sum(jnp.where(oh2, tot, 0.0), 0, keepdims=True).astype(jnp.int32)
    new = carry_s[...] + jnp.sum(onehot, -1, keepdims=True)
    carry_s[...] = new
    cnt_ref[...] = jnp.broadcast_to(new, cnt_ref.shape)


def _router(x, w_router_t, bias):
    t, d = x.shape
    tm = _tile(t, 512)
    return pl.pallas_call(
        functools.partial(_router_kernel, tm=tm),
        grid=(t // tm,),
        in_specs=[pl.BlockSpec((tm, d), lambda i: (i, 0)),
                  pl.BlockSpec((N_EXPERTS, d), lambda i: (0, 0)),
                  pl.BlockSpec((N_EXPERTS, 1), lambda i: (0, 0))],
        out_specs=[pl.BlockSpec((TOP_K, tm), lambda i: (0, i)),
                   pl.BlockSpec((TOP_K, tm), lambda i: (0, i)),
                   pl.BlockSpec((TOP_K, tm), lambda i: (0, i)),
                   pl.BlockSpec((N_EXPERTS, 128), lambda i: (0, 0))],
        out_shape=[jax.ShapeDtypeStruct((TOP_K, t), jnp.int32),
                   jax.ShapeDtypeStruct((TOP_K, t), F32),
                   jax.ShapeDtypeStruct((TOP_K, t), jnp.int32),
                   jax.ShapeDtypeStruct((N_EXPERTS, 128), F32)],
        scratch_shapes=[pltpu.VMEM((N_EXPERTS, 1), F32)],
        compiler_params=_params("arbitrary"),
        name="moe_router",
    )(x, w_router_t, bias)


def _slot_kernel(idx_ref, rank_ref, pstart_ref, dest_ref, *, tm):
    ioe = lax.broadcasted_iota(jnp.int32, (N_EXPERTS, tm), 0)
    pstart = pstart_ref[...]
    for k in range(TOP_K):
        start = jnp.sum(jnp.where(ioe == idx_ref[k:k + 1, :], pstart, 0.0), 0, keepdims=True)
        dest_ref[k:k + 1, :] = start.astype(jnp.int32) + rank_ref[k:k + 1, :]


def _slots(idx, rank, pstart):
    t = idx.shape[1]
    tm = _tile(t, 2048)
    spec = pl.BlockSpec((TOP_K, tm), lambda i: (0, i))
    return pl.pallas_call(
        functools.partial(_slot_kernel, tm=tm),
        grid=(t // tm,),
        in_specs=[spec, spec, pl.BlockSpec((N_EXPERTS, 1), lambda i: (0, 0))],
        out_specs=spec,
        out_shape=jax.ShapeDtypeStruct((TOP_K, t), jnp.int32),
        compiler_params=_params("parallel"),
        name="moe_slots",
    )(idx, rank, pstart)


def _dispatch_kernel(dest_ref, x_ref, xs_in_ref, xs_ref, sem, *, tm, t_total):
    del xs_in_ref
    base = pl.program_id(0) * tm

    def issue(t, carry):
        for k in range(TOP_K):
            d = dest_ref[k * t_total + base + t]
            pltpu.make_async_copy(x_ref.at[pl.ds(t, 1), :], xs_ref.at[pl.ds(d, 1), :], sem).start()
        return carry

    lax.fori_loop(0, tm, issue, 0)

    def drain(t, carry):
        for k in range(TOP_K):
            pltpu.make_async_copy(x_ref.at[pl.ds(0, 1), :], xs_ref.at[pl.ds(0, 1), :], sem).wait()
        return carry

    lax.fori_loop(0, tm, drain, 0)


def _dispatch(dest_flat, x, n_slots):
    t, d = x.shape
    tm = _tile(t, 256)
    xs0 = jnp.zeros((n_slots, d), x.dtype)
    return pl.pallas_call(
        functools.partial(_dispatch_kernel, tm=tm, t_total=t),
        grid_spec=pltpu.PrefetchScalarGridSpec(
            num_scalar_prefetch=1,
            grid=(t // tm,),
            in_specs=[pl.BlockSpec((tm, d), lambda i, dest: (i, 0)),
                      pl.BlockSpec(memory_space=pl.ANY)],
            out_specs=pl.BlockSpec(memory_space=pl.ANY),
            scratch_shapes=[pltpu.SemaphoreType.DMA(())],
        ),
        out_shape=jax.ShapeDtypeStruct((n_slots, d), x.dtype),
        input_output_aliases={2: 0},
        compiler_params=_params("arbitrary"),
        name="moe_dispatch",
    )(dest_flat, x, xs0)


def _expert_kernel(be_ref, nu_ref, xs_ref, wg_ref, wu_ref, wd_ref, ys_ref, wg_s, wu_s, wd_s):
    i = pl.program_id(0)
    prev = be_ref[jnp.maximum(i - 1, 0)]

    @pl.when((i == 0) | (be_ref[i] != prev))
    def _():
        wg_s[...] = wg_ref[0].astype(BF16)
        wu_s[...] = wu_ref[0].astype(BF16)
        wd_s[...] = wd_ref[0].astype(BF16)

    @pl.when(i < nu_ref[0])
    def _():
        xb = xs_ref[...].astype(BF16)
        hg = _dot(xb, wg_s[...])
        hu = _dot(xb, wu_s[...])
        h = (hg * _sigmoid(hg)) * hu
        ys_ref[...] = _dot(h.astype(BF16), wd_s[...])

    @pl.when(i >= nu_ref[0])
    def _():
        ys_ref[...] = jnp.zeros(ys_ref.shape, F32)


def _experts(block_expert, n_used, xs, wg, wu, wd, layer):
    n_slots, d = xs.shape
    blk = MOE_BLOCK
    n_blocks = n_slots // blk
    de = wg.shape[-1]
    return pl.pallas_call(
        _expert_kernel,
        grid_spec=pltpu.PrefetchScalarGridSpec(
            num_scalar_prefetch=2,
            grid=(n_blocks,),
            in_specs=[
                pl.BlockSpec((blk, d), lambda i, be, nu: (jnp.minimum(i, nu[0] - 1), 0)),
                pl.BlockSpec((None, 1, d, de), lambda i, be, nu: (layer, be[i], 0, 0)),
                pl.BlockSpec((None, 1, d, de), lambda i, be, nu: (layer, be[i], 0, 0)),
                pl.BlockSpec((None, 1, de, d), lambda i, be, nu: (layer, be[i], 0, 0)),
            ],
            out_specs=pl.BlockSpec((blk, d), lambda i, be, nu: (i, 0)),
            scratch_shapes=[pltpu.VMEM((d, de), BF16), pltpu.VMEM((d, de), BF16), pltpu.VMEM((de, d), BF16)],
        ),
        out_shape=jax.ShapeDtypeStruct((n_slots, d), F32),
        compiler_params=_params("arbitrary"),
        name="moe_experts",
    )(block_expert, n_used, xs, wg, wu, wd)


def _combine_kernel(dest_ref, ys_ref, x_ref, gate_ref, g_ref, b_ref, o_ref, obf_ref, buf_s, sem, *, tm, t_total):
    base = pl.program_id(0) * tm

    def issue(t, carry):
        for k in range(TOP_K):
            d = dest_ref[k * t_total + base + t]
            pltpu.make_async_copy(ys_ref.at[pl.ds(d, 1), :], buf_s.at[k, pl.ds(t, 1), :], sem).start()
        return carry

    lax.fori_loop(0, tm, issue, 0)

    def drain(t, carry):
        for k in range(TOP_K):
            pltpu.make_async_copy(ys_ref.at[pl.ds(0, 1), :], buf_s.at[0, pl.ds(0, 1), :], sem).wait()
        return carry

    lax.fori_loop(0, tm, drain, 0)
    gate = gate_ref[...]
    y = gate[:, 0:1] * buf_s[0] + gate[:, 1:2] * buf_s[1]
    out = _layer_norm(DN_ALPHA * x_ref[...] + y, g_ref[...], b_ref[...])
    o_ref[...] = out
    obf_ref[...] = out.astype(BF16)


def _combine(dest_flat, ys, x, gate_tk, g, bias):
    t, d = x.shape
    tm = _tile(t, 256)
    return pl.pallas_call(
        functools.partial(_combine_kernel, tm=tm, t_total=t),
        grid_spec=pltpu.PrefetchScalarGridSpec(
            num_scalar_prefetch=1,
            grid=(t // tm,),
            in_specs=[
                pl.BlockSpec(memory_space=pl.ANY),
                pl.BlockSpec((tm, d), lambda i, dest: (i, 0)),
                pl.BlockSpec((tm, TOP_K), lambda i, dest: (i, 0)),
                pl.BlockSpec((1, d), lambda i, dest: (0, 0)),
                pl.BlockSpec((1, d), lambda i, dest: (0, 0)),
            ],
            out_specs=[pl.BlockSpec((tm, d), lambda i, dest: (i, 0)),
                       pl.BlockSpec((tm, d), lambda i, dest: (i, 0))],
            scratch_shapes=[pltpu.VMEM((TOP_K, tm, d), F32), pltpu.SemaphoreType.DMA(())],
        ),
        out_shape=[jax.ShapeDtypeStruct((t, d), F32), jax.ShapeDtypeStruct((t, d), BF16)],
        compiler_params=_params("arbitrary"),
        name="moe_combine_ln",
    )(dest_flat, ys, x, gate_tk, g.reshape(1, d), bias.reshape(1, d))


def _moe(x, w_router_t, router_bias, wg, wu, wd, layer, g, bias):
    t, d = x.shape
    idx, gate, rank, cnt = _router(x, w_router_t, router_bias)
    blk = MOE_BLOCK
    n_blocks = t * TOP_K // blk + N_EXPERTS
    counts = cnt[:, 0].astype(jnp.int32)
    padded = (counts + blk - 1) // blk * blk
    padded_end = jnp.cumsum(padded)
    padded_start = padded_end - padded
    dest = _slots(idx, rank, padded_start.astype(F32).reshape(N_EXPERTS, 1)).reshape(-1)
    n_used = (padded_end[-1] // blk).astype(jnp.int32)
    blocks = jnp.minimum(jnp.arange(n_blocks, dtype=jnp.int32), n_used - 1)
    block_expert = jnp.minimum(jnp.searchsorted(padded_end, blocks * blk, side='right'),
                               N_EXPERTS - 1).astype(jnp.int32)
    xs = _dispatch(dest, x, n_blocks * blk)
    ys = _experts(block_expert, n_used.reshape(1), xs, wg, wu, wd, layer)
    return _combine(dest, ys, x, gate.T, g, bias)


def _in_proj_weights(w_in):
    sizes = (Q_LORA, KV_LORA, QK_ROPE, GDN_W, GDN_W, GDN_W, GDN_W, GDN_HEADS, GDN_HEADS, D_MODEL, D_MODEL)
    pts = np.cumsum(sizes)[:-1].tolist()
    c_q, c_kv, k_rope, g_q, g_k, g_v, g_z, g_b, g_a, gate_mla, gate_gdn = jnp.split(w_in, pts, axis=-1)
    half = QK_ROPE // 2
    k_rope_sw = jnp.concatenate([k_rope[:, half:], k_rope[:, :half]], -1)
    main = jnp.concatenate([g_q, g_k, g_v, g_z, gate_mla, gate_gdn], -1)
    used = Q_LORA + KV_LORA + 2 * QK_ROPE + 2 * GDN_HEADS
    small = jnp.concatenate([c_q, c_kv, k_rope, k_rope_sw, g_b, g_a,
                             jnp.zeros((w_in.shape[0], SMALL_W - used), w_in.dtype)], -1)
    return main.astype(BF16), small.astype(BF16)


def _mla_weights(w_uq, w_ukv):
    half = QK_ROPE // 2
    wq = w_uq.reshape(Q_LORA, MLA_HEADS, QK_DIM).transpose(1, 0, 2)
    pe = wq[..., QK_NOPE:]
    pe_sw = jnp.concatenate([pe[..., half:], pe[..., :half]], -1)
    wq = jnp.concatenate([wq, pe_sw], -1).astype(BF16)
    wkv = w_ukv.reshape(KV_LORA, MLA_HEADS, QK_NOPE + V_HEAD).transpose(1, 0, 2).astype(BF16)
    return wq, wkv


def _rope_table(positions):
    half = QK_ROPE // 2
    inv_freq = jnp.power(ROPE_BASE, -jnp.arange(half, dtype=F32) / half)
    ang = positions.astype(F32)[..., None] * inv_freq
    cos, sin = jnp.cos(ang), jnp.sin(ang)
    return jnp.concatenate([cos, cos, -sin, sin], -1)


def kernel(x, mem, positions, w_in, mla_q_norm, mla_kv_norm, w_uq, w_ukv, gdn_conv, gdn_a_log, gdn_dt_bias, gdn_o_norm, w_mla_proj, w_gdn_proj, w_mix_out, ln_mix_g, ln_mix_b, w_xq, w_xkv, w_xo, ln_x_g, ln_x_b, w_router, router_bias, w_gate, w_up, w_down, ln_moe_g, ln_moe_b):
    b, s, d = x.shape
    t = b * s
    cs = _rope_table(positions)
    mem_bf = mem.reshape(-1, d).astype(BF16)
    w_router_t = w_router.T.astype(F32)
    rbias = router_bias.reshape(N_EXPERTS, 1).astype(F32)
    x_bf = x.astype(BF16)
    for l in range(w_in.shape[0]):
        w_main, w_small = _in_proj_weights(w_in[l])
        wq, wkv = _mla_weights(w_uq[l], w_ukv[l])
        xt_bf = x_bf.reshape(t, d)
        main = _matmul(xt_bf, w_main, F32, 1024, 1024, "in_proj_main").reshape(b, s, MAIN_W)
        small = _matmul(xt_bf, w_small, F32, 1024, SMALL_W, "in_proj_small").reshape(b, s, SMALL_W)

        q, k, v = _mla_prep(small, cs, mla_q_norm[l].reshape(1, -1), mla_kv_norm[l].reshape(1, -1), wq, wkv)
        o_mla = _attention(q, k, v)

        small_t = small[:, :, SMALL_GATE_COL:SMALL_GATE_COL + 2 * GDN_HEADS].transpose(0, 2, 1)
        o_gdn = _gdn(main, gdn_conv[l], small, small_t, gdn_a_log[l], gdn_dt_bias[l], gdn_o_norm[l])

        merged = _merge(o_mla.reshape(t, d), o_gdn.reshape(t, d), main.reshape(t, MAIN_W),
                        w_mla_proj[l].astype(BF16), w_gdn_proj[l].astype(BF16))
        x1, x1_bf = _proj_ln(merged, w_mix_out[l].astype(BF16), x.reshape(t, d), ln_mix_g[l], ln_mix_b[l])

        kv = _matmul(mem_bf, w_xkv[l].astype(BF16), BF16, 1024, 1024, "mem_kv").reshape(b, -1, 2 * X_W)
        x2, x2_bf = _xattn(x1_bf.reshape(b, s, d), x1.reshape(b, s, d), w_xq[l].astype(BF16), kv,
                           w_xo[l].astype(BF16), ln_x_g[l], ln_x_b[l])

        x3, x3_bf = _moe(x2.reshape(t, d), w_router_t, rbias, w_gate, w_up, w_down, l,
                         ln_moe_g[l], ln_moe_b[l])
        x, x_bf = x3.reshape(b, s, d), x3_bf.reshape(b, s, d)
    return x
```

```python
import functools
import math

import jax
import jax.numpy as jnp
import numpy as np
from jax import lax
from jax.experimental import pallas as pl
from jax.experimental.pallas import tpu as pltpu

D_MODEL = 2048
DEPTH = 2
MLA_HEADS = 16
Q_LORA = 512
KV_LORA = 512
QK_NOPE = 128
QK_ROPE = 64
V_HEAD = 128
QK_DIM = QK_NOPE + QK_ROPE
ROPE_BASE = 10000.0
GDN_HEADS = 16
GDN_DK = 128
GDN_DV = 128
CONV_K = 4
CHUNK = 64
GDN_W = GDN_HEADS * GDN_DK
X_HEADS = 4
X_HEAD_DIM = 128
X_W = X_HEADS * X_HEAD_DIM
N_EXPERTS = 64
N_GROUPS = 8
EXPERTS_PER_GROUP = N_EXPERTS // N_GROUPS
TOP_K = 2
D_EXPERT = 512
DN_ALPHA = (2.0 * DEPTH) ** 0.25
LN_EPS = 1e-5
RMS_EPS = 1e-6

MAIN_W = 3 * GDN_W + GDN_W + 2 * D_MODEL
SMALL_W = 1280
SMALL_ROPE_BLOCK = (Q_LORA + KV_LORA) // 128
SMALL_GATE_COL = Q_LORA + KV_LORA + 2 * QK_ROPE
SMALL_GATE_BLOCK = SMALL_GATE_COL // 128

GDN_HEAD_BLOCK = 4
GDN_ROW_BLOCK = 256
MOE_BLOCK = 256
DMA_ISSUE_UNROLL = 8
VMEM_LIMIT = 56 * 1024 * 1024

BF16 = jnp.bfloat16
F32 = jnp.float32


def _tile(n, pref):
    return pref if n % pref == 0 else n


def _params(*sem):
    return pltpu.CompilerParams(dimension_semantics=sem, vmem_limit_bytes=VMEM_LIMIT)


def _sigmoid(x):
    return 0.5 * jnp.tanh(0.5 * x) + 0.5


def _pack_halves(y):
    n = y.shape[1] // 2
    lo = pltpu.bitcast(y[:, :n].astype(BF16).astype(F32), jnp.uint32)
    hi = pltpu.bitcast(y[:, n:].astype(BF16).astype(F32), jnp.uint32)
    return (lo >> 16) | (hi & jnp.uint32(0xFFFF0000))


def _unpack_halves(p):
    lo = pltpu.bitcast(p << 16, F32)
    hi = pltpu.bitcast(p & jnp.uint32(0xFFFF0000), F32)
    return lo, hi


def _softplus(x):
    return jnp.maximum(x, 0.0) + jnp.log1p(jnp.exp(-jnp.abs(x)))


def _layer_norm(y, g, b):
    mu = jnp.mean(y, -1, keepdims=True)
    d = y - mu
    var = jnp.mean(d * d, -1, keepdims=True)
    return d * lax.rsqrt(var + LN_EPS) * g + b


def _dot(a, b):
    return jnp.dot(a, b, preferred_element_type=F32)


def _dot_nt(a, b):
    return lax.dot_general(a, b, (((1,), (1,)), ((), ())), preferred_element_type=F32)


def _dot_tn(a, b):
    return lax.dot_general(a, b, (((0,), (0,)), ((), ())), preferred_element_type=F32)


def _mm_kernel(a_ref, w_ref, o_ref):
    o_ref[...] = _dot(a_ref[...], w_ref[...]).astype(o_ref.dtype)


def _matmul(a, w, out_dtype, tm, tn, name):
    m, k = a.shape
    n = w.shape[1]
    tm, tn = _tile(m, tm), _tile(n, tn)
    return pl.pallas_call(
        _mm_kernel,
        grid=(m // tm, n // tn),
        in_specs=[pl.BlockSpec((tm, k), lambda i, j: (i, 0)),
                  pl.BlockSpec((k, tn), lambda i, j: (0, j))],
        out_specs=pl.BlockSpec((tm, tn), lambda i, j: (i, j)),
        out_shape=jax.ShapeDtypeStruct((m, n), out_dtype),
        compiler_params=_params("parallel", "parallel"),
        name=name,
    )(a, w)


def _mla_prep_kernel(cq_ref, ckv_ref, kr_ref, cs_ref, qn_ref, kvn_ref, wq_ref, wkv_ref, q_ref, k_ref, v_ref):
    ts = cq_ref.shape[1]
    cq = cq_ref[0]
    cqn = (cq * lax.rsqrt(jnp.mean(cq * cq, -1, keepdims=True) + RMS_EPS) * qn_ref[...]).astype(BF16)
    ckv = ckv_ref[0]
    ckvn = (ckv * lax.rsqrt(jnp.mean(ckv * ckv, -1, keepdims=True) + RMS_EPS) * kvn_ref[...]).astype(BF16)
    cs = cs_ref[0]
    lane = lax.broadcasted_iota(jnp.int32, (ts, 128), 1)

    def rotate(t):
        u = t * cs
        return jnp.where(lane < QK_ROPE, u + pltpu.roll(u, QK_ROPE, 1), 0.0).astype(BF16)

    kpe = rotate(kr_ref[0])
    ones = jnp.ones((ts, V_HEAD), BF16)
    for h in range(MLA_HEADS):
        rq = _dot(cqn, wq_ref[h])
        q_ref[0, h, :, :QK_NOPE] = rq[:, :QK_NOPE].astype(BF16)
        q_ref[0, h, :, QK_NOPE:] = rotate(rq[:, QK_NOPE:])
        rkv = _dot(ckvn, wkv_ref[h])
        k_ref[0, h, :, :QK_NOPE] = rkv[:, :QK_NOPE].astype(BF16)
        k_ref[0, h, :, QK_NOPE:] = kpe
        v_ref[0, h, :, :V_HEAD] = rkv[:, QK_NOPE:].astype(BF16)
        v_ref[0, h, :, V_HEAD:] = ones


def _mla_prep(small, cs, q_norm, kv_norm, wq, wkv):
    b, s, _ = small.shape
    ts = _tile(s, 512)
    hq = MLA_HEADS
    out_spec = pl.BlockSpec((1, hq, ts, 256), lambda bi, si: (bi, 0, si, 0))
    out_shape = jax.ShapeDtypeStruct((b, hq, s, 256), BF16)
    return pl.pallas_call(
        _mla_prep_kernel,
        grid=(b, s // ts),
        in_specs=[
            pl.BlockSpec((1, ts, Q_LORA), lambda bi, si: (bi, si, 0)),
            pl.BlockSpec((1, ts, KV_LORA), lambda bi, si: (bi, si, 1)),
            pl.BlockSpec((1, ts, 128), lambda bi, si: (bi, si, SMALL_ROPE_BLOCK)),
            pl.BlockSpec((1, ts, 128), lambda bi, si: (bi, si, 0)),
            pl.BlockSpec((1, Q_LORA), lambda bi, si: (0, 0)),
            pl.BlockSpec((1, KV_LORA), lambda bi, si: (0, 0)),
            pl.BlockSpec((hq, Q_LORA, 256), lambda bi, si: (0, 0, 0)),
            pl.BlockSpec((hq, KV_LORA, 256), lambda bi, si: (0, 0, 0)),
        ],
        out_specs=[out_spec, out_spec, out_spec],
        out_shape=[out_shape, out_shape, out_shape],
        compiler_params=_params("parallel", "parallel"),
        name="mla_prep",
    )(small, small, small, cs, q_norm, kv_norm, wq, wkv)


def _attn_kernel(q_ref, k_ref, v_ref, o_ref, sa_s, sb_s, m_s, acc_s, *, tq, rc, scale):
    i = pl.program_id(2)
    dv2 = acc_s.shape[-1]
    m_s[...] = jnp.full(m_s.shape, -jnp.inf, F32)
    acc_s[...] = jnp.zeros(acc_s.shape, F32)

    def scores(j):
        start = pl.multiple_of(j * tq, tq)
        return _dot_nt(q_ref[0, 0], k_ref[0, 0, pl.ds(start, tq), :])

    def softmax_pv(s_ref, c, j, nk, masked):
        rows = slice(c * rc, (c + 1) * rc)
        start = pl.multiple_of(j * tq, tq)
        s = s_ref[rows, :nk] * scale
        if masked:
            row = lax.broadcasted_iota(jnp.int32, (rc, nk), 0) + c * rc
            col = lax.broadcasted_iota(jnp.int32, (rc, nk), 1)
            s = jnp.where(col <= row, s, -jnp.inf)
        m_prev = m_s[rows, :]
        m_new = jnp.maximum(m_prev, jnp.max(s, -1, keepdims=True))
        alpha = jnp.exp(m_prev - m_new)
        p = jnp.exp(s - jnp.concatenate([m_new] * (nk // 128), -1))
        acc_s[rows, :] = (jnp.concatenate([alpha] * (dv2 // 128), -1) * acc_s[rows, :]
                          + _dot(p.astype(BF16), v_ref[0, 0, pl.ds(start, nk), :]))
        m_s[rows, :] = m_new

    nch = tq // rc

    def step(cur_ref, nxt_ref, j):
        nxt_ref[...] = scores(j + 1)
        for c in range(nch):
            softmax_pv(cur_ref, c, j, tq, False)

    def diagonal(cur_ref):
        for c in range(nch):
            softmax_pv(cur_ref, c, i, (c + 1) * rc, True)
        dv = dv2 // 2
        o_ref[0] = (acc_s[:, :dv] / acc_s[:, dv:]).astype(o_ref.dtype)

    sa_s[...] = scores(0)

    def pair(jj, carry):
        step(sa_s, sb_s, 2 * jj)
        step(sb_s, sa_s, 2 * jj + 1)
        return carry

    lax.fori_loop(0, i // 2, pair, 0)

    @pl.when(i % 2 == 0)
    def _():
        diagonal(sa_s)

    @pl.when(i % 2 == 1)
    def _():
        step(sa_s, sb_s, i - 1)
        diagonal(sb_s)


def _attention(q, k, v):
    b, h, s, dqk = q.shape
    dv2 = v.shape[-1]
    tq = _tile(s, 1024)
    rc = _tile(tq, 256)
    kern = functools.partial(_attn_kernel, tq=tq, rc=rc, scale=QK_DIM ** -0.5)
    return pl.pallas_call(
        kern,
        grid=(b, h, s // tq),
        in_specs=[
            pl.BlockSpec((1, 1, tq, dqk), lambda bi, hi, i: (bi, hi, i, 0)),
            pl.BlockSpec((1, 1, s, dqk), lambda bi, hi, i: (bi, hi, 0, 0)),
            pl.BlockSpec((1, 1, s, dv2), lambda bi, hi, i: (bi, hi, 0, 0)),
        ],
        out_specs=pl.BlockSpec((1, tq, dv2 // 2), lambda bi, hi, i: (bi, i, hi)),
        out_shape=jax.ShapeDtypeStruct((b, s, h * dv2 // 2), BF16),
        scratch_shapes=[pltpu.VMEM((tq, tq), F32), pltpu.VMEM((tq, tq), F32), pltpu.VMEM((tq, 128), F32),
                        pltpu.VMEM((tq, dv2), F32)],
        compiler_params=_params("parallel", "parallel", "arbitrary"),
        name="mla_attention",
    )(q, k, v)


def _gdn_kernel(qr_ref, kr_ref, vr_ref, wq_ref, wk_ref, wv_ref, z_ref, gate_ref, gatet_ref, arow_ref, dtrow_ref,
                acol_ref, dtcol_ref, onorm_ref, o_ref, state_s, xx_s, qkv_s, *, hb, rb):
    hg = pl.program_id(1)
    r = pl.program_id(2)

    @pl.when(r == 0)
    def _():
        state_s[...] = jnp.zeros(state_s.shape, F32)
        xx_s[:, 0:8, :] = jnp.zeros((3, 8, hb * 128), F32)

    @pl.when(r > 0)
    def _():
        xx_s[:, 0:8, :] = xx_s[:, rb:rb + 8, :]

    for ti, (x_ref, w_ref) in enumerate(((qr_ref, wq_ref), (kr_ref, wk_ref), (vr_ref, wv_ref))):
        xx_s[ti, 8:rb + 8, :] = x_ref[0]
        wc = w_ref[...]
        y = wc[CONV_K - 1:CONV_K, :] * xx_s[ti, 8:rb + 8, :]
        for j in range(CONV_K - 1):
            off = 8 - (CONV_K - 1) + j
            y = y + wc[j:j + 1, :] * xx_s[ti, off:off + rb, :]
        y = y * _sigmoid(y)
        if ti == 2:
            qkv_s[ti] = y
        else:
            for h in range(hb):
                yh = y[:, h * 128:(h + 1) * 128]
                yn = yh * lax.rsqrt(jnp.sum(yh * yh, -1, keepdims=True) + RMS_EPS)
                qkv_s[ti, :, h * 128:(h + 1) * 128] = yn * (GDN_DK ** -0.5) if ti == 0 else yn

    def head_rows(ti, c, h):
        return qkv_s[ti, c * CHUNK:(c + 1) * CHUNK, h * 128:(h + 1) * 128]

    blk = gate_ref[0]
    beta_tm = _sigmoid(blk)
    g_tm = -jnp.exp(arow_ref[...]) * _softplus(blk + dtrow_ref[...])
    g_t = -jnp.exp(acol_ref[...]) * _softplus(gatet_ref[0] + dtcol_ref[...])

    ri = lax.broadcasted_iota(jnp.int32, (CHUNK, CHUNK), 0)
    ci = lax.broadcasted_iota(jnp.int32, (CHUNK, CHUNK), 1)
    causal = ri >= ci
    strict = ri > ci
    tri = jnp.where(causal, 1.0, 0.0).astype(F32)
    tri_t = jnp.where(ri <= ci, 1.0, 0.0).astype(F32)
    eye = jnp.where(ri == ci, 1.0, 0.0).astype(F32)
    onorm = onorm_ref[...]

    nc = rb // CHUNK
    items = [(c, h) for c in range(nc) for h in range(hb)]
    rows = [slice(c * CHUNK, (c + 1) * CHUNK) for c in range(nc)]
    gc_tm = [jnp.dot(tri, g_tm[rows[c], :], precision=lax.Precision.HIGHEST, preferred_element_type=F32)
             for c in range(nc)]
    gc_t = [jnp.dot(g_t[:, rows[c]], tri_t, precision=lax.Precision.HIGHEST, preferred_element_type=F32)
            for c in range(nc)]
    sel_l = lax.broadcasted_iota(jnp.int32, (CHUNK, 128), 1)
    sel_s = lax.broadcasted_iota(jnp.int32, (32, CHUNK), 0)
    gcol, bcol, grow, glast, egc, kt, kb = {}, {}, {}, {}, {}, {}, {}
    for it in items:
        c, h = it
        lane = GDN_HEADS + hg * hb + h
        gcol[it] = jnp.sum(jnp.where(sel_l == lane, gc_tm[c], 0.0), -1, keepdims=True)
        bcol[it] = jnp.sum(jnp.where(sel_l == hg * hb + h, beta_tm[rows[c], :], 0.0), -1, keepdims=True)
        grow[it] = jnp.sum(jnp.where(sel_s == lane, gc_t[c], 0.0), 0, keepdims=True)
        glast[it] = gcol[it][CHUNK - 1:CHUNK, :]
        egc[it] = jnp.exp(gcol[it])
        k = head_rows(1, c, h)
        kt[it] = k.T
        kb[it] = k * bcol[it]
    kq = {it: _dot(jnp.concatenate([kb[it], head_rows(0, *it)], 0).astype(BF16),
                   kt[it].astype(BF16)) for it in items}
    amat, m, inv = {}, {}, {}
    for it in items:
        decay = jnp.exp(jnp.where(causal, gcol[it] - grow[it], -jnp.inf))
        amat[it] = (kq[it][CHUNK:] * decay).astype(BF16)
        m[it] = -jnp.where(strict, kq[it][:CHUNK] * decay, 0.0)
        inv[it] = eye + m[it]
    for it in items:
        mb = m[it].astype(BF16)
        m[it] = _dot(mb, mb)
    n_fac = int(round(math.log2(CHUNK))) - 1
    for t in range(n_fac):
        for it in items:
            mb = m[it].astype(BF16)
            if t < n_fac - 1:
                r2 = _dot(jnp.concatenate([inv[it], m[it]], 0).astype(BF16), mb)
                inv[it] = inv[it] + r2[:CHUNK]
                m[it] = r2[CHUNK:]
            else:
                inv[it] = inv[it] + _dot(inv[it].astype(BF16), mb)
    uw = {}
    for it in items:
        c, h = it
        vb = head_rows(2, c, h) * bcol[it]
        uw[it] = _dot(inv[it].astype(BF16), jnp.concatenate([vb, kb[it] * egc[it]], 1).astype(BF16))
    for c in range(nc):
        its = [(c, h) for h in range(hb)]
        st = {it: state_s[it[1]] for it in its}
        wq = {it: _dot(jnp.concatenate([uw[it][:, GDN_DV:], head_rows(0, *it) * egc[it]], 0).astype(BF16),
                       st[it].astype(BF16)) for it in its}
        vnb = {it: (uw[it][:, :GDN_DV] - wq[it][:CHUNK]).astype(BF16) for it in its}
        o = {it: wq[it][CHUNK:] + _dot(amat[it], vnb[it]) for it in its}
        for it in its:
            h = it[1]
            kdt = kt[it] * jnp.exp(glast[it] - grow[it])
            state_s[h] = st[it] * jnp.exp(glast[it]) + _dot(kdt.astype(BF16), vnb[it])
        for it in its:
            h = it[1]
            on = o[it] * lax.rsqrt(jnp.mean(o[it] * o[it], -1, keepdims=True) + RMS_EPS) * onorm
            zz = z_ref[0, rows[c], h * GDN_DV:(h + 1) * GDN_DV]
            o_ref[0, rows[c], h * GDN_DV:(h + 1) * GDN_DV] = (on * (zz * _sigmoid(zz))).astype(o_ref.dtype)


def _gdn(main, w_conv, small, small_t, a_log, dt_bias, o_norm):
    b, s, _ = main.shape
    hb = GDN_HEAD_BLOCK
    rb = _tile(s, GDN_ROW_BLOCK)
    ng = GDN_HEADS // hb
    zoff = 3 * GDN_W // (hb * 128)
    pad = lambda a, lo: jnp.zeros((128,), F32).at[lo:lo + GDN_HEADS].set(a.astype(F32))
    arow = pad(a_log, GDN_HEADS).reshape(1, 128)
    dtrow = pad(dt_bias, GDN_HEADS).reshape(1, 128)
    acol = arow[0, :32].reshape(32, 1)
    dtcol = dtrow[0, :32].reshape(32, 1)
    kern = functools.partial(_gdn_kernel, hb=hb, rb=rb)
    return pl.pallas_call(
        kern,
        grid=(b, ng, s // rb),
        in_specs=[
            pl.BlockSpec((1, rb, hb * 128), lambda bi, g, r: (bi, r, g)),
            pl.BlockSpec((1, rb, hb * 128), lambda bi, g, r: (bi, r, ng + g)),
            pl.BlockSpec((1, rb, hb * 128), lambda bi, g, r: (bi, r, 2 * ng + g)),
            pl.BlockSpec((CONV_K, hb * 128), lambda bi, g, r: (0, g)),
            pl.BlockSpec((CONV_K, hb * 128), lambda bi, g, r: (0, ng + g)),
            pl.BlockSpec((CONV_K, hb * 128), lambda bi, g, r: (0, 2 * ng + g)),
            pl.BlockSpec((1, rb, hb * 128), lambda bi, g, r: (bi, r, zoff + g)),
            pl.BlockSpec((1, rb, 128), lambda bi, g, r: (bi, r, SMALL_GATE_BLOCK)),
            pl.BlockSpec((1, 32, rb), lambda bi, g, r: (bi, 0, r)),
            pl.BlockSpec((1, 128), lambda bi, g, r: (0, 0)),
            pl.BlockSpec((1, 128), lambda bi, g, r: (0, 0)),
            pl.BlockSpec((32, 1), lambda bi, g, r: (0, 0)),
            pl.BlockSpec((32, 1), lambda bi, g, r: (0, 0)),
            pl.BlockSpec((1, 128), lambda bi, g, r: (0, 0)),
        ],
        out_specs=pl.BlockSpec((1, rb, hb * 128), lambda bi, g, r: (bi, r, g)),
        out_shape=jax.ShapeDtypeStruct((b, s, GDN_W), BF16),
        scratch_shapes=[pltpu.VMEM((hb, GDN_DK, GDN_DV), F32), pltpu.VMEM((3, rb + 8, hb * 128), F32),
                        pltpu.VMEM((3, rb, hb * 128), F32)],
        compiler_params=_params("parallel", "parallel", "arbitrary"),
        name="gdn_delta_rule",
    )(main, main, main, w_conv, w_conv, w_conv, main, small, small_t, arow, dtrow, acol, dtcol,
      o_norm.reshape(1, GDN_DV).astype(F32))


def _merge_kernel(om_ref, og_ref, gm_ref, gg_ref, wm_ref, wg_ref, o_ref):
    ym = _dot(om_ref[...], wm_ref[...])
    yg = _dot(og_ref[...], wg_ref[...])
    o_ref[...] = (_sigmoid(gm_ref[...]) * ym + _sigmoid(gg_ref[...]) * yg).astype(o_ref.dtype)


def _merge(o_mla, o_gdn, main, w_mla, w_gdn):
    t, d = o_mla.shape
    tm, tn = _tile(t, 512), _tile(d, 1024)
    goff = (4 * GDN_W) // tn
    nn = d // tn
    return pl.pallas_call(
        _merge_kernel,
        grid=(nn, t // tm),
        in_specs=[
            pl.BlockSpec((tm, d), lambda j, i: (i, 0)),
            pl.BlockSpec((tm, d), lambda j, i: (i, 0)),
            pl.BlockSpec((tm, tn), lambda j, i: (i, goff + j)),
            pl.BlockSpec((tm, tn), lambda j, i: (i, goff + nn + j)),
            pl.BlockSpec((d, tn), lambda j, i: (0, j)),
            pl.BlockSpec((d, tn), lambda j, i: (0, j)),
        ],
        out_specs=pl.BlockSpec((tm, tn), lambda j, i: (i, j)),
        out_shape=jax.ShapeDtypeStruct((t, d), BF16),
        compiler_params=_params("parallel", "parallel"),
        name="mixer_merge",
    )(o_mla, o_gdn, main, main, w_mla, w_gdn)


def _proj_ln_kernel(a_ref, w_ref, x_ref, g_ref, b_ref, o_ref, obf_ref):
    y = DN_ALPHA * x_ref[...] + _dot(a_ref[...], w_ref[...])
    out = _layer_norm(y, g_ref[...], b_ref[...])
    o_ref[...] = out
    obf_ref[...] = out.astype(BF16)


def _proj_ln(a, w, x, g, bias):
    t, k = a.shape
    d = w.shape[1]
    tm = _tile(t, 512)
    return pl.pallas_call(
        _proj_ln_kernel,
        grid=(t // tm,),
        in_specs=[
            pl.BlockSpec((tm, k), lambda i: (i, 0)),
            pl.BlockSpec((k, d), lambda i: (0, 0)),
            pl.BlockSpec((tm, d), lambda i: (i, 0)),
            pl.BlockSpec((1, d), lambda i: (0, 0)),
            pl.BlockSpec((1, d), lambda i: (0, 0)),
        ],
        out_specs=[pl.BlockSpec((tm, d), lambda i: (i, 0)), pl.BlockSpec((tm, d), lambda i: (i, 0))],
        out_shape=[jax.ShapeDtypeStruct((t, d), F32), jax.ShapeDtypeStruct((t, d), BF16)],
        compiler_params=_params("parallel"),
        name="proj_residual_ln",
    )(a, w, x, g.reshape(1, d), bias.reshape(1, d))


def _xattn_kernel(xbf_ref, x_ref, wq_ref, kv_ref, wo_ref, g_ref, b_ref, o_ref, opk_ref):
    q = _dot(xbf_ref[0], wq_ref[...])
    scale = X_HEAD_DIM ** -0.5
    outs = []
    for h in range(X_HEADS):
        qh = q[:, h * X_HEAD_DIM:(h + 1) * X_HEAD_DIM].astype(BF16)
        kh = kv_ref[0, :, h * X_HEAD_DIM:(h + 1) * X_HEAD_DIM]
        vh = kv_ref[0, :, X_W + h * X_HEAD_DIM:X_W + (h + 1) * X_HEAD_DIM]
        s = _dot_nt(qh, kh) * scale
        e = jnp.exp(s - jnp.max(s, -1, keepdims=True))
        p = e / jnp.sum(e, -1, keepdims=True)
        outs.append(_dot(p.astype(BF16), vh))
    o = jnp.concatenate(outs, -1).astype(BF16)
    y = DN_ALPHA * x_ref[0] + _dot(o, wo_ref[...])
    out = _layer_norm(y, g_ref[...], b_ref[...])
    o_ref[0] = out
    opk_ref[0] = _pack_halves(out)


def _xattn(x_bf, x, wq, kv, wo, g, bias):
    b, s, d = x.shape
    tm = _tile(s, 512)
    ml = kv.shape[1]
    return pl.pallas_call(
        _xattn_kernel,
        grid=(b, s // tm),
        in_specs=[
            pl.BlockSpec((1, tm, d), lambda bi, i: (bi, i, 0)),
            pl.BlockSpec((1, tm, d), lambda bi, i: (bi, i, 0)),
            pl.BlockSpec((d, X_W), lambda bi, i: (0, 0)),
            pl.BlockSpec((1, ml, 2 * X_W), lambda bi, i: (bi, 0, 0)),
            pl.BlockSpec((X_W, d), lambda bi, i: (0, 0)),
            pl.BlockSpec((1, d), lambda bi, i: (0, 0)),
            pl.BlockSpec((1, d), lambda bi, i: (0, 0)),
        ],
        out_specs=[pl.BlockSpec((1, tm, d), lambda bi, i: (bi, i, 0)),
                   pl.BlockSpec((1, tm, d // 2), lambda bi, i: (bi, i, 0))],
        out_shape=[jax.ShapeDtypeStruct((b, s, d), F32), jax.ShapeDtypeStruct((b, s, d // 2), jnp.uint32)],
        compiler_params=_params("parallel", "parallel"),
        name="memory_cross_attention",
    )(x_bf, x, wq, kv, wo, g.reshape(1, d), bias.reshape(1, d))


def _first_argmax(vals, iota, n):
    m = jnp.max(vals, 0, keepdims=True)
    idx = jnp.min(jnp.where(vals == m, iota, n), 0, keepdims=True)
    return m, idx


def _router_kernel(x_ref, wr_ref, bias_ref, idx_ref, gate_ref, rank_ref, cnt_ref, carry_s, *, tm):
    i = pl.program_id(0)

    @pl.when(i == 0)
    def _():
        carry_s[...] = jnp.zeros(carry_s.shape, F32)

    logits = lax.dot_general(wr_ref[...], x_ref[...], (((1,), (1,)), ((), ())),
                             precision=lax.Precision.HIGHEST, preferred_element_type=F32)
    scores = 1.0 / (1.0 + jnp.exp(-logits))
    sel = scores + bias_ref[...]
    epg = EXPERTS_PER_GROUP
    io8 = lax.broadcasted_iota(jnp.int32, (epg, tm), 0)
    best_score = None
    for g in range(N_GROUPS):
        sg = sel[g * epg:(g + 1) * epg, :]
        m1, i1 = _first_argmax(sg, io8, epg)
        m2 = jnp.max(jnp.where(io8 == i1, -jnp.inf, sg), 0, keepdims=True)
        gs = m1 + m2
        if g == 0:
            best_score, best = gs, jnp.zeros((1, tm), jnp.int32)
            in_sel, in_sc = sg, scores[0:epg, :]
        else:
            better = gs > best_score
            best_score = jnp.where(better, gs, best_score)
            best = jnp.where(better, g, best)
            in_sel = jnp.where(better, sg, in_sel)
            in_sc = jnp.where(better, scores[g * epg:(g + 1) * epg, :], in_sc)
    _, l1 = _first_argmax(in_sel, io8, epg)
    _, l2 = _first_argmax(jnp.where(io8 == l1, -jnp.inf, in_sel), io8, epg)
    g1 = jnp.sum(jnp.where(io8 == l1, in_sc, 0.0), 0, keepdims=True)
    g2 = jnp.sum(jnp.where(io8 == l2, in_sc, 0.0), 0, keepdims=True)
    e1 = best * epg + l1
    e2 = best * epg + l2
    idx_ref[0:1, :] = e1
    idx_ref[1:2, :] = e2
    gsum = g1 + g2
    gate_ref[0:1, :] = g1 / gsum
    gate_ref[1:2, :] = g2 / gsum

    ioe = lax.broadcasted_iota(jnp.int32, (N_EXPERTS, tm), 0)
    oh1 = ioe == e1
    oh2 = ioe == e2
    onehot = jnp.where(oh1, 1.0, 0.0) + jnp.where(oh2, 1.0, 0.0)
    rr = lax.broadcasted_iota(jnp.int32, (tm, tm), 0)
    cc = lax.broadcasted_iota(jnp.int32, (tm, tm), 1)
    upper = jnp.where(rr < cc, 1.0, 0.0).astype(BF16)
    tot = carry_s[...] + _dot(onehot.astype(BF16), upper)
    rank_ref[0:1, :] = jnp.sum(jnp.where(oh1, tot, 0.0), 0, keepdims=True).astype(jnp.int32)
    rank_ref[1:2, :] = jnp.sum(jnp.where(oh2, tot, 0.0), 0, keepdims=True).astype(jnp.int32)
    new = carry_s[...] + jnp.sum(onehot, -1, keepdims=True)
    carry_s[...] = new
    cnt_ref[...] = jnp.broadcast_to(new, cnt_ref.shape)


def _router(x, w_router_t, bias):
    t, d = x.shape
    tm = _tile(t, 512)
    return pl.pallas_call(
        functools.partial(_router_kernel, tm=tm),
        grid=(t // tm,),
        in_specs=[pl.BlockSpec((tm, d), lambda i: (i, 0)),
                  pl.BlockSpec((N_EXPERTS, d), lambda i: (0, 0)),
                  pl.BlockSpec((N_EXPERTS, 1), lambda i: (0, 0))],
        out_specs=[pl.BlockSpec((TOP_K, tm), lambda i: (0, i)),
                   pl.BlockSpec((TOP_K, tm), lambda i: (0, i)),
                   pl.BlockSpec((TOP_K, tm), lambda i: (0, i)),
                   pl.BlockSpec((N_EXPERTS, 128), lambda i: (0, 0))],
        out_shape=[jax.ShapeDtypeStruct((TOP_K, t), jnp.int32),
                   jax.ShapeDtypeStruct((TOP_K, t), F32),
                   jax.ShapeDtypeStruct((TOP_K, t), jnp.int32),
                   jax.ShapeDtypeStruct((N_EXPERTS, 128), F32)],
        scratch_shapes=[pltpu.VMEM((N_EXPERTS, 1), F32)],
        compiler_params=_params("arbitrary"),
        name="moe_router",
    )(x, w_router_t, bias)


def _slot_kernel(idx_ref, rank_ref, pstart_ref, dest_ref, *, tm):
    ioe = lax.broadcasted_iota(jnp.int32, (N_EXPERTS, tm), 0)
    pstart = pstart_ref[...]
    for k in range(TOP_K):
        start = jnp.sum(jnp.where(ioe == idx_ref[k:k + 1, :], pstart, 0.0), 0, keepdims=True)
        dest_ref[k:k + 1, :] = start.astype(jnp.int32) + rank_ref[k:k + 1, :]


def _slots(idx, rank, pstart):
    t = idx.shape[1]
    tm = _tile(t, 2048)
    spec = pl.BlockSpec((TOP_K, tm), lambda i: (0, i))
    return pl.pallas_call(
        functools.partial(_slot_kernel, tm=tm),
        grid=(t // tm,),
        in_specs=[spec, spec, pl.BlockSpec((N_EXPERTS, 1), lambda i: (0, 0))],
        out_specs=spec,
        out_shape=jax.ShapeDtypeStruct((TOP_K, t), jnp.int32),
        compiler_params=_params("parallel"),
        name="moe_slots",
    )(idx, rank, pstart)


def _dispatch_kernel(dest_ref, x_ref, xs_in_ref, xs_ref, sem, *, tm, t_total):
    del xs_in_ref
    base = pl.program_id(0) * tm

    def issue(t, carry):
        for k in range(TOP_K):
            d = dest_ref[k * t_total + base + t]
            pltpu.make_async_copy(x_ref.at[pl.ds(t, 1), :], xs_ref.at[pl.ds(d, 1), :], sem).start()
        return carry

    lax.fori_loop(0, tm, issue, 0, unroll=DMA_ISSUE_UNROLL)

    def drain(t, carry):
        for k in range(TOP_K):
            pltpu.make_async_copy(x_ref.at[pl.ds(0, 1), :], xs_ref.at[pl.ds(0, 1), :], sem).wait()
        return carry

    lax.fori_loop(0, tm, drain, 0, unroll=DMA_ISSUE_UNROLL)


def _dispatch(dest_flat, x, n_slots):
    t, d = x.shape
    tm = _tile(t, 256)
    xs0 = jnp.zeros((n_slots, d), x.dtype)
    return pl.pallas_call(
        functools.partial(_dispatch_kernel, tm=tm, t_total=t),
        grid_spec=pltpu.PrefetchScalarGridSpec(
            num_scalar_prefetch=1,
            grid=(t // tm,),
            in_specs=[pl.BlockSpec((tm, d), lambda i, dest: (i, 0)),
                      pl.BlockSpec(memory_space=pl.ANY)],
            out_specs=pl.BlockSpec(memory_space=pl.ANY),
            scratch_shapes=[pltpu.SemaphoreType.DMA(())],
        ),
        out_shape=jax.ShapeDtypeStruct((n_slots, d), x.dtype),
        input_output_aliases={2: 0},
        compiler_params=_params("arbitrary"),
        name="moe_dispatch",
    )(dest_flat, x, xs0)


def _expert_kernel(be_ref, nu_ref, xs_ref, wg_ref, wu_ref, wd_ref, ys_ref, wg_s, wu_s, wd_s):
    i = pl.program_id(0)
    prev = be_ref[jnp.maximum(i - 1, 0)]

    @pl.when((i == 0) | (be_ref[i] != prev))
    def _():
        wg_s[...] = wg_ref[0].astype(BF16)
        wu_s[...] = wu_ref[0].astype(BF16)
        wd_s[...] = wd_ref[0].astype(BF16)

    @pl.when(i < nu_ref[0])
    def _():
        lo, hi = _unpack_halves(xs_ref[...])
        xb = jnp.concatenate([lo.astype(BF16), hi.astype(BF16)], -1)
        hg = _dot(xb, wg_s[...])
        hu = _dot(xb, wu_s[...])
        h = (hg * _sigmoid(hg)) * hu
        ys_ref[...] = _pack_halves(_dot(h.astype(BF16), wd_s[...]))

    @pl.when(i >= nu_ref[0])
    def _():
        ys_ref[...] = jnp.zeros(ys_ref.shape, ys_ref.dtype)


def _experts(block_expert, n_used, xs, wg, wu, wd, layer):
    n_slots, dp = xs.shape
    d = 2 * dp
    blk = MOE_BLOCK
    n_blocks = n_slots // blk
    de = wg.shape[-1]
    return pl.pallas_call(
        _expert_kernel,
        grid_spec=pltpu.PrefetchScalarGridSpec(
            num_scalar_prefetch=2,
            grid=(n_blocks,),
            in_specs=[
                pl.BlockSpec((blk, dp), lambda i, be, nu: (jnp.minimum(i, nu[0] - 1), 0)),
                pl.BlockSpec((None, 1, d, de), lambda i, be, nu: (layer, be[i], 0, 0)),
                pl.BlockSpec((None, 1, d, de), lambda i, be, nu: (layer, be[i], 0, 0)),
                pl.BlockSpec((None, 1, de, d), lambda i, be, nu: (layer, be[i], 0, 0)),
            ],
            out_specs=pl.BlockSpec((blk, dp), lambda i, be, nu: (i, 0)),
            scratch_shapes=[pltpu.VMEM((d, de), BF16), pltpu.VMEM((d, de), BF16), pltpu.VMEM((de, d), BF16)],
        ),
        out_shape=jax.ShapeDtypeStruct((n_slots, dp), jnp.uint32),
        compiler_params=_params("arbitrary"),
        name="moe_experts",
    )(block_expert, n_used, xs, wg, wu, wd)


def _combine_kernel(dest_ref, ys_ref, x_ref, gate_ref, g_ref, b_ref, o_ref, obf_ref, buf_s, sem, *, tm, t_total):
    base = pl.program_id(0) * tm

    def issue(t, carry):
        for k in range(TOP_K):
            d = dest_ref[k * t_total + base + t]
            pltpu.make_async_copy(ys_ref.at[pl.ds(d, 1), :], buf_s.at[k, pl.ds(t, 1), :], sem).start()
        return carry

    lax.fori_loop(0, tm, issue, 0, unroll=DMA_ISSUE_UNROLL)

    def drain(t, carry):
        for k in range(TOP_K):
            pltpu.make_async_copy(ys_ref.at[pl.ds(0, 1), :], buf_s.at[0, pl.ds(0, 1), :], sem).wait()
        return carry

    lax.fori_loop(0, tm, drain, 0, unroll=DMA_ISSUE_UNROLL)
    gate = gate_ref[...]
    lo0, hi0 = _unpack_halves(buf_s[0])
    lo1, hi1 = _unpack_halves(buf_s[1])
    g0, g1 = gate[:, 0:1], gate[:, 1:2]
    y = jnp.concatenate([g0 * lo0 + g1 * lo1, g0 * hi0 + g1 * hi1], -1)
    out = _layer_norm(DN_ALPHA * x_ref[...] + y, g_ref[...], b_ref[...])
    o_ref[...] = out
    obf_ref[...] = out.astype(BF16)


def _combine(dest_flat, ys, x, gate_tk, g, bias):
    t, d = x.shape
    tm = _tile(t, 256)
    return pl.pallas_call(
        functools.partial(_combine_kernel, tm=tm, t_total=t),
        grid_spec=pltpu.PrefetchScalarGridSpec(
            num_scalar_prefetch=1,
            grid=(t // tm,),
            in_specs=[
                pl.BlockSpec(memory_space=pl.ANY),
                pl.BlockSpec((tm, d), lambda i, dest: (i, 0)),
                pl.BlockSpec((tm, TOP_K), lambda i, dest: (i, 0)),
                pl.BlockSpec((1, d), lambda i, dest: (0, 0)),
                pl.BlockSpec((1, d), lambda i, dest: (0, 0)),
            ],
            out_specs=[pl.BlockSpec((tm, d), lambda i, dest: (i, 0)),
                       pl.BlockSpec((tm, d), lambda i, dest: (i, 0))],
            scratch_shapes=[pltpu.VMEM((TOP_K, tm, d // 2), jnp.uint32), pltpu.SemaphoreType.DMA(())],
        ),
        out_shape=[jax.ShapeDtypeStruct((t, d), F32), jax.ShapeDtypeStruct((t, d), BF16)],
        compiler_params=_params("arbitrary"),
        name="moe_combine_ln",
    )(dest_flat, ys, x, gate_tk, g.reshape(1, d), bias.reshape(1, d))


def _moe(x, x_packed, w_router_t, router_bias, wg, wu, wd, layer, g, bias):
    t, d = x.shape
    idx, gate, rank, cnt = _router(x, w_router_t, router_bias)
    blk = MOE_BLOCK
    n_blocks = t * TOP_K // blk + N_EXPERTS
    counts = cnt[:, 0].astype(jnp.int32)
    padded = (counts + blk - 1) // blk * blk
    padded_end = jnp.cumsum(padded)
    padded_start = padded_end - padded
    dest = _slots(idx, rank, padded_start.astype(F32).reshape(N_EXPERTS, 1)).reshape(-1)
    n_used = (padded_end[-1] // blk).astype(jnp.int32)
    blocks = jnp.minimum(jnp.arange(n_blocks, dtype=jnp.int32), n_used - 1)
    block_expert = jnp.minimum(jnp.searchsorted(padded_end, blocks * blk, side='right'),
                               N_EXPERTS - 1).astype(jnp.int32)
    xs = _dispatch(dest, x_packed, n_blocks * blk)
    ys = _experts(block_expert, n_used.reshape(1), xs, wg, wu, wd, layer)
    return _combine(dest, ys, x, gate.T, g, bias)


def _in_proj_weights(w_in):
    sizes = (Q_LORA, KV_LORA, QK_ROPE, GDN_W, GDN_W, GDN_W, GDN_W, GDN_HEADS, GDN_HEADS, D_MODEL, D_MODEL)
    pts = np.cumsum(sizes)[:-1].tolist()
    c_q, c_kv, k_rope, g_q, g_k, g_v, g_z, g_b, g_a, gate_mla, gate_gdn = jnp.split(w_in, pts, axis=-1)
    half = QK_ROPE // 2
    k_rope_sw = jnp.concatenate([k_rope[:, half:], k_rope[:, :half]], -1)
    main = jnp.concatenate([g_q, g_k, g_v, g_z, gate_mla, gate_gdn], -1)
    used = Q_LORA + KV_LORA + 2 * QK_ROPE + 2 * GDN_HEADS
    small = jnp.concatenate([c_q, c_kv, k_rope, k_rope_sw, g_b, g_a,
                             jnp.zeros((w_in.shape[0], SMALL_W - used), w_in.dtype)], -1)
    return main.astype(BF16), small.astype(BF16)


def _mla_weights(w_uq, w_ukv):
    half = QK_ROPE // 2
    wq = w_uq.reshape(Q_LORA, MLA_HEADS, QK_DIM).transpose(1, 0, 2)
    pe = wq[..., QK_NOPE:]
    pe_sw = jnp.concatenate([pe[..., half:], pe[..., :half]], -1)
    wq = jnp.concatenate([wq, pe_sw], -1).astype(BF16)
    wkv = w_ukv.reshape(KV_LORA, MLA_HEADS, QK_NOPE + V_HEAD).transpose(1, 0, 2).astype(BF16)
    return wq, wkv


def _rope_table(positions):
    half = QK_ROPE // 2
    inv_freq = jnp.power(ROPE_BASE, -jnp.arange(half, dtype=F32) / half)
    ang = positions.astype(F32)[..., None] * inv_freq
    cos, sin = jnp.cos(ang), jnp.sin(ang)
    return jnp.concatenate([cos, cos, -sin, sin], -1)


def kernel(x, mem, positions, w_in, mla_q_norm, mla_kv_norm, w_uq, w_ukv, gdn_conv, gdn_a_log, gdn_dt_bias, gdn_o_norm, w_mla_proj, w_gdn_proj, w_mix_out, ln_mix_g, ln_mix_b, w_xq, w_xkv, w_xo, ln_x_g, ln_x_b, w_router, router_bias, w_gate, w_up, w_down, ln_moe_g, ln_moe_b):
    b, s, d = x.shape
    t = b * s
    cs = _rope_table(positions)
    mem_bf = mem.reshape(-1, d).astype(BF16)
    w_router_t = w_router.T.astype(F32)
    rbias = router_bias.reshape(N_EXPERTS, 1).astype(F32)
    x_bf = x.astype(BF16)
    for l in range(w_in.shape[0]):
        w_main, w_small = _in_proj_weights(w_in[l])
        wq, wkv = _mla_weights(w_uq[l], w_ukv[l])
        xt_bf = x_bf.reshape(t, d)
        main = _matmul(xt_bf, w_main, F32, 1024, 1024, "in_proj_main").reshape(b, s, MAIN_W)
        small = _matmul(xt_bf, w_small, F32, 1024, SMALL_W, "in_proj_small").reshape(b, s, SMALL_W)

        q, k, v = _mla_prep(small, cs, mla_q_norm[l].reshape(1, -1), mla_kv_norm[l].reshape(1, -1), wq, wkv)
        o_mla = _attention(q, k, v)

        small_t = small[:, :, SMALL_GATE_COL:SMALL_GATE_COL + 2 * GDN_HEADS].transpose(0, 2, 1)
        o_gdn = _gdn(main, gdn_conv[l], small, small_t, gdn_a_log[l], gdn_dt_bias[l], gdn_o_norm[l])

        merged = _merge(o_mla.reshape(t, d), o_gdn.reshape(t, d), main.reshape(t, MAIN_W),
                        w_mla_proj[l].astype(BF16), w_gdn_proj[l].astype(BF16))
        x1, x1_bf = _proj_ln(merged, w_mix_out[l].astype(BF16), x.reshape(t, d), ln_mix_g[l], ln_mix_b[l])

        kv = _matmul(mem_bf, w_xkv[l].astype(BF16), BF16, 1024, 1024, "mem_kv").reshape(b, -1, 2 * X_W)
        x2, x2_pk = _xattn(x1_bf.reshape(b, s, d), x1.reshape(b, s, d), w_xq[l].astype(BF16), kv,
                           w_xo[l].astype(BF16), ln_x_g[l], ln_x_b[l])

        x3, x3_bf = _moe(x2.reshape(t, d), x2_pk.reshape(t, d // 2), w_router_t, rbias, w_gate, w_up, w_down, l,
                         ln_moe_g[l], ln_moe_b[l])
        x, x_bf = x3.reshape(b, s, d), x3_bf.reshape(b, s, d)
    return x
```

```python
import functools
import math

import jax
import jax.numpy as jnp
import numpy as np
from jax import lax
from jax.experimental import pallas as pl
from jax.experimental.pallas import tpu as pltpu

D_MODEL = 2048
DEPTH = 2
MLA_HEADS = 16
Q_LORA = 512
KV_LORA = 512
QK_NOPE = 128
QK_ROPE = 64
V_HEAD = 128
QK_DIM = QK_NOPE + QK_ROPE
ROPE_BASE = 10000.0
GDN_HEADS = 16
GDN_DK = 128
GDN_DV = 128
CONV_K = 4
CHUNK = 64
GDN_W = GDN_HEADS * GDN_DK
X_HEADS = 4
X_HEAD_DIM = 128
X_W = X_HEADS * X_HEAD_DIM
N_EXPERTS = 64
N_GROUPS = 8
EXPERTS_PER_GROUP = N_EXPERTS // N_GROUPS
TOP_K = 2
D_EXPERT = 512
Q_PRESCALE = QK_DIM ** -0.5 * math.log2(math.e)
DN_ALPHA = (2.0 * DEPTH) ** 0.25
LN_EPS = 1e-5
RMS_EPS = 1e-6

MAIN_W = 3 * GDN_W + GDN_W + 2 * D_MODEL
SMALL_W = 1280
SMALL_ROPE_BLOCK = (Q_LORA + KV_LORA) // 128
SMALL_GATE_COL = Q_LORA + KV_LORA + 2 * QK_ROPE
SMALL_GATE_BLOCK = SMALL_GATE_COL // 128

GDN_HEAD_BLOCK = 8
GDN_ROW_BLOCK = 128
MOE_BLOCK = 256
DMA_ISSUE_UNROLL = 8
VMEM_LIMIT = 56 * 1024 * 1024

BF16 = jnp.bfloat16
F32 = jnp.float32


def _tile(n, pref):
    return pref if n % pref == 0 else n


def _params(*sem):
    return pltpu.CompilerParams(dimension_semantics=sem, vmem_limit_bytes=VMEM_LIMIT)


def _sigmoid(x):
    return 0.5 * jnp.tanh(0.5 * x) + 0.5


def _pack_halves(y):
    n = y.shape[1] // 2
    lo = pltpu.bitcast(y[:, :n].astype(BF16).astype(F32), jnp.uint32)
    hi = pltpu.bitcast(y[:, n:].astype(BF16).astype(F32), jnp.uint32)
    return (lo >> 16) | (hi & jnp.uint32(0xFFFF0000))


def _unpack_halves(p):
    lo = pltpu.bitcast(p << 16, F32)
    hi = pltpu.bitcast(p & jnp.uint32(0xFFFF0000), F32)
    return lo, hi


def _softplus(x):
    return jnp.maximum(x, 0.0) + jnp.log1p(jnp.exp(-jnp.abs(x)))


def _layer_norm(y, g, b):
    mu = jnp.mean(y, -1, keepdims=True)
    d = y - mu
    var = jnp.mean(d * d, -1, keepdims=True)
    return d * lax.rsqrt(var + LN_EPS) * g + b


def _dot(a, b):
    return jnp.dot(a, b, preferred_element_type=F32)


def _dot_nt(a, b):
    return lax.dot_general(a, b, (((1,), (1,)), ((), ())), preferred_element_type=F32)


def _dot_tn(a, b):
    return lax.dot_general(a, b, (((0,), (0,)), ((), ())), preferred_element_type=F32)


def _mm_kernel(a_ref, w_ref, o_ref, a_s):
    @pl.when(pl.program_id(1) == 0)
    def _():
        a_s[...] = a_ref[...].astype(BF16)

    o_ref[...] = _dot(a_s[...], w_ref[...]).astype(o_ref.dtype)


def _matmul(a, w, out_dtype, tm, tn, name):
    m, k = a.shape
    n = w.shape[1]
    tm, tn = _tile(m, tm), _tile(n, tn)
    return pl.pallas_call(
        _mm_kernel,
        grid=(m // tm, n // tn),
        in_specs=[pl.BlockSpec((tm, k), lambda i, j: (i, 0)),
                  pl.BlockSpec((k, tn), lambda i, j: (0, j))],
        out_specs=pl.BlockSpec((tm, tn), lambda i, j: (i, j)),
        out_shape=jax.ShapeDtypeStruct((m, n), out_dtype),
        scratch_shapes=[pltpu.VMEM((tm, k), BF16)],
        compiler_params=_params("parallel", "arbitrary"),
        name=name,
    )(a, w)


def _mla_prep_kernel(cq_ref, ckv_ref, kr_ref, cs_ref, qn_ref, kvn_ref, wq_ref, wkv_ref, q_ref, k_ref, v_ref):
    ts = cq_ref.shape[1]
    cq = cq_ref[0]
    cqn = (cq * lax.rsqrt(jnp.mean(cq * cq, -1, keepdims=True) + RMS_EPS) * qn_ref[...]).astype(BF16)
    ckv = ckv_ref[0]
    ckvn = (ckv * lax.rsqrt(jnp.mean(ckv * ckv, -1, keepdims=True) + RMS_EPS) * kvn_ref[...]).astype(BF16)
    cs = cs_ref[0]
    lane = lax.broadcasted_iota(jnp.int32, (ts, 128), 1)

    def rotate(t):
        u = t * cs
        return jnp.where(lane < QK_ROPE, u + pltpu.roll(u, QK_ROPE, 1), 0.0).astype(BF16)

    kpe = rotate(kr_ref[0])
    ones = jnp.ones((ts, V_HEAD), BF16)
    for h in range(MLA_HEADS):
        rq = _dot(cqn, wq_ref[h]) * Q_PRESCALE
        q_ref[0, h, :, :QK_NOPE] = rq[:, :QK_NOPE].astype(BF16)
        q_ref[0, h, :, QK_NOPE:] = rotate(rq[:, QK_NOPE:])
        rkv = _dot(ckvn, wkv_ref[h])
        k_ref[0, h, :, :QK_NOPE] = rkv[:, :QK_NOPE].astype(BF16)
        k_ref[0, h, :, QK_NOPE:] = kpe
        v_ref[0, h, :, :V_HEAD] = rkv[:, QK_NOPE:].astype(BF16)
        v_ref[0, h, :, V_HEAD:] = ones


def _mla_prep(small, cs, q_norm, kv_norm, wq, wkv):
    b, s, _ = small.shape
    ts = _tile(s, 512)
    hq = MLA_HEADS
    out_spec = pl.BlockSpec((1, hq, ts, 256), lambda bi, si: (bi, 0, si, 0))
    out_shape = jax.ShapeDtypeStruct((b, hq, s, 256), BF16)
    return pl.pallas_call(
        _mla_prep_kernel,
        grid=(b, s // ts),
        in_specs=[
            pl.BlockSpec((1, ts, Q_LORA), lambda bi, si: (bi, si, 0)),
            pl.BlockSpec((1, ts, KV_LORA), lambda bi, si: (bi, si, 1)),
            pl.BlockSpec((1, ts, 128), lambda bi, si: (bi, si, SMALL_ROPE_BLOCK)),
            pl.BlockSpec((1, ts, 128), lambda bi, si: (bi, si, 0)),
            pl.BlockSpec((1, Q_LORA), lambda bi, si: (0, 0)),
            pl.BlockSpec((1, KV_LORA), lambda bi, si: (0, 0)),
            pl.BlockSpec((hq, Q_LORA, 256), lambda bi, si: (0, 0, 0)),
            pl.BlockSpec((hq, KV_LORA, 256), lambda bi, si: (0, 0, 0)),
        ],
        out_specs=[out_spec, out_spec, out_spec],
        out_shape=[out_shape, out_shape, out_shape],
        compiler_params=_params("parallel", "parallel"),
        name="mla_prep",
    )(small, small, small, cs, q_norm, kv_norm, wq, wkv)


def _attn_kernel(q_ref, k_ref, v_ref, o_ref, sa_s, sb_s, m_s, acc_s, *, tq, rc):
    i = pl.program_id(2)
    dv2 = acc_s.shape[-1]
    m_s[...] = jnp.full(m_s.shape, -jnp.inf, F32)
    acc_s[...] = jnp.zeros(acc_s.shape, F32)

    def scores(j):
        start = pl.multiple_of(j * tq, tq)
        return _dot_nt(q_ref[0, 0], k_ref[0, 0, pl.ds(start, tq), :])

    def softmax_pv(s_ref, c, j, nk, masked):
        rows = slice(c * rc, (c + 1) * rc)
        start = pl.multiple_of(j * tq, tq)
        s = s_ref[rows, :nk]
        if masked:
            row = lax.broadcasted_iota(jnp.int32, (rc, nk), 0) + c * rc
            col = lax.broadcasted_iota(jnp.int32, (rc, nk), 1)
            s = jnp.where(col <= row, s, -jnp.inf)
        m_prev = m_s[rows, :]
        m_new = jnp.maximum(m_prev, jnp.max(s, -1, keepdims=True))
        alpha = jnp.exp2(m_prev - m_new)
        p = jnp.exp2(s - jnp.concatenate([m_new] * (nk // 128), -1))
        acc_s[rows, :] = (jnp.concatenate([alpha] * (dv2 // 128), -1) * acc_s[rows, :]
                          + _dot(p.astype(BF16), v_ref[0, 0, pl.ds(start, nk), :]))
        m_s[rows, :] = m_new

    nch = tq // rc

    def step(cur_ref, nxt_ref, j):
        nxt_ref[...] = scores(j + 1)
        for c in range(nch):
            softmax_pv(cur_ref, c, j, tq, False)

    def diagonal(cur_ref):
        for c in range(nch):
            softmax_pv(cur_ref, c, i, (c + 1) * rc, True)
        dv = dv2 // 2
        o_ref[0] = (acc_s[:, :dv] / acc_s[:, dv:]).astype(o_ref.dtype)

    sa_s[...] = scores(0)

    def pair(jj, carry):
        step(sa_s, sb_s, 2 * jj)
        step(sb_s, sa_s, 2 * jj + 1)
        return carry

    lax.fori_loop(0, i // 2, pair, 0)

    @pl.when(i % 2 == 0)
    def _():
        diagonal(sa_s)

    @pl.when(i % 2 == 1)
    def _():
        step(sa_s, sb_s, i - 1)
        diagonal(sb_s)


def _attention(q, k, v):
    b, h, s, dqk = q.shape
    dv2 = v.shape[-1]
    tq = _tile(s, 1024)
    rc = _tile(tq, 256)
    kern = functools.partial(_attn_kernel, tq=tq, rc=rc)
    return pl.pallas_call(
        kern,
        grid=(b, h, s // tq),
        in_specs=[
            pl.BlockSpec((1, 1, tq, dqk), lambda bi, hi, i: (bi, hi, i, 0)),
            pl.BlockSpec((1, 1, s, dqk), lambda bi, hi, i: (bi, hi, 0, 0)),
            pl.BlockSpec((1, 1, s, dv2), lambda bi, hi, i: (bi, hi, 0, 0)),
        ],
        out_specs=pl.BlockSpec((1, tq, dv2 // 2), lambda bi, hi, i: (bi, i, hi)),
        out_shape=jax.ShapeDtypeStruct((b, s, h * dv2 // 2), BF16),
        scratch_shapes=[pltpu.VMEM((tq, tq), F32), pltpu.VMEM((tq, tq), F32), pltpu.VMEM((tq, 128), F32),
                        pltpu.VMEM((tq, dv2), F32)],
        compiler_params=_params("parallel", "parallel", "arbitrary"),
        name="mla_attention",
    )(q, k, v)


def _gdn_kernel(qr_ref, kr_ref, vr_ref, wq_ref, wk_ref, wv_ref, z_ref, gate_ref, gatet_ref, arow_ref, dtrow_ref,
                acol_ref, dtcol_ref, onorm_ref, o_ref, state_s, xx_s, qkv_s, *, hb, rb):
    hg = pl.program_id(1)
    r = pl.program_id(2)

    @pl.when(r == 0)
    def _():
        state_s[...] = jnp.zeros(state_s.shape, F32)
        xx_s[:, 0:8, :] = jnp.zeros((3, 8, hb * 128), F32)

    @pl.when(r > 0)
    def _():
        xx_s[:, 0:8, :] = xx_s[:, rb:rb + 8, :]

    for ti, (x_ref, w_ref) in enumerate(((qr_ref, wq_ref), (kr_ref, wk_ref), (vr_ref, wv_ref))):
        xx_s[ti, 8:rb + 8, :] = x_ref[0]
        wc = w_ref[...]
        y = wc[CONV_K - 1:CONV_K, :] * xx_s[ti, 8:rb + 8, :]
        for j in range(CONV_K - 1):
            off = 8 - (CONV_K - 1) + j
            y = y + wc[j:j + 1, :] * xx_s[ti, off:off + rb, :]
        y = y * _sigmoid(y)
        if ti == 2:
            qkv_s[ti] = y
        else:
            for h in range(hb):
                yh = y[:, h * 128:(h + 1) * 128]
                yn = yh * lax.rsqrt(jnp.sum(yh * yh, -1, keepdims=True) + RMS_EPS)
                qkv_s[ti, :, h * 128:(h + 1) * 128] = yn * (GDN_DK ** -0.5) if ti == 0 else yn

    def head_rows(ti, c, h):
        return qkv_s[ti, c * CHUNK:(c + 1) * CHUNK, h * 128:(h + 1) * 128]

    blk = gate_ref[0]
    beta_tm = _sigmoid(blk)
    g_tm = -jnp.exp(arow_ref[...]) * _softplus(blk + dtrow_ref[...])
    g_t = -jnp.exp(acol_ref[...]) * _softplus(gatet_ref[0] + dtcol_ref[...])

    ri = lax.broadcasted_iota(jnp.int32, (CHUNK, CHUNK), 0)
    ci = lax.broadcasted_iota(jnp.int32, (CHUNK, CHUNK), 1)
    causal = ri >= ci
    strict = ri > ci
    tri = jnp.where(causal, 1.0, 0.0).astype(F32)
    tri_t = jnp.where(ri <= ci, 1.0, 0.0).astype(F32)
    eye = jnp.where(ri == ci, 1.0, 0.0).astype(F32)
    onorm = onorm_ref[...]

    nc = rb // CHUNK
    items = [(c, h) for c in range(nc) for h in range(hb)]
    rows = [slice(c * CHUNK, (c + 1) * CHUNK) for c in range(nc)]
    gc_tm = [jnp.dot(tri, g_tm[rows[c], :], precision=lax.Precision.HIGHEST, preferred_element_type=F32)
             for c in range(nc)]
    gc_t = [jnp.dot(g_t[:, rows[c]], tri_t, precision=lax.Precision.HIGHEST, preferred_element_type=F32)
            for c in range(nc)]
    sel_l = lax.broadcasted_iota(jnp.int32, (CHUNK, 128), 1)
    sel_s = lax.broadcasted_iota(jnp.int32, (32, CHUNK), 0)
    gcol, bcol, grow, glast, egc, kt, kb = {}, {}, {}, {}, {}, {}, {}
    for it in items:
        c, h = it
        lane = GDN_HEADS + hg * hb + h
        gcol[it] = jnp.sum(jnp.where(sel_l == lane, gc_tm[c], 0.0), -1, keepdims=True)
        bcol[it] = jnp.sum(jnp.where(sel_l == hg * hb + h, beta_tm[rows[c], :], 0.0), -1, keepdims=True)
        grow[it] = jnp.sum(jnp.where(sel_s == lane, gc_t[c], 0.0), 0, keepdims=True)
        glast[it] = gcol[it][CHUNK - 1:CHUNK, :]
        egc[it] = jnp.exp(gcol[it])
        k = head_rows(1, c, h)
        kt[it] = k.T
        kb[it] = k * bcol[it]
    kq = {it: _dot(jnp.concatenate([kb[it], head_rows(0, *it)], 0).astype(BF16),
                   kt[it].astype(BF16)) for it in items}
    amat, m, inv = {}, {}, {}
    for it in items:
        decay = jnp.exp(jnp.where(causal, gcol[it] - grow[it], -jnp.inf))
        amat[it] = (kq[it][CHUNK:] * decay).astype(BF16)
        m[it] = -jnp.where(strict, kq[it][:CHUNK] * decay, 0.0)
        inv[it] = eye + m[it]
    for it in items:
        mb = m[it].astype(BF16)
        m[it] = _dot(mb, mb)
    n_fac = int(round(math.log2(CHUNK))) - 1
    for t in range(n_fac):
        for it in items:
            mb = m[it].astype(BF16)
            if t < n_fac - 1:
                r2 = _dot(jnp.concatenate([inv[it], m[it]], 0).astype(BF16), mb)
                inv[it] = inv[it] + r2[:CHUNK]
                m[it] = r2[CHUNK:]
            else:
                inv[it] = inv[it] + _dot(inv[it].astype(BF16), mb)
    uw = {}
    for it in items:
        c, h = it
        vb = head_rows(2, c, h) * bcol[it]
        uw[it] = _dot(inv[it].astype(BF16), jnp.concatenate([vb, kb[it] * egc[it]], 1).astype(BF16))
    for c in range(nc):
        its = [(c, h) for h in range(hb)]
        st = {it: state_s[it[1]] for it in its}
        wq = {it: _dot(jnp.concatenate([uw[it][:, GDN_DV:], head_rows(0, *it) * egc[it]], 0).astype(BF16),
                       st[it].astype(BF16)) for it in its}
        vnb = {it: (uw[it][:, :GDN_DV] - wq[it][:CHUNK]).astype(BF16) for it in its}
        o = {it: wq[it][CHUNK:] + _dot(amat[it], vnb[it]) for it in its}
        for it in its:
            h = it[1]
            kdt = kt[it] * jnp.exp(glast[it] - grow[it])
            state_s[h] = st[it] * jnp.exp(glast[it]) + _dot(kdt.astype(BF16), vnb[it])
        for it in its:
            h = it[1]
            on = o[it] * lax.rsqrt(jnp.mean(o[it] * o[it], -1, keepdims=True) + RMS_EPS) * onorm
            zz = z_ref[0, rows[c], h * GDN_DV:(h + 1) * GDN_DV]
            o_ref[0, rows[c], h * GDN_DV:(h + 1) * GDN_DV] = (on * (zz * _sigmoid(zz))).astype(o_ref.dtype)


def _gdn(main, w_conv, small, small_t, a_log, dt_bias, o_norm):
    b, s, _ = main.shape
    hb = GDN_HEAD_BLOCK
    rb = _tile(s, GDN_ROW_BLOCK)
    ng = GDN_HEADS // hb
    zoff = 3 * GDN_W // (hb * 128)
    pad = lambda a, lo: jnp.zeros((128,), F32).at[lo:lo + GDN_HEADS].set(a.astype(F32))
    arow = pad(a_log, GDN_HEADS).reshape(1, 128)
    dtrow = pad(dt_bias, GDN_HEADS).reshape(1, 128)
    acol = arow[0, :32].reshape(32, 1)
    dtcol = dtrow[0, :32].reshape(32, 1)
    kern = functools.partial(_gdn_kernel, hb=hb, rb=rb)
    return pl.pallas_call(
        kern,
        grid=(b, ng, s // rb),
        in_specs=[
            pl.BlockSpec((1, rb, hb * 128), lambda bi, g, r: (bi, r, g)),
            pl.BlockSpec((1, rb, hb * 128), lambda bi, g, r: (bi, r, ng + g)),
            pl.BlockSpec((1, rb, hb * 128), lambda bi, g, r: (bi, r, 2 * ng + g)),
            pl.BlockSpec((CONV_K, hb * 128), lambda bi, g, r: (0, g)),
            pl.BlockSpec((CONV_K, hb * 128), lambda bi, g, r: (0, ng + g)),
            pl.BlockSpec((CONV_K, hb * 128), lambda bi, g, r: (0, 2 * ng + g)),
            pl.BlockSpec((1, rb, hb * 128), lambda bi, g, r: (bi, r, zoff + g)),
            pl.BlockSpec((1, rb, 128), lambda bi, g, r: (bi, r, SMALL_GATE_BLOCK)),
            pl.BlockSpec((1, 32, rb), lambda bi, g, r: (bi, 0, r)),
            pl.BlockSpec((1, 128), lambda bi, g, r: (0, 0)),
            pl.BlockSpec((1, 128), lambda bi, g, r: (0, 0)),
            pl.BlockSpec((32, 1), lambda bi, g, r: (0, 0)),
            pl.BlockSpec((32, 1), lambda bi, g, r: (0, 0)),
            pl.BlockSpec((1, 128), lambda bi, g, r: (0, 0)),
        ],
        out_specs=pl.BlockSpec((1, rb, hb * 128), lambda bi, g, r: (bi, r, g)),
        out_shape=jax.ShapeDtypeStruct((b, s, GDN_W), BF16),
        scratch_shapes=[pltpu.VMEM((hb, GDN_DK, GDN_DV), F32), pltpu.VMEM((3, rb + 8, hb * 128), F32),
                        pltpu.VMEM((3, rb, hb * 128), F32)],
        compiler_params=_params("parallel", "parallel", "arbitrary"),
        name="gdn_delta_rule",
    )(main, main, main, w_conv, w_conv, w_conv, main, small, small_t, arow, dtrow, acol, dtcol,
      o_norm.reshape(1, GDN_DV).astype(F32))


def _merge_kernel(om_ref, og_ref, gm_ref, gg_ref, wm_ref, wg_ref, o_ref):
    ym = _dot(om_ref[...], wm_ref[...])
    yg = _dot(og_ref[...], wg_ref[...])
    o_ref[...] = (_sigmoid(gm_ref[...]) * ym + _sigmoid(gg_ref[...]) * yg).astype(o_ref.dtype)


def _merge(o_mla, o_gdn, main, w_mla, w_gdn):
    t, d = o_mla.shape
    tm, tn = _tile(t, 512), _tile(d, 1024)
    goff = (4 * GDN_W) // tn
    nn = d // tn
    return pl.pallas_call(
        _merge_kernel,
        grid=(nn, t // tm),
        in_specs=[
            pl.BlockSpec((tm, d), lambda j, i: (i, 0)),
            pl.BlockSpec((tm, d), lambda j, i: (i, 0)),
            pl.BlockSpec((tm, tn), lambda j, i: (i, goff + j)),
            pl.BlockSpec((tm, tn), lambda j, i: (i, goff + nn + j)),
            pl.BlockSpec((d, tn), lambda j, i: (0, j)),
            pl.BlockSpec((d, tn), lambda j, i: (0, j)),
        ],
        out_specs=pl.BlockSpec((tm, tn), lambda j, i: (i, j)),
        out_shape=jax.ShapeDtypeStruct((t, d), BF16),
        compiler_params=_params("parallel", "parallel"),
        name="mixer_merge",
    )(o_mla, o_gdn, main, main, w_mla, w_gdn)


def _proj_ln_kernel(a_ref, w_ref, x_ref, g_ref, b_ref, o_ref, obf_ref):
    y = DN_ALPHA * x_ref[...] + _dot(a_ref[...], w_ref[...])
    out = _layer_norm(y, g_ref[...], b_ref[...])
    o_ref[...] = out
    obf_ref[...] = out.astype(BF16)


def _proj_ln(a, w, x, g, bias):
    t, k = a.shape
    d = w.shape[1]
    tm = _tile(t, 512)
    return pl.pallas_call(
        _proj_ln_kernel,
        grid=(t // tm,),
        in_specs=[
            pl.BlockSpec((tm, k), lambda i: (i, 0)),
            pl.BlockSpec((k, d), lambda i: (0, 0)),
            pl.BlockSpec((tm, d), lambda i: (i, 0)),
            pl.BlockSpec((1, d), lambda i: (0, 0)),
            pl.BlockSpec((1, d), lambda i: (0, 0)),
        ],
        out_specs=[pl.BlockSpec((tm, d), lambda i: (i, 0)), pl.BlockSpec((tm, d), lambda i: (i, 0))],
        out_shape=[jax.ShapeDtypeStruct((t, d), F32), jax.ShapeDtypeStruct((t, d), BF16)],
        compiler_params=_params("parallel"),
        name="proj_residual_ln",
    )(a, w, x, g.reshape(1, d), bias.reshape(1, d))


def _xattn_kernel(xbf_ref, x_ref, wq_ref, kv_ref, wo_ref, g_ref, b_ref, o_ref, opk_ref):
    q = _dot(xbf_ref[0], wq_ref[...])
    scale = X_HEAD_DIM ** -0.5
    outs = []
    for h in range(X_HEADS):
        qh = q[:, h * X_HEAD_DIM:(h + 1) * X_HEAD_DIM].astype(BF16)
        kh = kv_ref[0, :, h * X_HEAD_DIM:(h + 1) * X_HEAD_DIM]
        vh = kv_ref[0, :, X_W + h * X_HEAD_DIM:X_W + (h + 1) * X_HEAD_DIM]
        s = _dot_nt(qh, kh) * scale
        e = jnp.exp(s - jnp.max(s, -1, keepdims=True))
        p = e / jnp.sum(e, -1, keepdims=True)
        outs.append(_dot(p.astype(BF16), vh))
    o = jnp.concatenate(outs, -1).astype(BF16)
    y = DN_ALPHA * x_ref[0] + _dot(o, wo_ref[...])
    out = _layer_norm(y, g_ref[...], b_ref[...])
    o_ref[0] = out
    opk_ref[0] = _pack_halves(out)


def _xattn(x_bf, x, wq, kv, wo, g, bias):
    b, s, d = x.shape
    tm = _tile(s, 512)
    ml = kv.shape[1]
    return pl.pallas_call(
        _xattn_kernel,
        grid=(b, s // tm),
        in_specs=[
            pl.BlockSpec((1, tm, d), lambda bi, i: (bi, i, 0)),
            pl.BlockSpec((1, tm, d), lambda bi, i: (bi, i, 0)),
            pl.BlockSpec((d, X_W), lambda bi, i: (0, 0)),
            pl.BlockSpec((1, ml, 2 * X_W), lambda bi, i: (bi, 0, 0)),
            pl.BlockSpec((X_W, d), lambda bi, i: (0, 0)),
            pl.BlockSpec((1, d), lambda bi, i: (0, 0)),
            pl.BlockSpec((1, d), lambda bi, i: (0, 0)),
        ],
        out_specs=[pl.BlockSpec((1, tm, d), lambda bi, i: (bi, i, 0)),
                   pl.BlockSpec((1, tm, d // 2), lambda bi, i: (bi, i, 0))],
        out_shape=[jax.ShapeDtypeStruct((b, s, d), F32), jax.ShapeDtypeStruct((b, s, d // 2), jnp.uint32)],
        compiler_params=_params("parallel", "parallel"),
        name="memory_cross_attention",
    )(x_bf, x, wq, kv, wo, g.reshape(1, d), bias.reshape(1, d))


def _first_argmax(vals, iota, n):
    m = jnp.max(vals, 0, keepdims=True)
    idx = jnp.min(jnp.where(vals == m, iota, n), 0, keepdims=True)
    return m, idx


def _router_kernel(x_ref, wr_ref, bias_ref, idx_ref, gate_ref, rank_ref, cnt_ref, carry_s, *, tm):
    i = pl.program_id(0)

    @pl.when(i == 0)
    def _():
        carry_s[...] = jnp.zeros(carry_s.shape, F32)

    logits = lax.dot_general(wr_ref[...], x_ref[...], (((1,), (1,)), ((), ())),
                             precision=lax.Precision.HIGHEST, preferred_element_type=F32)
    scores = 1.0 / (1.0 + jnp.exp(-logits))
    sel = scores + bias_ref[...]
    epg = EXPERTS_PER_GROUP
    io8 = lax.broadcasted_iota(jnp.int32, (epg, tm), 0)
    best_score = None
    for g in range(N_GROUPS):
        sg = sel[g * epg:(g + 1) * epg, :]
        m1, i1 = _first_argmax(sg, io8, epg)
        m2 = jnp.max(jnp.where(io8 == i1, -jnp.inf, sg), 0, keepdims=True)
        gs = m1 + m2
        if g == 0:
            best_score, best = gs, jnp.zeros((1, tm), jnp.int32)
            in_sel, in_sc = sg, scores[0:epg, :]
        else:
            better = gs > best_score
            best_score = jnp.where(better, gs, best_score)
            best = jnp.where(better, g, best)
            in_sel = jnp.where(better, sg, in_sel)
            in_sc = jnp.where(better, scores[g * epg:(g + 1) * epg, :], in_sc)
    _, l1 = _first_argmax(in_sel, io8, epg)
    _, l2 = _first_argmax(jnp.where(io8 == l1, -jnp.inf, in_sel), io8, epg)
    g1 = jnp.sum(jnp.where(io8 == l1, in_sc, 0.0), 0, keepdims=True)
    g2 = jnp.sum(jnp.where(io8 == l2, in_sc, 0.0), 0, keepdims=True)
    e1 = best * epg + l1
    e2 = best * epg + l2
    idx_ref[0:1, :] = e1
    idx_ref[1:2, :] = e2
    gsum = g1 + g2
    gate_ref[0:1, :] = g1 / gsum
    gate_ref[1:2, :] = g2 / gsum

    ioe = lax.broadcasted_iota(jnp.int32, (N_EXPERTS, tm), 0)
    oh1 = ioe == e1
    oh2 = ioe == e2
    onehot = jnp.where(oh1, 1.0, 0.0) + jnp.where(oh2, 1.0, 0.0)
    rr = lax.broadcasted_iota(jnp.int32, (tm, tm), 0)
    cc = lax.broadcasted_iota(jnp.int32, (tm, tm), 1)
    upper = jnp.where(rr < cc, 1.0, 0.0).astype(BF16)
    tot = carry_s[...] + _dot(onehot.astype(BF16), upper)
    rank_ref[0:1, :] = jnp.sum(jnp.where(oh1, tot, 0.0), 0, keepdims=True).astype(jnp.int32)
    rank_ref[1:2, :] = jnp.sum(jnp.where(oh2, tot, 0.0), 0, keepdims=True).astype(jnp.int32)
    new = carry_s[...] + jnp.sum(onehot, -1, keepdims=True)
    carry_s[...] = new
    cnt_ref[...] = jnp.broadcast_to(new, cnt_ref.shape)


def _router(x, w_router_t, bias):
    t, d = x.shape
    tm = _tile(t, 512)
    return pl.pallas_call(
        functools.partial(_router_kernel, tm=tm),
        grid=(t // tm,),
        in_specs=[pl.BlockSpec((tm, d), lambda i: (i, 0)),
                  pl.BlockSpec((N_EXPERTS, d), lambda i: (0, 0)),
                  pl.BlockSpec((N_EXPERTS, 1), lambda i: (0, 0))],
        out_specs=[pl.BlockSpec((TOP_K, tm), lambda i: (0, i)),
                   pl.BlockSpec((TOP_K, tm), lambda i: (0, i)),
                   pl.BlockSpec((TOP_K, tm), lambda i: (0, i)),
                   pl.BlockSpec((N_EXPERTS, 128), lambda i: (0, 0))],
        out_shape=[jax.ShapeDtypeStruct((TOP_K, t), jnp.int32),
                   jax.ShapeDtypeStruct((TOP_K, t), F32),
                   jax.ShapeDtypeStruct((TOP_K, t), jnp.int32),
                   jax.ShapeDtypeStruct((N_EXPERTS, 128), F32)],
        scratch_shapes=[pltpu.VMEM((N_EXPERTS, 1), F32)],
        compiler_params=_params("arbitrary"),
        name="moe_router",
    )(x, w_router_t, bias)


def _slot_kernel(idx_ref, rank_ref, pstart_ref, dest_ref, *, tm):
    ioe = lax.broadcasted_iota(jnp.int32, (N_EXPERTS, tm), 0)
    pstart = pstart_ref[...]
    for k in range(TOP_K):
        start = jnp.sum(jnp.where(ioe == idx_ref[k:k + 1, :], pstart, 0.0), 0, keepdims=True)
        dest_ref[k:k + 1, :] = start.astype(jnp.int32) + rank_ref[k:k + 1, :]


def _slots(idx, rank, pstart):
    t = idx.shape[1]
    tm = _tile(t, 2048)
    spec = pl.BlockSpec((TOP_K, tm), lambda i: (0, i))
    return pl.pallas_call(
        functools.partial(_slot_kernel, tm=tm),
        grid=(t // tm,),
        in_specs=[spec, spec, pl.BlockSpec((N_EXPERTS, 1), lambda i: (0, 0))],
        out_specs=spec,
        out_shape=jax.ShapeDtypeStruct((TOP_K, t), jnp.int32),
        compiler_params=_params("parallel"),
        name="moe_slots",
    )(idx, rank, pstart)


def _dispatch_kernel(dest_ref, x_ref, xs_in_ref, xs_ref, sem, *, tm, t_total):
    del xs_in_ref
    base = pl.program_id(0) * tm

    def issue(t, carry):
        for k in range(TOP_K):
            d = dest_ref[k * t_total + base + t]
            pltpu.make_async_copy(x_ref.at[pl.ds(t, 1), :], xs_ref.at[pl.ds(d, 1), :], sem).start()
        return carry

    lax.fori_loop(0, tm, issue, 0, unroll=DMA_ISSUE_UNROLL)

    def drain(t, carry):
        for k in range(TOP_K):
            pltpu.make_async_copy(x_ref.at[pl.ds(0, 1), :], xs_ref.at[pl.ds(0, 1), :], sem).wait()
        return carry

    lax.fori_loop(0, tm, drain, 0, unroll=DMA_ISSUE_UNROLL)


def _dispatch(dest_flat, x, n_slots):
    t, d = x.shape
    tm = _tile(t, 256)
    xs0 = jnp.zeros((n_slots, d), x.dtype)
    return pl.pallas_call(
        functools.partial(_dispatch_kernel, tm=tm, t_total=t),
        grid_spec=pltpu.PrefetchScalarGridSpec(
            num_scalar_prefetch=1,
            grid=(t // tm,),
            in_specs=[pl.BlockSpec((tm, d), lambda i, dest: (i, 0)),
                      pl.BlockSpec(memory_space=pl.ANY)],
            out_specs=pl.BlockSpec(memory_space=pl.ANY),
            scratch_shapes=[pltpu.SemaphoreType.DMA(())],
        ),
        out_shape=jax.ShapeDtypeStruct((n_slots, d), x.dtype),
        input_output_aliases={2: 0},
        compiler_params=_params("arbitrary"),
        name="moe_dispatch",
    )(dest_flat, x, xs0)


def _expert_kernel(be_ref, nu_ref, slot_ref, nx_ref, xs_ref, wg_hbm, wu_hbm, wd_hbm, ys_ref,
                   fg_s, fu_s, fd_s, wg_s, wu_s, wd_s, sems, *, layer):
    i = pl.program_id(0)
    e = be_ref[i]

    def weight_copies(expert, slot):
        return (pltpu.make_async_copy(wg_hbm.at[layer, expert], fg_s.at[slot], sems.at[slot, 0]),
                pltpu.make_async_copy(wu_hbm.at[layer, expert], fu_s.at[slot], sems.at[slot, 1]),
                pltpu.make_async_copy(wd_hbm.at[layer, expert], fd_s.at[slot], sems.at[slot, 2]))

    @pl.when(i == 0)
    def _():
        for cp in weight_copies(e, 0):
            cp.start()

    @pl.when((i == 0) | (e != be_ref[jnp.maximum(i - 1, 0)]))
    def _():
        slot = slot_ref[i]
        for cp in weight_copies(e, slot):
            cp.wait()
        wg_s[...] = fg_s[slot].astype(BF16)
        wu_s[...] = fu_s[slot].astype(BF16)
        wd_s[...] = fd_s[slot].astype(BF16)

        @pl.when(nx_ref[i] >= 0)
        def _():
            for cp in weight_copies(nx_ref[i], 1 - slot):
                cp.start()

    @pl.when(i < nu_ref[0])
    def _():
        lo, hi = _unpack_halves(xs_ref[...])
        xb = jnp.concatenate([lo.astype(BF16), hi.astype(BF16)], -1)
        hg = _dot(xb, wg_s[...])
        hu = _dot(xb, wu_s[...])
        h = (hg * _sigmoid(hg)) * hu
        ys_ref[...] = _pack_halves(_dot(h.astype(BF16), wd_s[...]))

    @pl.when(i >= nu_ref[0])
    def _():
        ys_ref[...] = jnp.zeros(ys_ref.shape, ys_ref.dtype)


def _experts(block_expert, n_used, xs, wg, wu, wd, layer):
    n_slots, dp = xs.shape
    d = 2 * dp
    blk = MOE_BLOCK
    n_blocks = n_slots // blk
    de = wg.shape[-1]
    pos = jnp.arange(n_blocks, dtype=jnp.int32)
    first = jnp.concatenate([jnp.ones((1,), bool), block_expert[1:] != block_expert[:-1]])
    slot = ((jnp.cumsum(first.astype(jnp.int32)) - 1) % 2).astype(jnp.int32)
    first_pos = jnp.where(first, pos, n_blocks)
    next_first = lax.cummin(jnp.concatenate([first_pos[1:], jnp.full((1,), n_blocks, jnp.int32)]), reverse=True)
    next_expert = jnp.where(next_first < n_blocks, block_expert[jnp.minimum(next_first, n_blocks - 1)], -1)
    hbm = pl.BlockSpec(memory_space=pl.ANY)
    return pl.pallas_call(
        functools.partial(_expert_kernel, layer=layer),
        grid_spec=pltpu.PrefetchScalarGridSpec(
            num_scalar_prefetch=4,
            grid=(n_blocks,),
            in_specs=[
                pl.BlockSpec((blk, dp), lambda i, be, nu, sl, nx: (jnp.minimum(i, nu[0] - 1), 0)),
                hbm, hbm, hbm,
            ],
            out_specs=pl.BlockSpec((blk, dp), lambda i, be, nu, sl, nx: (i, 0)),
            scratch_shapes=[pltpu.VMEM((2, d, de), F32), pltpu.VMEM((2, d, de), F32), pltpu.VMEM((2, de, d), F32),
                            pltpu.VMEM((d, de), BF16), pltpu.VMEM((d, de), BF16), pltpu.VMEM((de, d), BF16),
                            pltpu.SemaphoreType.DMA((2, 3))],
        ),
        out_shape=jax.ShapeDtypeStruct((n_slots, dp), jnp.uint32),
        compiler_params=_params("arbitrary"),
        name="moe_experts",
    )(block_expert, n_used, slot, next_expert.astype(jnp.int32), xs, wg, wu, wd)


def _combine_kernel(dest_ref, ys_ref, x_ref, gate_ref, g_ref, b_ref, o_ref, buf_s, sem, *, tm, t_total):
    base = pl.program_id(0) * tm

    def issue(t, carry):
        for k in range(TOP_K):
            d = dest_ref[k * t_total + base + t]
            pltpu.make_async_copy(ys_ref.at[pl.ds(d, 1), :], buf_s.at[k, pl.ds(t, 1), :], sem).start()
        return carry

    lax.fori_loop(0, tm, issue, 0, unroll=DMA_ISSUE_UNROLL)

    def drain(t, carry):
        for k in range(TOP_K):
            pltpu.make_async_copy(ys_ref.at[pl.ds(0, 1), :], buf_s.at[0, pl.ds(0, 1), :], sem).wait()
        return carry

    lax.fori_loop(0, tm, drain, 0, unroll=DMA_ISSUE_UNROLL)
    gate = gate_ref[...]
    lo0, hi0 = _unpack_halves(buf_s[0])
    lo1, hi1 = _unpack_halves(buf_s[1])
    g0, g1 = gate[:, 0:1], gate[:, 1:2]
    y = jnp.concatenate([g0 * lo0 + g1 * lo1, g0 * hi0 + g1 * hi1], -1)
    o_ref[...] = _layer_norm(DN_ALPHA * x_ref[...] + y, g_ref[...], b_ref[...])


def _combine(dest_flat, ys, x, gate_tk, g, bias):
    t, d = x.shape
    tm = _tile(t, 256)
    return pl.pallas_call(
        functools.partial(_combine_kernel, tm=tm, t_total=t),
        grid_spec=pltpu.PrefetchScalarGridSpec(
            num_scalar_prefetch=1,
            grid=(t // tm,),
            in_specs=[
                pl.BlockSpec(memory_space=pl.ANY),
                pl.BlockSpec((tm, d), lambda i, dest: (i, 0)),
                pl.BlockSpec((tm, TOP_K), lambda i, dest: (i, 0)),
                pl.BlockSpec((1, d), lambda i, dest: (0, 0)),
                pl.BlockSpec((1, d), lambda i, dest: (0, 0)),
            ],
            out_specs=pl.BlockSpec((tm, d), lambda i, dest: (i, 0)),
            scratch_shapes=[pltpu.VMEM((TOP_K, tm, d // 2), jnp.uint32), pltpu.SemaphoreType.DMA(())],
        ),
        out_shape=jax.ShapeDtypeStruct((t, d), F32),
        compiler_params=_params("arbitrary"),
        name="moe_combine_ln",
    )(dest_flat, ys, x, gate_tk, g.reshape(1, d), bias.reshape(1, d))


def _moe(x, x_packed, w_router_t, router_bias, wg, wu, wd, layer, g, bias):
    t, d = x.shape
    idx, gate, rank, cnt = _router(x, w_router_t, router_bias)
    blk = MOE_BLOCK
    n_blocks = t * TOP_K // blk + N_EXPERTS
    counts = cnt[:, 0].astype(jnp.int32)
    padded = (counts + blk - 1) // blk * blk
    padded_end = jnp.cumsum(padded)
    padded_start = padded_end - padded
    dest = _slots(idx, rank, padded_start.astype(F32).reshape(N_EXPERTS, 1)).reshape(-1)
    n_used = (padded_end[-1] // blk).astype(jnp.int32)
    blocks = jnp.minimum(jnp.arange(n_blocks, dtype=jnp.int32), n_used - 1)
    block_expert = jnp.minimum(jnp.searchsorted(padded_end, blocks * blk, side='right'),
                               N_EXPERTS - 1).astype(jnp.int32)
    xs = _dispatch(dest, x_packed, n_blocks * blk)
    ys = _experts(block_expert, n_used.reshape(1), xs, wg, wu, wd, layer)
    return _combine(dest, ys, x, gate.T, g, bias)


def _in_proj_weights(w_in):
    sizes = (Q_LORA, KV_LORA, QK_ROPE, GDN_W, GDN_W, GDN_W, GDN_W, GDN_HEADS, GDN_HEADS, D_MODEL, D_MODEL)
    pts = np.cumsum(sizes)[:-1].tolist()
    c_q, c_kv, k_rope, g_q, g_k, g_v, g_z, g_b, g_a, gate_mla, gate_gdn = jnp.split(w_in, pts, axis=-1)
    half = QK_ROPE // 2
    k_rope_sw = jnp.concatenate([k_rope[:, half:], k_rope[:, :half]], -1)
    main = jnp.concatenate([g_q, g_k, g_v, g_z, gate_mla, gate_gdn], -1)
    used = Q_LORA + KV_LORA + 2 * QK_ROPE + 2 * GDN_HEADS
    small = jnp.concatenate([c_q, c_kv, k_rope, k_rope_sw, g_b, g_a,
                             jnp.zeros((w_in.shape[0], SMALL_W - used), w_in.dtype)], -1)
    return main.astype(BF16), small.astype(BF16)


def _mla_weights(w_uq, w_ukv):
    half = QK_ROPE // 2
    wq = w_uq.reshape(Q_LORA, MLA_HEADS, QK_DIM).transpose(1, 0, 2)
    pe = wq[..., QK_NOPE:]
    pe_sw = jnp.concatenate([pe[..., half:], pe[..., :half]], -1)
    wq = jnp.concatenate([wq, pe_sw], -1).astype(BF16)
    wkv = w_ukv.reshape(KV_LORA, MLA_HEADS, QK_NOPE + V_HEAD).transpose(1, 0, 2).astype(BF16)
    return wq, wkv


def _rope_table(positions):
    half = QK_ROPE // 2
    inv_freq = jnp.power(ROPE_BASE, -jnp.arange(half, dtype=F32) / half)
    ang = positions.astype(F32)[..., None] * inv_freq
    cos, sin = jnp.cos(ang), jnp.sin(ang)
    return jnp.concatenate([cos, cos, -sin, sin], -1)


def kernel(x, mem, positions, w_in, mla_q_norm, mla_kv_norm, w_uq, w_ukv, gdn_conv, gdn_a_log, gdn_dt_bias, gdn_o_norm, w_mla_proj, w_gdn_proj, w_mix_out, ln_mix_g, ln_mix_b, w_xq, w_xkv, w_xo, ln_x_g, ln_x_b, w_router, router_bias, w_gate, w_up, w_down, ln_moe_g, ln_moe_b):
    b, s, d = x.shape
    t = b * s
    cs = _rope_table(positions)
    mem2 = mem.reshape(-1, d)
    w_router_t = w_router.T.astype(F32)
    rbias = router_bias.reshape(N_EXPERTS, 1).astype(F32)
    for l in range(w_in.shape[0]):
        w_main, w_small = _in_proj_weights(w_in[l])
        wq, wkv = _mla_weights(w_uq[l], w_ukv[l])
        xt = x.reshape(t, d)
        main = _matmul(xt, w_main, F32, 1024, 1024, "in_proj_main").reshape(b, s, MAIN_W)
        small = _matmul(xt, w_small, F32, 1024, SMALL_W, "in_proj_small").reshape(b, s, SMALL_W)

        q, k, v = _mla_prep(small, cs, mla_q_norm[l].reshape(1, -1), mla_kv_norm[l].reshape(1, -1), wq, wkv)
        o_mla = _attention(q, k, v)

        small_t = small[:, :, SMALL_GATE_COL:SMALL_GATE_COL + 2 * GDN_HEADS].transpose(0, 2, 1)
        o_gdn = _gdn(main, gdn_conv[l], small, small_t, gdn_a_log[l], gdn_dt_bias[l], gdn_o_norm[l])

        merged = _merge(o_mla.reshape(t, d), o_gdn.reshape(t, d), main.reshape(t, MAIN_W),
                        w_mla_proj[l].astype(BF16), w_gdn_proj[l].astype(BF16))
        x1, x1_bf = _proj_ln(merged, w_mix_out[l].astype(BF16), x.reshape(t, d), ln_mix_g[l], ln_mix_b[l])

        kv = _matmul(mem2, w_xkv[l].astype(BF16), BF16, 1024, 1024, "mem_kv").reshape(b, -1, 2 * X_W)
        x2, x2_pk = _xattn(x1_bf.reshape(b, s, d), x1.reshape(b, s, d), w_xq[l].astype(BF16), kv,
                           w_xo[l].astype(BF16), ln_x_g[l], ln_x_b[l])

        x3 = _moe(x2.reshape(t, d), x2_pk.reshape(t, d // 2), w_router_t, rbias, w_gate, w_up, w_down, l,
                  ln_moe_g[l], ln_moe_b[l])
        x = x3.reshape(b, s, d)
    return x
```

```python
import functools
import math

import jax
import jax.numpy as jnp
import numpy as np
from jax import lax
from jax.experimental import pallas as pl
from jax.experimental.pallas import tpu as pltpu

D_MODEL = 2048
DEPTH = 2
MLA_HEADS = 16
Q_LORA = 512
KV_LORA = 512
QK_NOPE = 128
QK_ROPE = 64
V_HEAD = 128
QK_DIM = QK_NOPE + QK_ROPE
ROPE_BASE = 10000.0
GDN_HEADS = 16
GDN_DK = 128
GDN_DV = 128
CONV_K = 4
CHUNK = 64
GDN_W = GDN_HEADS * GDN_DK
X_HEADS = 4
X_HEAD_DIM = 128
X_W = X_HEADS * X_HEAD_DIM
N_EXPERTS = 64
N_GROUPS = 8
EXPERTS_PER_GROUP = N_EXPERTS // N_GROUPS
TOP_K = 2
D_EXPERT = 512
Q_PRESCALE = QK_DIM ** -0.5 * math.log2(math.e)
DN_ALPHA = (2.0 * DEPTH) ** 0.25
LN_EPS = 1e-5
RMS_EPS = 1e-6

MAIN_W = 3 * GDN_W + GDN_W + 2 * D_MODEL
SMALL_W = 1280
SMALL_ROPE_BLOCK = (Q_LORA + KV_LORA) // 128
SMALL_GATE_COL = Q_LORA + KV_LORA + 2 * QK_ROPE
SMALL_GATE_BLOCK = SMALL_GATE_COL // 128

GDN_HEAD_BLOCK = 8
GDN_ROW_BLOCK = 128
MOE_BLOCK = 256
DMA_ISSUE_UNROLL = 8
VMEM_LIMIT = 56 * 1024 * 1024

BF16 = jnp.bfloat16
F32 = jnp.float32


def _tile(n, pref):
    return pref if n % pref == 0 else n


def _params(*sem):
    return pltpu.CompilerParams(dimension_semantics=sem, vmem_limit_bytes=VMEM_LIMIT)


def _sigmoid(x):
    return 0.5 * jnp.tanh(0.5 * x) + 0.5


def _pack_halves(y):
    n = y.shape[1] // 2
    lo = pltpu.bitcast(y[:, :n].astype(BF16).astype(F32), jnp.uint32)
    hi = pltpu.bitcast(y[:, n:].astype(BF16).astype(F32), jnp.uint32)
    return (lo >> 16) | (hi & jnp.uint32(0xFFFF0000))


def _unpack_halves(p):
    lo = pltpu.bitcast(p << 16, F32)
    hi = pltpu.bitcast(p & jnp.uint32(0xFFFF0000), F32)
    return lo, hi


def _softplus(x):
    return jnp.maximum(x, 0.0) + jnp.log1p(jnp.exp(-jnp.abs(x)))


def _layer_norm(y, g, b):
    mu = jnp.mean(y, -1, keepdims=True)
    d = y - mu
    var = jnp.mean(d * d, -1, keepdims=True)
    return d * lax.rsqrt(var + LN_EPS) * g + b


def _dot(a, b):
    return jnp.dot(a, b, preferred_element_type=F32)


def _dot_nt(a, b):
    return lax.dot_general(a, b, (((1,), (1,)), ((), ())), preferred_element_type=F32)


def _dot_tn(a, b):
    return lax.dot_general(a, b, (((0,), (0,)), ((), ())), preferred_element_type=F32)


def _mm_kernel(a_ref, w_ref, o_ref, a_s):
    @pl.when(pl.program_id(1) == 0)
    def _():
        a_s[...] = a_ref[...].astype(BF16)

    o_ref[...] = _dot(a_s[...], w_ref[...]).astype(o_ref.dtype)


def _matmul(a, w, out_dtype, tm, tn, name):
    m, k = a.shape
    n = w.shape[1]
    tm, tn = _tile(m, tm), _tile(n, tn)
    return pl.pallas_call(
        _mm_kernel,
        grid=(m // tm, n // tn),
        in_specs=[pl.BlockSpec((tm, k), lambda i, j: (i, 0)),
                  pl.BlockSpec((k, tn), lambda i, j: (0, j))],
        out_specs=pl.BlockSpec((tm, tn), lambda i, j: (i, j)),
        out_shape=jax.ShapeDtypeStruct((m, n), out_dtype),
        scratch_shapes=[pltpu.VMEM((tm, k), BF16)],
        compiler_params=_params("parallel", "arbitrary"),
        name=name,
    )(a, w)


def _mla_prep_kernel(cq_ref, ckv_ref, kr_ref, cs_ref, qn_ref, kvn_ref, wq_ref, wkv_ref, q_ref, k_ref, v_ref):
    ts = cq_ref.shape[1]
    cq = cq_ref[0]
    cqn = (cq * lax.rsqrt(jnp.mean(cq * cq, -1, keepdims=True) + RMS_EPS) * qn_ref[...]).astype(BF16)
    ckv = ckv_ref[0]
    ckvn = (ckv * lax.rsqrt(jnp.mean(ckv * ckv, -1, keepdims=True) + RMS_EPS) * kvn_ref[...]).astype(BF16)
    cs = cs_ref[0]
    lane = lax.broadcasted_iota(jnp.int32, (ts, 128), 1)

    def rotate(t):
        u = t * cs
        return jnp.where(lane < QK_ROPE, u + pltpu.roll(u, QK_ROPE, 1), 0.0).astype(BF16)

    kpe = rotate(kr_ref[0])
    ones = jnp.ones((ts, V_HEAD), BF16)
    for h in range(MLA_HEADS):
        rq = _dot(cqn, wq_ref[h]) * Q_PRESCALE
        q_ref[0, h, :, :QK_NOPE] = rq[:, :QK_NOPE].astype(BF16)
        q_ref[0, h, :, QK_NOPE:] = rotate(rq[:, QK_NOPE:])
        rkv = _dot(ckvn, wkv_ref[h])
        k_ref[0, h, :, :QK_NOPE] = rkv[:, :QK_NOPE].astype(BF16)
        k_ref[0, h, :, QK_NOPE:] = kpe
        v_ref[0, h, :, :V_HEAD] = rkv[:, QK_NOPE:].astype(BF16)
        v_ref[0, h, :, V_HEAD:] = ones


def _mla_prep(small, cs, q_norm, kv_norm, wq, wkv):
    b, s, _ = small.shape
    ts = _tile(s, 512)
    hq = MLA_HEADS
    out_spec = pl.BlockSpec((1, hq, ts, 256), lambda bi, si: (bi, 0, si, 0))
    out_shape = jax.ShapeDtypeStruct((b, hq, s, 256), BF16)
    return pl.pallas_call(
        _mla_prep_kernel,
        grid=(b, s // ts),
        in_specs=[
            pl.BlockSpec((1, ts, Q_LORA), lambda bi, si: (bi, si, 0)),
            pl.BlockSpec((1, ts, KV_LORA), lambda bi, si: (bi, si, 1)),
            pl.BlockSpec((1, ts, 128), lambda bi, si: (bi, si, SMALL_ROPE_BLOCK)),
            pl.BlockSpec((1, ts, 128), lambda bi, si: (bi, si, 0)),
            pl.BlockSpec((1, Q_LORA), lambda bi, si: (0, 0)),
            pl.BlockSpec((1, KV_LORA), lambda bi, si: (0, 0)),
            pl.BlockSpec((hq, Q_LORA, 256), lambda bi, si: (0, 0, 0)),
            pl.BlockSpec((hq, KV_LORA, 256), lambda bi, si: (0, 0, 0)),
        ],
        out_specs=[out_spec, out_spec, out_spec],
        out_shape=[out_shape, out_shape, out_shape],
        compiler_params=_params("parallel", "parallel"),
        name="mla_prep",
    )(small, small, small, cs, q_norm, kv_norm, wq, wkv)


def _attn_kernel(q_ref, k_ref, v_ref, o_ref, sa_s, sb_s, m_s, acc_s, *, tq, rc):
    i = pl.program_id(2)
    dv2 = acc_s.shape[-1]
    m_s[...] = jnp.full(m_s.shape, -jnp.inf, F32)
    acc_s[...] = jnp.zeros(acc_s.shape, F32)

    def scores(j):
        start = pl.multiple_of(j * tq, tq)
        return _dot_nt(q_ref[0, 0], k_ref[0, 0, pl.ds(start, tq), :])

    def softmax_pv(s_ref, c, j, nk, masked):
        rows = slice(c * rc, (c + 1) * rc)
        start = pl.multiple_of(j * tq, tq)
        s = s_ref[rows, :nk]
        if masked:
            row = lax.broadcasted_iota(jnp.int32, (rc, nk), 0) + c * rc
            col = lax.broadcasted_iota(jnp.int32, (rc, nk), 1)
            s = jnp.where(col <= row, s, -jnp.inf)
        m_prev = m_s[rows, :]
        m_new = jnp.maximum(m_prev, jnp.max(s, -1, keepdims=True))
        alpha = jnp.exp2(m_prev - m_new)
        p = jnp.exp2(s - jnp.concatenate([m_new] * (nk // 128), -1))
        acc_s[rows, :] = (jnp.concatenate([alpha] * (dv2 // 128), -1) * acc_s[rows, :]
                          + _dot(p.astype(BF16), v_ref[0, 0, pl.ds(start, nk), :]))
        m_s[rows, :] = m_new

    nch = tq // rc

    def step(cur_ref, nxt_ref, j):
        nxt_ref[...] = scores(j + 1)
        for c in range(nch):
            softmax_pv(cur_ref, c, j, tq, False)

    def diagonal(cur_ref):
        for c in range(nch):
            softmax_pv(cur_ref, c, i, (c + 1) * rc, True)
        dv = dv2 // 2
        o_ref[0] = (acc_s[:, :dv] / acc_s[:, dv:]).astype(o_ref.dtype)

    sa_s[...] = scores(0)

    def pair(jj, carry):
        step(sa_s, sb_s, 2 * jj)
        step(sb_s, sa_s, 2 * jj + 1)
        return carry

    lax.fori_loop(0, i // 2, pair, 0)

    @pl.when(i % 2 == 0)
    def _():
        diagonal(sa_s)

    @pl.when(i % 2 == 1)
    def _():
        step(sa_s, sb_s, i - 1)
        diagonal(sb_s)


def _attention(q, k, v):
    b, h, s, dqk = q.shape
    dv2 = v.shape[-1]
    tq = _tile(s, 1024)
    rc = _tile(tq, 256)
    kern = functools.partial(_attn_kernel, tq=tq, rc=rc)
    return pl.pallas_call(
        kern,
        grid=(b, h, s // tq),
        in_specs=[
            pl.BlockSpec((1, 1, tq, dqk), lambda bi, hi, i: (bi, hi, i, 0)),
            pl.BlockSpec((1, 1, s, dqk), lambda bi, hi, i: (bi, hi, 0, 0)),
            pl.BlockSpec((1, 1, s, dv2), lambda bi, hi, i: (bi, hi, 0, 0)),
        ],
        out_specs=pl.BlockSpec((1, tq, dv2 // 2), lambda bi, hi, i: (bi, i, hi)),
        out_shape=jax.ShapeDtypeStruct((b, s, h * dv2 // 2), BF16),
        scratch_shapes=[pltpu.VMEM((tq, tq), F32), pltpu.VMEM((tq, tq), F32), pltpu.VMEM((tq, 128), F32),
                        pltpu.VMEM((tq, dv2), F32)],
        compiler_params=_params("parallel", "parallel", "arbitrary"),
        name="mla_attention",
    )(q, k, v)


def _gdn_kernel(qr_ref, kr_ref, vr_ref, wq_ref, wk_ref, wv_ref, z_ref, gate_ref, gatet_ref, arow_ref, dtrow_ref,
                acol_ref, dtcol_ref, onorm_ref, o_ref, state_s, xx_s, qkv_s, *, hb, rb):
    hg = pl.program_id(1)
    r = pl.program_id(2)

    @pl.when(r == 0)
    def _():
        state_s[...] = jnp.zeros(state_s.shape, F32)
        xx_s[:, 0:8, :] = jnp.zeros((3, 8, hb * 128), F32)

    @pl.when(r > 0)
    def _():
        xx_s[:, 0:8, :] = xx_s[:, rb:rb + 8, :]

    for ti, (x_ref, w_ref) in enumerate(((qr_ref, wq_ref), (kr_ref, wk_ref), (vr_ref, wv_ref))):
        xx_s[ti, 8:rb + 8, :] = x_ref[0]
        wc = w_ref[...]
        y = wc[CONV_K - 1:CONV_K, :] * xx_s[ti, 8:rb + 8, :]
        for j in range(CONV_K - 1):
            off = 8 - (CONV_K - 1) + j
            y = y + wc[j:j + 1, :] * xx_s[ti, off:off + rb, :]
        y = y * _sigmoid(y)
        if ti == 2:
            qkv_s[ti] = y
        else:
            for h in range(hb):
                yh = y[:, h * 128:(h + 1) * 128]
                yn = yh * lax.rsqrt(jnp.sum(yh * yh, -1, keepdims=True) + RMS_EPS)
                qkv_s[ti, :, h * 128:(h + 1) * 128] = yn * (GDN_DK ** -0.5) if ti == 0 else yn

    def head_rows(ti, c, h):
        return qkv_s[ti, c * CHUNK:(c + 1) * CHUNK, h * 128:(h + 1) * 128]

    blk = gate_ref[0]
    beta_tm = _sigmoid(blk)
    g_tm = -jnp.exp(arow_ref[...]) * _softplus(blk + dtrow_ref[...])
    g_t = -jnp.exp(acol_ref[...]) * _softplus(gatet_ref[0] + dtcol_ref[...])

    ri = lax.broadcasted_iota(jnp.int32, (CHUNK, CHUNK), 0)
    ci = lax.broadcasted_iota(jnp.int32, (CHUNK, CHUNK), 1)
    causal = ri >= ci
    strict = ri > ci
    tri = jnp.where(causal, 1.0, 0.0).astype(F32)
    tri_t = jnp.where(ri <= ci, 1.0, 0.0).astype(F32)
    eye = jnp.where(ri == ci, 1.0, 0.0).astype(F32)
    onorm = onorm_ref[...]

    nc = rb // CHUNK
    items = [(c, h) for c in range(nc) for h in range(hb)]
    rows = [slice(c * CHUNK, (c + 1) * CHUNK) for c in range(nc)]
    gc_tm = [jnp.dot(tri, g_tm[rows[c], :], precision=lax.Precision.HIGHEST, preferred_element_type=F32)
             for c in range(nc)]
    gc_t = [jnp.dot(g_t[:, rows[c]], tri_t, precision=lax.Precision.HIGHEST, preferred_element_type=F32)
            for c in range(nc)]
    sel_l = lax.broadcasted_iota(jnp.int32, (CHUNK, 128), 1)
    sel_s = lax.broadcasted_iota(jnp.int32, (32, CHUNK), 0)
    gcol, bcol, grow, glast, egc, kt, kb = {}, {}, {}, {}, {}, {}, {}
    for it in items:
        c, h = it
        lane = GDN_HEADS + hg * hb + h
        gcol[it] = jnp.sum(jnp.where(sel_l == lane, gc_tm[c], 0.0), -1, keepdims=True)
        bcol[it] = jnp.sum(jnp.where(sel_l == hg * hb + h, beta_tm[rows[c], :], 0.0), -1, keepdims=True)
        grow[it] = jnp.sum(jnp.where(sel_s == lane, gc_t[c], 0.0), 0, keepdims=True)
        glast[it] = gcol[it][CHUNK - 1:CHUNK, :]
        egc[it] = jnp.exp(gcol[it])
        k = head_rows(1, c, h)
        kt[it] = k.T
        kb[it] = k * bcol[it]
    kq = {it: _dot(jnp.concatenate([kb[it], head_rows(0, *it)], 0).astype(BF16),
                   kt[it].astype(BF16)) for it in items}
    amat, m, inv = {}, {}, {}
    for it in items:
        decay = jnp.exp(jnp.where(causal, gcol[it] - grow[it], -jnp.inf))
        amat[it] = (kq[it][CHUNK:] * decay).astype(BF16)
        m[it] = -jnp.where(strict, kq[it][:CHUNK] * decay, 0.0)
        inv[it] = eye + m[it]
    for it in items:
        mb = m[it].astype(BF16)
        m[it] = _dot(mb, mb)
    n_fac = int(round(math.log2(CHUNK))) - 1
    for t in range(n_fac):
        for it in items:
            mb = m[it].astype(BF16)
            if t < n_fac - 1:
                r2 = _dot(jnp.concatenate([inv[it], m[it]], 0).astype(BF16), mb)
                inv[it] = inv[it] + r2[:CHUNK]
                m[it] = r2[CHUNK:]
            else:
                inv[it] = inv[it] + _dot(inv[it].astype(BF16), mb)
    uw = {}
    for it in items:
        c, h = it
        vb = head_rows(2, c, h) * bcol[it]
        uw[it] = _dot(inv[it].astype(BF16), jnp.concatenate([vb, kb[it] * egc[it]], 1).astype(BF16))
    for c in range(nc):
        its = [(c, h) for h in range(hb)]
        st = {it: state_s[it[1]] for it in its}
        wq = {it: _dot(jnp.concatenate([uw[it][:, GDN_DV:], head_rows(0, *it) * egc[it]], 0).astype(BF16),
                       st[it].astype(BF16)) for it in its}
        vnb = {it: (uw[it][:, :GDN_DV] - wq[it][:CHUNK]).astype(BF16) for it in its}
        o = {it: wq[it][CHUNK:] + _dot(amat[it], vnb[it]) for it in its}
        for it in its:
            h = it[1]
            kdt = kt[it] * jnp.exp(glast[it] - grow[it])
            state_s[h] = st[it] * jnp.exp(glast[it]) + _dot(kdt.astype(BF16), vnb[it])
        for it in its:
            h = it[1]
            on = o[it] * lax.rsqrt(jnp.mean(o[it] * o[it], -1, keepdims=True) + RMS_EPS) * onorm
            zz = z_ref[0, rows[c], h * GDN_DV:(h + 1) * GDN_DV]
            o_ref[0, rows[c], h * GDN_DV:(h + 1) * GDN_DV] = (on * (zz * _sigmoid(zz))).astype(o_ref.dtype)


def _gdn(main, w_conv, small, small_t, a_log, dt_bias, o_norm):
    b, s, _ = main.shape
    hb = GDN_HEAD_BLOCK
    rb = _tile(s, GDN_ROW_BLOCK)
    ng = GDN_HEADS // hb
    zoff = 3 * GDN_W // (hb * 128)
    pad = lambda a, lo: jnp.zeros((128,), F32).at[lo:lo + GDN_HEADS].set(a.astype(F32))
    arow = pad(a_log, GDN_HEADS).reshape(1, 128)
    dtrow = pad(dt_bias, GDN_HEADS).reshape(1, 128)
    acol = arow[0, :32].reshape(32, 1)
    dtcol = dtrow[0, :32].reshape(32, 1)
    kern = functools.partial(_gdn_kernel, hb=hb, rb=rb)
    return pl.pallas_call(
        kern,
        grid=(b, ng, s // rb),
        in_specs=[
            pl.BlockSpec((1, rb, hb * 128), lambda bi, g, r: (bi, r, g)),
            pl.BlockSpec((1, rb, hb * 128), lambda bi, g, r: (bi, r, ng + g)),
            pl.BlockSpec((1, rb, hb * 128), lambda bi, g, r: (bi, r, 2 * ng + g)),
            pl.BlockSpec((CONV_K, hb * 128), lambda bi, g, r: (0, g)),
            pl.BlockSpec((CONV_K, hb * 128), lambda bi, g, r: (0, ng + g)),
            pl.BlockSpec((CONV_K, hb * 128), lambda bi, g, r: (0, 2 * ng + g)),
            pl.BlockSpec((1, rb, hb * 128), lambda bi, g, r: (bi, r, zoff + g)),
            pl.BlockSpec((1, rb, 128), lambda bi, g, r: (bi, r, SMALL_GATE_BLOCK)),
            pl.BlockSpec((1, 32, rb), lambda bi, g, r: (bi, 0, r)),
            pl.BlockSpec((1, 128), lambda bi, g, r: (0, 0)),
            pl.BlockSpec((1, 128), lambda bi, g, r: (0, 0)),
            pl.BlockSpec((32, 1), lambda bi, g, r: (0, 0)),
            pl.BlockSpec((32, 1), lambda bi, g, r: (0, 0)),
            pl.BlockSpec((1, 128), lambda bi, g, r: (0, 0)),
        ],
        out_specs=pl.BlockSpec((1, rb, hb * 128), lambda bi, g, r: (bi, r, g)),
        out_shape=jax.ShapeDtypeStruct((b, s, GDN_W), BF16),
        scratch_shapes=[pltpu.VMEM((hb, GDN_DK, GDN_DV), F32), pltpu.VMEM((3, rb + 8, hb * 128), F32),
                        pltpu.VMEM((3, rb, hb * 128), F32)],
        compiler_params=_params("parallel", "parallel", "arbitrary"),
        name="gdn_delta_rule",
    )(main, main, main, w_conv, w_conv, w_conv, main, small, small_t, arow, dtrow, acol, dtcol,
      o_norm.reshape(1, GDN_DV).astype(F32))


def _merge_kernel(om_ref, og_ref, gm_ref, gg_ref, wm_ref, wg_ref, o_ref):
    ym = _dot(om_ref[...], wm_ref[...])
    yg = _dot(og_ref[...], wg_ref[...])
    o_ref[...] = (_sigmoid(gm_ref[...]) * ym + _sigmoid(gg_ref[...]) * yg).astype(o_ref.dtype)


def _merge(o_mla, o_gdn, main, w_mla, w_gdn):
    t, d = o_mla.shape
    tm, tn = _tile(t, 512), _tile(d, 1024)
    goff = (4 * GDN_W) // tn
    nn = d // tn
    return pl.pallas_call(
        _merge_kernel,
        grid=(nn, t // tm),
        in_specs=[
            pl.BlockSpec((tm, d), lambda j, i: (i, 0)),
            pl.BlockSpec((tm, d), lambda j, i: (i, 0)),
            pl.BlockSpec((tm, tn), lambda j, i: (i, goff + j)),
            pl.BlockSpec((tm, tn), lambda j, i: (i, goff + nn + j)),
            pl.BlockSpec((d, tn), lambda j, i: (0, j)),
            pl.BlockSpec((d, tn), lambda j, i: (0, j)),
        ],
        out_specs=pl.BlockSpec((tm, tn), lambda j, i: (i, j)),
        out_shape=jax.ShapeDtypeStruct((t, d), BF16),
        compiler_params=_params("parallel", "parallel"),
        name="mixer_merge",
    )(o_mla, o_gdn, main, main, w_mla, w_gdn)


def _proj_ln_kernel(a_ref, w_ref, x_ref, g_ref, b_ref, o_ref, obf_ref):
    y = DN_ALPHA * x_ref[...] + _dot(a_ref[...], w_ref[...])
    out = _layer_norm(y, g_ref[...], b_ref[...])
    o_ref[...] = out
    obf_ref[...] = out.astype(BF16)


def _proj_ln(a, w, x, g, bias):
    t, k = a.shape
    d = w.shape[1]
    tm = _tile(t, 512)
    return pl.pallas_call(
        _proj_ln_kernel,
        grid=(t // tm,),
        in_specs=[
            pl.BlockSpec((tm, k), lambda i: (i, 0)),
            pl.BlockSpec((k, d), lambda i: (0, 0)),
            pl.BlockSpec((tm, d), lambda i: (i, 0)),
            pl.BlockSpec((1, d), lambda i: (0, 0)),
            pl.BlockSpec((1, d), lambda i: (0, 0)),
        ],
        out_specs=[pl.BlockSpec((tm, d), lambda i: (i, 0)), pl.BlockSpec((tm, d), lambda i: (i, 0))],
        out_shape=[jax.ShapeDtypeStruct((t, d), F32), jax.ShapeDtypeStruct((t, d), BF16)],
        compiler_params=_params("parallel"),
        name="proj_residual_ln",
    )(a, w, x, g.reshape(1, d), bias.reshape(1, d))


def _xattn_kernel(xbf_ref, x_ref, wq_ref, kv_ref, wo_ref, g_ref, b_ref, o_ref, opk_ref):
    q = _dot(xbf_ref[0], wq_ref[...])
    scale = X_HEAD_DIM ** -0.5
    outs = []
    for h in range(X_HEADS):
        qh = q[:, h * X_HEAD_DIM:(h + 1) * X_HEAD_DIM].astype(BF16)
        kh = kv_ref[0, :, h * X_HEAD_DIM:(h + 1) * X_HEAD_DIM]
        vh = kv_ref[0, :, X_W + h * X_HEAD_DIM:X_W + (h + 1) * X_HEAD_DIM]
        s = _dot_nt(qh, kh) * scale
        e = jnp.exp(s - jnp.max(s, -1, keepdims=True))
        p = e / jnp.sum(e, -1, keepdims=True)
        outs.append(_dot(p.astype(BF16), vh))
    o = jnp.concatenate(outs, -1).astype(BF16)
    y = DN_ALPHA * x_ref[0] + _dot(o, wo_ref[...])
    out = _layer_norm(y, g_ref[...], b_ref[...])
    o_ref[0] = out
    opk_ref[0] = _pack_halves(out)


def _xattn(x_bf, x, wq, kv, wo, g, bias):
    b, s, d = x.shape
    tm = _tile(s, 512)
    ml = kv.shape[1]
    return pl.pallas_call(
        _xattn_kernel,
        grid=(b, s // tm),
        in_specs=[
            pl.BlockSpec((1, tm, d), lambda bi, i: (bi, i, 0)),
            pl.BlockSpec((1, tm, d), lambda bi, i: (bi, i, 0)),
            pl.BlockSpec((d, X_W), lambda bi, i: (0, 0)),
            pl.BlockSpec((1, ml, 2 * X_W), lambda bi, i: (bi, 0, 0)),
            pl.BlockSpec((X_W, d), lambda bi, i: (0, 0)),
            pl.BlockSpec((1, d), lambda bi, i: (0, 0)),
            pl.BlockSpec((1, d), lambda bi, i: (0, 0)),
        ],
        out_specs=[pl.BlockSpec((1, tm, d), lambda bi, i: (bi, i, 0)),
                   pl.BlockSpec((1, tm, d // 2), lambda bi, i: (bi, i, 0))],
        out_shape=[jax.ShapeDtypeStruct((b, s, d), F32), jax.ShapeDtypeStruct((b, s, d // 2), jnp.uint32)],
        compiler_params=_params("parallel", "parallel"),
        name="memory_cross_attention",
    )(x_bf, x, wq, kv, wo, g.reshape(1, d), bias.reshape(1, d))


def _first_argmax(vals, iota, n):
    m = jnp.max(vals, 0, keepdims=True)
    idx = jnp.min(jnp.where(vals == m, iota, n), 0, keepdims=True)
    return m, idx


def _router_kernel(x_ref, wr_ref, bias_ref, idx_ref, gate_ref, rank_ref, cnt_ref, carry_s, *, tm):
    i = pl.program_id(0)

    @pl.when(i == 0)
    def _():
        carry_s[...] = jnp.zeros(carry_s.shape, F32)

    logits = _dot_nt(wr_ref[...].astype(BF16), x_ref[...].astype(BF16))
    scores = 1.0 / (1.0 + jnp.exp(-logits))
    sel = scores + bias_ref[...]
    epg = EXPERTS_PER_GROUP
    io8 = lax.broadcasted_iota(jnp.int32, (epg, tm), 0)
    best_score = None
    for g in range(N_GROUPS):
        sg = sel[g * epg:(g + 1) * epg, :]
        m1, i1 = _first_argmax(sg, io8, epg)
        m2 = jnp.max(jnp.where(io8 == i1, -jnp.inf, sg), 0, keepdims=True)
        gs = m1 + m2
        if g == 0:
            best_score, best = gs, jnp.zeros((1, tm), jnp.int32)
            in_sel, in_sc = sg, scores[0:epg, :]
        else:
            better = gs > best_score
            best_score = jnp.where(better, gs, best_score)
            best = jnp.where(better, g, best)
            in_sel = jnp.where(better, sg, in_sel)
            in_sc = jnp.where(better, scores[g * epg:(g + 1) * epg, :], in_sc)
    _, l1 = _first_argmax(in_sel, io8, epg)
    _, l2 = _first_argmax(jnp.where(io8 == l1, -jnp.inf, in_sel), io8, epg)
    g1 = jnp.sum(jnp.where(io8 == l1, in_sc, 0.0), 0, keepdims=True)
    g2 = jnp.sum(jnp.where(io8 == l2, in_sc, 0.0), 0, keepdims=True)
    e1 = best * epg + l1
    e2 = best * epg + l2
    idx_ref[0:1, :] = e1
    idx_ref[1:2, :] = e2
    gsum = g1 + g2
    gate_ref[0:1, :] = g1 / gsum
    gate_ref[1:2, :] = g2 / gsum

    ioe = lax.broadcasted_iota(jnp.int32, (N_EXPERTS, tm), 0)
    oh1 = ioe == e1
    oh2 = ioe == e2
    onehot = jnp.where(oh1, 1.0, 0.0) + jnp.where(oh2, 1.0, 0.0)
    rr = lax.broadcasted_iota(jnp.int32, (tm, tm), 0)
    cc = lax.broadcasted_iota(jnp.int32, (tm, tm), 1)
    upper = jnp.where(rr < cc, 1.0, 0.0).astype(BF16)
    tot = carry_s[...] + _dot(onehot.astype(BF16), upper)
    rank_ref[0:1, :] = jnp.sum(jnp.where(oh1, tot, 0.0), 0, keepdims=True).astype(jnp.int32)
    rank_ref[1:2, :] = jnp.sum(jnp.where(oh2, tot, 0.0), 0, keepdims=True).astype(jnp.int32)
    new = carry_s[...] + jnp.sum(onehot, -1, keepdims=True)
    carry_s[...] = new
    cnt_ref[...] = jnp.broadcast_to(new, cnt_ref.shape)


def _router(x, w_router_t, bias):
    t, d = x.shape
    tm = _tile(t, 512)
    return pl.pallas_call(
        functools.partial(_router_kernel, tm=tm),
        grid=(t // tm,),
        in_specs=[pl.BlockSpec((tm, d), lambda i: (i, 0)),
                  pl.BlockSpec((N_EXPERTS, d), lambda i: (0, 0)),
                  pl.BlockSpec((N_EXPERTS, 1), lambda i: (0, 0))],
        out_specs=[pl.BlockSpec((TOP_K, tm), lambda i: (0, i)),
                   pl.BlockSpec((TOP_K, tm), lambda i: (0, i)),
                   pl.BlockSpec((TOP_K, tm), lambda i: (0, i)),
                   pl.BlockSpec((N_EXPERTS, 128), lambda i: (0, 0))],
        out_shape=[jax.ShapeDtypeStruct((TOP_K, t), jnp.int32),
                   jax.ShapeDtypeStruct((TOP_K, t), F32),
                   jax.ShapeDtypeStruct((TOP_K, t), jnp.int32),
                   jax.ShapeDtypeStruct((N_EXPERTS, 128), F32)],
        scratch_shapes=[pltpu.VMEM((N_EXPERTS, 1), F32)],
        compiler_params=_params("arbitrary"),
        name="moe_router",
    )(x, w_router_t, bias)


def _slot_kernel(idx_ref, rank_ref, pstart_ref, dest_ref, *, tm):
    ioe = lax.broadcasted_iota(jnp.int32, (N_EXPERTS, tm), 0)
    pstart = pstart_ref[...]
    for k in range(TOP_K):
        start = jnp.sum(jnp.where(ioe == idx_ref[k:k + 1, :], pstart, 0.0), 0, keepdims=True)
        dest_ref[k:k + 1, :] = start.astype(jnp.int32) + rank_ref[k:k + 1, :]


def _slots(idx, rank, pstart):
    t = idx.shape[1]
    tm = _tile(t, 2048)
    spec = pl.BlockSpec((TOP_K, tm), lambda i: (0, i))
    return pl.pallas_call(
        functools.partial(_slot_kernel, tm=tm),
        grid=(t // tm,),
        in_specs=[spec, spec, pl.BlockSpec((N_EXPERTS, 1), lambda i: (0, 0))],
        out_specs=spec,
        out_shape=jax.ShapeDtypeStruct((TOP_K, t), jnp.int32),
        compiler_params=_params("parallel"),
        name="moe_slots",
    )(idx, rank, pstart)


def _dispatch_kernel(dest_ref, x_ref, xs_in_ref, xs_ref, sem, *, tm, t_total):
    del xs_in_ref
    base = pl.program_id(0) * tm

    def issue(t, carry):
        for k in range(TOP_K):
            d = dest_ref[k * t_total + base + t]
            pltpu.make_async_copy(x_ref.at[pl.ds(t, 1), :], xs_ref.at[pl.ds(d, 1), :], sem).start()
        return carry

    lax.fori_loop(0, tm, issue, 0, unroll=DMA_ISSUE_UNROLL)

    def drain(t, carry):
        for k in range(TOP_K):
            pltpu.make_async_copy(x_ref.at[pl.ds(0, 1), :], xs_ref.at[pl.ds(0, 1), :], sem).wait()
        return carry

    lax.fori_loop(0, tm, drain, 0, unroll=DMA_ISSUE_UNROLL)


def _dispatch(dest_flat, x, n_slots):
    t, d = x.shape
    tm = _tile(t, 512)
    xs0 = jnp.zeros((n_slots, d), x.dtype)
    return pl.pallas_call(
        functools.partial(_dispatch_kernel, tm=tm, t_total=t),
        grid_spec=pltpu.PrefetchScalarGridSpec(
            num_scalar_prefetch=1,
            grid=(t // tm,),
            in_specs=[pl.BlockSpec((tm, d), lambda i, dest: (i, 0)),
                      pl.BlockSpec(memory_space=pl.ANY)],
            out_specs=pl.BlockSpec(memory_space=pl.ANY),
            scratch_shapes=[pltpu.SemaphoreType.DMA(())],
        ),
        out_shape=jax.ShapeDtypeStruct((n_slots, d), x.dtype),
        input_output_aliases={2: 0},
        compiler_params=_params("arbitrary"),
        name="moe_dispatch",
    )(dest_flat, x, xs0)


def _expert_kernel(be_ref, nu_ref, slot_ref, nx_ref, xs_ref, wg_hbm, wu_hbm, wd_hbm, ys_ref,
                   fg_s, fu_s, fd_s, wg_s, wu_s, wd_s, sems, *, layer):
    i = pl.program_id(0)
    e = be_ref[i]

    def weight_copies(expert, slot):
        return (pltpu.make_async_copy(wg_hbm.at[layer, expert], fg_s.at[slot], sems.at[slot, 0]),
                pltpu.make_async_copy(wu_hbm.at[layer, expert], fu_s.at[slot], sems.at[slot, 1]),
                pltpu.make_async_copy(wd_hbm.at[layer, expert], fd_s.at[slot], sems.at[slot, 2]))

    @pl.when(i == 0)
    def _():
        for cp in weight_copies(e, 0):
            cp.start()

    @pl.when((i == 0) | (e != be_ref[jnp.maximum(i - 1, 0)]))
    def _():
        slot = slot_ref[i]
        for cp in weight_copies(e, slot):
            cp.wait()
        wg_s[...] = fg_s[slot].astype(BF16)
        wu_s[...] = fu_s[slot].astype(BF16)
        wd_s[...] = fd_s[slot].astype(BF16)

        @pl.when(nx_ref[i] >= 0)
        def _():
            for cp in weight_copies(nx_ref[i], 1 - slot):
                cp.start()

    @pl.when(i < nu_ref[0])
    def _():
        lo, hi = _unpack_halves(xs_ref[...])
        xb = jnp.concatenate([lo.astype(BF16), hi.astype(BF16)], -1)
        hg = _dot(xb, wg_s[...])
        hu = _dot(xb, wu_s[...])
        h = (hg * _sigmoid(hg)) * hu
        ys_ref[...] = _pack_halves(_dot(h.astype(BF16), wd_s[...]))

    @pl.when(i >= nu_ref[0])
    def _():
        ys_ref[...] = jnp.zeros(ys_ref.shape, ys_ref.dtype)


def _experts(block_expert, n_used, xs, wg, wu, wd, layer):
    n_slots, dp = xs.shape
    d = 2 * dp
    blk = MOE_BLOCK
    n_blocks = n_slots // blk
    de = wg.shape[-1]
    pos = jnp.arange(n_blocks, dtype=jnp.int32)
    first = jnp.concatenate([jnp.ones((1,), bool), block_expert[1:] != block_expert[:-1]])
    slot = ((jnp.cumsum(first.astype(jnp.int32)) - 1) % 2).astype(jnp.int32)
    first_pos = jnp.where(first, pos, n_blocks)
    next_first = lax.cummin(jnp.concatenate([first_pos[1:], jnp.full((1,), n_blocks, jnp.int32)]), reverse=True)
    next_expert = jnp.where(next_first < n_blocks, block_expert[jnp.minimum(next_first, n_blocks - 1)], -1)
    hbm = pl.BlockSpec(memory_space=pl.ANY)
    return pl.pallas_call(
        functools.partial(_expert_kernel, layer=layer),
        grid_spec=pltpu.PrefetchScalarGridSpec(
            num_scalar_prefetch=4,
            grid=(n_blocks,),
            in_specs=[
                pl.BlockSpec((blk, dp), lambda i, be, nu, sl, nx: (jnp.minimum(i, nu[0] - 1), 0)),
                hbm, hbm, hbm,
            ],
            out_specs=pl.BlockSpec((blk, dp), lambda i, be, nu, sl, nx: (i, 0)),
            scratch_shapes=[pltpu.VMEM((2, d, de), F32), pltpu.VMEM((2, d, de), F32), pltpu.VMEM((2, de, d), F32),
                            pltpu.VMEM((d, de), BF16), pltpu.VMEM((d, de), BF16), pltpu.VMEM((de, d), BF16),
                            pltpu.SemaphoreType.DMA((2, 3))],
        ),
        out_shape=jax.ShapeDtypeStruct((n_slots, dp), jnp.uint32),
        compiler_params=_params("arbitrary"),
        name="moe_experts",
    )(block_expert, n_used, slot, next_expert.astype(jnp.int32), xs, wg, wu, wd)


def _combine_kernel(dest_ref, ys_ref, x_ref, gate_ref, g_ref, b_ref, o_ref, buf_s, sem, *, tm, t_total):
    base = pl.program_id(0) * tm

    def issue(t, carry):
        for k in range(TOP_K):
            d = dest_ref[k * t_total + base + t]
            pltpu.make_async_copy(ys_ref.at[pl.ds(d, 1), :], buf_s.at[k, pl.ds(t, 1), :], sem).start()
        return carry

    lax.fori_loop(0, tm, issue, 0, unroll=DMA_ISSUE_UNROLL)

    def drain(t, carry):
        for k in range(TOP_K):
            pltpu.make_async_copy(ys_ref.at[pl.ds(0, 1), :], buf_s.at[0, pl.ds(0, 1), :], sem).wait()
        return carry

    lax.fori_loop(0, tm, drain, 0, unroll=DMA_ISSUE_UNROLL)
    gate = gate_ref[...]
    lo0, hi0 = _unpack_halves(buf_s[0])
    lo1, hi1 = _unpack_halves(buf_s[1])
    g0, g1 = gate[:, 0:1], gate[:, 1:2]
    y = jnp.concatenate([g0 * lo0 + g1 * lo1, g0 * hi0 + g1 * hi1], -1)
    o_ref[...] = _layer_norm(DN_ALPHA * x_ref[...] + y, g_ref[...], b_ref[...])


def _combine(dest_flat, ys, x, gate_tk, g, bias):
    t, d = x.shape
    tm = _tile(t, 512)
    return pl.pallas_call(
        functools.partial(_combine_kernel, tm=tm, t_total=t),
        grid_spec=pltpu.PrefetchScalarGridSpec(
            num_scalar_prefetch=1,
            grid=(t // tm,),
            in_specs=[
                pl.BlockSpec(memory_space=pl.ANY),
                pl.BlockSpec((tm, d), lambda i, dest: (i, 0)),
                pl.BlockSpec((tm, TOP_K), lambda i, dest: (i, 0)),
                pl.BlockSpec((1, d), lambda i, dest: (0, 0)),
                pl.BlockSpec((1, d), lambda i, dest: (0, 0)),
            ],
            out_specs=pl.BlockSpec((tm, d), lambda i, dest: (i, 0)),
            scratch_shapes=[pltpu.VMEM((TOP_K, tm, d // 2), jnp.uint32), pltpu.SemaphoreType.DMA(())],
        ),
        out_shape=jax.ShapeDtypeStruct((t, d), F32),
        compiler_params=_params("arbitrary"),
        name="moe_combine_ln",
    )(dest_flat, ys, x, gate_tk, g.reshape(1, d), bias.reshape(1, d))


def _moe(x, x_packed, w_router_t, router_bias, wg, wu, wd, layer, g, bias):
    t, d = x.shape
    idx, gate, rank, cnt = _router(x, w_router_t, router_bias)
    blk = MOE_BLOCK
    n_blocks = t * TOP_K // blk + N_EXPERTS
    counts = cnt[:, 0].astype(jnp.int32)
    padded = (counts + blk - 1) // blk * blk
    padded_end = jnp.cumsum(padded)
    padded_start = padded_end - padded
    dest = _slots(idx, rank, padded_start.astype(F32).reshape(N_EXPERTS, 1)).reshape(-1)
    n_used = (padded_end[-1] // blk).astype(jnp.int32)
    blocks = jnp.minimum(jnp.arange(n_blocks, dtype=jnp.int32), n_used - 1)
    block_expert = jnp.minimum(jnp.searchsorted(padded_end, blocks * blk, side='right'),
                               N_EXPERTS - 1).astype(jnp.int32)
    xs = _dispatch(dest, x_packed, n_blocks * blk)
    ys = _experts(block_expert, n_used.reshape(1), xs, wg, wu, wd, layer)
    return _combine(dest, ys, x, gate.T, g, bias)


def _in_proj_weights(w_in):
    sizes = (Q_LORA, KV_LORA, QK_ROPE, GDN_W, GDN_W, GDN_W, GDN_W, GDN_HEADS, GDN_HEADS, D_MODEL, D_MODEL)
    pts = np.cumsum(sizes)[:-1].tolist()
    c_q, c_kv, k_rope, g_q, g_k, g_v, g_z, g_b, g_a, gate_mla, gate_gdn = jnp.split(w_in, pts, axis=-1)
    half = QK_ROPE // 2
    k_rope_sw = jnp.concatenate([k_rope[:, half:], k_rope[:, :half]], -1)
    main = jnp.concatenate([g_q, g_k, g_v, g_z, gate_mla, gate_gdn], -1)
    used = Q_LORA + KV_LORA + 2 * QK_ROPE + 2 * GDN_HEADS
    small = jnp.concatenate([c_q, c_kv, k_rope, k_rope_sw, g_b, g_a,
                             jnp.zeros((w_in.shape[0], SMALL_W - used), w_in.dtype)], -1)
    return main.astype(BF16), small.astype(BF16)


def _mla_weights(w_uq, w_ukv):
    half = QK_ROPE // 2
    wq = w_uq.reshape(Q_LORA, MLA_HEADS, QK_DIM).transpose(1, 0, 2)
    pe = wq[..., QK_NOPE:]
    pe_sw = jnp.concatenate([pe[..., half:], pe[..., :half]], -1)
    wq = jnp.concatenate([wq, pe_sw], -1).astype(BF16)
    wkv = w_ukv.reshape(KV_LORA, MLA_HEADS, QK_NOPE + V_HEAD).transpose(1, 0, 2).astype(BF16)
    return wq, wkv


def _rope_table(positions):
    half = QK_ROPE // 2
    inv_freq = jnp.power(ROPE_BASE, -jnp.arange(half, dtype=F32) / half)
    ang = positions.astype(F32)[..., None] * inv_freq
    cos, sin = jnp.cos(ang), jnp.sin(ang)
    return jnp.concatenate([cos, cos, -sin, sin], -1)


def kernel(x, mem, positions, w_in, mla_q_norm, mla_kv_norm, w_uq, w_ukv, gdn_conv, gdn_a_log, gdn_dt_bias, gdn_o_norm, w_mla_proj, w_gdn_proj, w_mix_out, ln_mix_g, ln_mix_b, w_xq, w_xkv, w_xo, ln_x_g, ln_x_b, w_router, router_bias, w_gate, w_up, w_down, ln_moe_g, ln_moe_b):
    b, s, d = x.shape
    t = b * s
    cs = _rope_table(positions)
    mem2 = mem.reshape(-1, d)
    w_router_t = w_router.T.astype(F32)
    rbias = router_bias.reshape(N_EXPERTS, 1).astype(F32)
    for l in range(w_in.shape[0]):
        w_main, w_small = _in_proj_weights(w_in[l])
        wq, wkv = _mla_weights(w_uq[l], w_ukv[l])
        xt = x.reshape(t, d)
        main = _matmul(xt, w_main, F32, 1024, 1024, "in_proj_main").reshape(b, s, MAIN_W)
        small = _matmul(xt, w_small, F32, 1024, SMALL_W, "in_proj_small").reshape(b, s, SMALL_W)

        q, k, v = _mla_prep(small, cs, mla_q_norm[l].reshape(1, -1), mla_kv_norm[l].reshape(1, -1), wq, wkv)
        o_mla = _attention(q, k, v)

        small_t = small[:, :, SMALL_GATE_COL:SMALL_GATE_COL + 2 * GDN_HEADS].transpose(0, 2, 1)
        o_gdn = _gdn(main, gdn_conv[l], small, small_t, gdn_a_log[l], gdn_dt_bias[l], gdn_o_norm[l])

        merged = _merge(o_mla.reshape(t, d), o_gdn.reshape(t, d), main.reshape(t, MAIN_W),
                        w_mla_proj[l].astype(BF16), w_gdn_proj[l].astype(BF16))
        x1, x1_bf = _proj_ln(merged, w_mix_out[l].astype(BF16), x.reshape(t, d), ln_mix_g[l], ln_mix_b[l])

        kv = _matmul(mem2, w_xkv[l].astype(BF16), BF16, 1024, 1024, "mem_kv").reshape(b, -1, 2 * X_W)
        x2, x2_pk = _xattn(x1_bf.reshape(b, s, d), x1.reshape(b, s, d), w_xq[l].astype(BF16), kv,
                           w_xo[l].astype(BF16), ln_x_g[l], ln_x_b[l])

        x3 = _moe(x2.reshape(t, d), x2_pk.reshape(t, d // 2), w_router_t, rbias, w_gate, w_up, w_down, l,
                  ln_moe_g[l], ln_moe_b[l])
        x = x3.reshape(b, s, d)
    return x
```

```python
import functools
import math

import jax
import jax.numpy as jnp
import numpy as np
from jax import lax
from jax.experimental import pallas as pl
from jax.experimental.pallas import tpu as pltpu

D_MODEL = 2048
DEPTH = 2
MLA_HEADS = 16
Q_LORA = 512
KV_LORA = 512
QK_NOPE = 128
QK_ROPE = 64
V_HEAD = 128
QK_DIM = QK_NOPE + QK_ROPE
ROPE_BASE = 10000.0
GDN_HEADS = 16
GDN_DK = 128
GDN_DV = 128
CONV_K = 4
CHUNK = 64
GDN_W = GDN_HEADS * GDN_DK
X_HEADS = 4
X_HEAD_DIM = 128
X_W = X_HEADS * X_HEAD_DIM
N_EXPERTS = 64
N_GROUPS = 8
EXPERTS_PER_GROUP = N_EXPERTS // N_GROUPS
TOP_K = 2
D_EXPERT = 512
Q_PRESCALE = QK_DIM ** -0.5 * math.log2(math.e)
DN_ALPHA = (2.0 * DEPTH) ** 0.25
LN_EPS = 1e-5
RMS_EPS = 1e-6

MAIN_W = 3 * GDN_W + GDN_W + 2 * D_MODEL
SMALL_W = 1280
SMALL_ROPE_BLOCK = (Q_LORA + KV_LORA) // 128
SMALL_GATE_COL = Q_LORA + KV_LORA + 2 * QK_ROPE
SMALL_GATE_BLOCK = SMALL_GATE_COL // 128

GDN_HEAD_BLOCK = 16
GDN_ROW_BLOCK = 128
MOE_BLOCK = 256
DMA_ISSUE_UNROLL = 8
VMEM_LIMIT = 56 * 1024 * 1024

BF16 = jnp.bfloat16
F32 = jnp.float32


def _tile(n, pref):
    return pref if n % pref == 0 else n


def _params(*sem):
    return pltpu.CompilerParams(dimension_semantics=sem, vmem_limit_bytes=VMEM_LIMIT)


def _sigmoid(x):
    return 0.5 * jnp.tanh(0.5 * x) + 0.5


def _pack_halves(y):
    n = y.shape[1] // 2
    lo = pltpu.bitcast(y[:, :n].astype(BF16).astype(F32), jnp.uint32)
    hi = pltpu.bitcast(y[:, n:].astype(BF16).astype(F32), jnp.uint32)
    return (lo >> 16) | (hi & jnp.uint32(0xFFFF0000))


def _unpack_halves(p):
    lo = pltpu.bitcast(p << 16, F32)
    hi = pltpu.bitcast(p & jnp.uint32(0xFFFF0000), F32)
    return lo, hi


def _softplus(x):
    return jnp.maximum(x, 0.0) + jnp.log1p(jnp.exp(-jnp.abs(x)))


def _layer_norm(y, g, b):
    mu = jnp.mean(y, -1, keepdims=True)
    d = y - mu
    var = jnp.mean(d * d, -1, keepdims=True)
    return d * lax.rsqrt(var + LN_EPS) * g + b


def _dot(a, b):
    return jnp.dot(a, b, preferred_element_type=F32)


def _dot_nt(a, b):
    return lax.dot_general(a, b, (((1,), (1,)), ((), ())), preferred_element_type=F32)


def _dot_tn(a, b):
    return lax.dot_general(a, b, (((0,), (0,)), ((), ())), preferred_element_type=F32)


def _mm_kernel(a_ref, w_ref, o_ref, a_s):
    @pl.when(pl.program_id(1) == 0)
    def _():
        a_s[...] = a_ref[...].astype(BF16)

    o_ref[...] = _dot(a_s[...], w_ref[...]).astype(o_ref.dtype)


def _matmul(a, w, out_dtype, tm, tn, name):
    m, k = a.shape
    n = w.shape[1]
    tm, tn = _tile(m, tm), _tile(n, tn)
    return pl.pallas_call(
        _mm_kernel,
        grid=(m // tm, n // tn),
        in_specs=[pl.BlockSpec((tm, k), lambda i, j: (i, 0)),
                  pl.BlockSpec((k, tn), lambda i, j: (0, j))],
        out_specs=pl.BlockSpec((tm, tn), lambda i, j: (i, j)),
        out_shape=jax.ShapeDtypeStruct((m, n), out_dtype),
        scratch_shapes=[pltpu.VMEM((tm, k), BF16)],
        compiler_params=_params("parallel", "arbitrary"),
        name=name,
    )(a, w)


def _mla_prep_kernel(cq_ref, ckv_ref, kr_ref, cs_ref, qn_ref, kvn_ref, wq_ref, wkv_ref, q_ref, k_ref, v_ref):
    ts = cq_ref.shape[1]
    cq = cq_ref[0]
    cqn = (cq * lax.rsqrt(jnp.mean(cq * cq, -1, keepdims=True) + RMS_EPS) * qn_ref[...]).astype(BF16)
    ckv = ckv_ref[0]
    ckvn = (ckv * lax.rsqrt(jnp.mean(ckv * ckv, -1, keepdims=True) + RMS_EPS) * kvn_ref[...]).astype(BF16)
    cs = cs_ref[0]
    lane = lax.broadcasted_iota(jnp.int32, (ts, 128), 1)

    def rotate(t):
        u = t * cs
        return jnp.where(lane < QK_ROPE, u + pltpu.roll(u, QK_ROPE, 1), 0.0).astype(BF16)

    kpe = rotate(kr_ref[0])
    ones = jnp.ones((ts, V_HEAD), BF16)
    for h in range(MLA_HEADS):
        rq = _dot(cqn, wq_ref[h]) * Q_PRESCALE
        q_ref[0, h, :, :QK_NOPE] = rq[:, :QK_NOPE].astype(BF16)
        q_ref[0, h, :, QK_NOPE:] = rotate(rq[:, QK_NOPE:])
        rkv = _dot(ckvn, wkv_ref[h])
        k_ref[0, h, :, :QK_NOPE] = rkv[:, :QK_NOPE].astype(BF16)
        k_ref[0, h, :, QK_NOPE:] = kpe
        v_ref[0, h, :, :V_HEAD] = rkv[:, QK_NOPE:].astype(BF16)
        v_ref[0, h, :, V_HEAD:] = ones


def _mla_prep(small, cs, q_norm, kv_norm, wq, wkv):
    b, s, _ = small.shape
    ts = _tile(s, 512)
    hq = MLA_HEADS
    out_spec = pl.BlockSpec((1, hq, ts, 256), lambda bi, si: (bi, 0, si, 0))
    out_shape = jax.ShapeDtypeStruct((b, hq, s, 256), BF16)
    return pl.pallas_call(
        _mla_prep_kernel,
        grid=(b, s // ts),
        in_specs=[
            pl.BlockSpec((1, ts, Q_LORA), lambda bi, si: (bi, si, 0)),
            pl.BlockSpec((1, ts, KV_LORA), lambda bi, si: (bi, si, 1)),
            pl.BlockSpec((1, ts, 128), lambda bi, si: (bi, si, SMALL_ROPE_BLOCK)),
            pl.BlockSpec((1, ts, 128), lambda bi, si: (bi, si, 0)),
            pl.BlockSpec((1, Q_LORA), lambda bi, si: (0, 0)),
            pl.BlockSpec((1, KV_LORA), lambda bi, si: (0, 0)),
            pl.BlockSpec((hq, Q_LORA, 256), lambda bi, si: (0, 0, 0)),
            pl.BlockSpec((hq, KV_LORA, 256), lambda bi, si: (0, 0, 0)),
        ],
        out_specs=[out_spec, out_spec, out_spec],
        out_shape=[out_shape, out_shape, out_shape],
        compiler_params=_params("parallel", "parallel"),
        name="mla_prep",
    )(small, small, small, cs, q_norm, kv_norm, wq, wkv)


def _attn_kernel(q_ref, k_ref, v_ref, o_ref, sa_s, sb_s, m_s, acc_s, *, tq, rc):
    i = pl.program_id(2)
    dv2 = acc_s.shape[-1]
    m_s[...] = jnp.full(m_s.shape, -jnp.inf, F32)
    acc_s[...] = jnp.zeros(acc_s.shape, F32)

    def scores(j):
        start = pl.multiple_of(j * tq, tq)
        return _dot_nt(q_ref[0, 0], k_ref[0, 0, pl.ds(start, tq), :])

    def softmax_pv(s_ref, c, j, nk, masked):
        rows = slice(c * rc, (c + 1) * rc)
        start = pl.multiple_of(j * tq, tq)
        s = s_ref[rows, :nk]
        if masked:
            row = lax.broadcasted_iota(jnp.int32, (rc, nk), 0) + c * rc
            col = lax.broadcasted_iota(jnp.int32, (rc, nk), 1)
            s = jnp.where(col <= row, s, -jnp.inf)
        m_prev = m_s[rows, :]
        m_new = jnp.maximum(m_prev, jnp.max(s, -1, keepdims=True))
        alpha = jnp.exp2(m_prev - m_new)
        p = jnp.exp2(s - jnp.concatenate([m_new] * (nk // 128), -1))
        acc_s[rows, :] = (jnp.concatenate([alpha] * (dv2 // 128), -1) * acc_s[rows, :]
                          + _dot(p.astype(BF16), v_ref[0, 0, pl.ds(start, nk), :]))
        m_s[rows, :] = m_new

    nch = tq // rc

    def step(cur_ref, nxt_ref, j):
        nxt_ref[...] = scores(j + 1)
        for c in range(nch):
            softmax_pv(cur_ref, c, j, tq, False)

    def diagonal(cur_ref):
        for c in range(nch):
            softmax_pv(cur_ref, c, i, (c + 1) * rc, True)
        dv = dv2 // 2
        o_ref[0] = (acc_s[:, :dv] / acc_s[:, dv:]).astype(o_ref.dtype)

    sa_s[...] = scores(0)

    def pair(jj, carry):
        step(sa_s, sb_s, 2 * jj)
        step(sb_s, sa_s, 2 * jj + 1)
        return carry

    lax.fori_loop(0, i // 2, pair, 0)

    @pl.when(i % 2 == 0)
    def _():
        diagonal(sa_s)

    @pl.when(i % 2 == 1)
    def _():
        step(sa_s, sb_s, i - 1)
        diagonal(sb_s)


def _attention(q, k, v):
    b, h, s, dqk = q.shape
    dv2 = v.shape[-1]
    tq = _tile(s, 1024)
    rc = _tile(tq, 256)
    kern = functools.partial(_attn_kernel, tq=tq, rc=rc)
    return pl.pallas_call(
        kern,
        grid=(b, h, s // tq),
        in_specs=[
            pl.BlockSpec((1, 1, tq, dqk), lambda bi, hi, i: (bi, hi, i, 0)),
            pl.BlockSpec((1, 1, s, dqk), lambda bi, hi, i: (bi, hi, 0, 0)),
            pl.BlockSpec((1, 1, s, dv2), lambda bi, hi, i: (bi, hi, 0, 0)),
        ],
        out_specs=pl.BlockSpec((1, tq, dv2 // 2), lambda bi, hi, i: (bi, i, hi)),
        out_shape=jax.ShapeDtypeStruct((b, s, h * dv2 // 2), BF16),
        scratch_shapes=[pltpu.VMEM((tq, tq), F32), pltpu.VMEM((tq, tq), F32), pltpu.VMEM((tq, 128), F32),
                        pltpu.VMEM((tq, dv2), F32)],
        compiler_params=_params("parallel", "parallel", "arbitrary"),
        name="mla_attention",
    )(q, k, v)


def _gdn_kernel(qr_ref, kr_ref, vr_ref, wq_ref, wk_ref, wv_ref, z_ref, gate_ref, gatet_ref, arow_ref, dtrow_ref,
                acol_ref, dtcol_ref, onorm_ref, o_ref, state_s, xx_s, qkv_s, *, hb, rb):
    hg = pl.program_id(1)
    r = pl.program_id(2)

    @pl.when(r == 0)
    def _():
        state_s[...] = jnp.zeros(state_s.shape, F32)
        xx_s[:, 0:8, :] = jnp.zeros((3, 8, hb * 128), F32)

    @pl.when(r > 0)
    def _():
        xx_s[:, 0:8, :] = xx_s[:, rb:rb + 8, :]

    for ti, (x_ref, w_ref) in enumerate(((qr_ref, wq_ref), (kr_ref, wk_ref), (vr_ref, wv_ref))):
        xx_s[ti, 8:rb + 8, :] = x_ref[0]
        wc = w_ref[...]
        y = wc[CONV_K - 1:CONV_K, :] * xx_s[ti, 8:rb + 8, :]
        for j in range(CONV_K - 1):
            off = 8 - (CONV_K - 1) + j
            y = y + wc[j:j + 1, :] * xx_s[ti, off:off + rb, :]
        y = y * _sigmoid(y)
        if ti == 2:
            qkv_s[ti] = y
        else:
            for h in range(hb):
                yh = y[:, h * 128:(h + 1) * 128]
                yn = yh * lax.rsqrt(jnp.sum(yh * yh, -1, keepdims=True) + RMS_EPS)
                qkv_s[ti, :, h * 128:(h + 1) * 128] = yn * (GDN_DK ** -0.5) if ti == 0 else yn

    def head_rows(ti, c, h):
        return qkv_s[ti, c * CHUNK:(c + 1) * CHUNK, h * 128:(h + 1) * 128]

    blk = gate_ref[0]
    beta_tm = _sigmoid(blk)
    g_tm = -jnp.exp(arow_ref[...]) * _softplus(blk + dtrow_ref[...])
    g_t = -jnp.exp(acol_ref[...]) * _softplus(gatet_ref[0] + dtcol_ref[...])

    ri = lax.broadcasted_iota(jnp.int32, (CHUNK, CHUNK), 0)
    ci = lax.broadcasted_iota(jnp.int32, (CHUNK, CHUNK), 1)
    causal = ri >= ci
    strict = ri > ci
    tri = jnp.where(causal, 1.0, 0.0).astype(F32)
    tri_t = jnp.where(ri <= ci, 1.0, 0.0).astype(F32)
    eye = jnp.where(ri == ci, 1.0, 0.0).astype(F32)
    onorm = onorm_ref[...]

    nc = rb // CHUNK
    items = [(c, h) for c in range(nc) for h in range(hb)]
    rows = [slice(c * CHUNK, (c + 1) * CHUNK) for c in range(nc)]
    gc_tm = [jnp.dot(tri, g_tm[rows[c], :], precision=lax.Precision.HIGHEST, preferred_element_type=F32)
             for c in range(nc)]
    gc_t = [jnp.dot(g_t[:, rows[c]], tri_t, precision=lax.Precision.HIGHEST, preferred_element_type=F32)
            for c in range(nc)]
    sel_l = lax.broadcasted_iota(jnp.int32, (CHUNK, 128), 1)
    sel_s = lax.broadcasted_iota(jnp.int32, (32, CHUNK), 0)
    gcol, bcol, grow, glast, egc, kt, kb = {}, {}, {}, {}, {}, {}, {}
    for it in items:
        c, h = it
        lane = GDN_HEADS + hg * hb + h
        gcol[it] = jnp.sum(jnp.where(sel_l == lane, gc_tm[c], 0.0), -1, keepdims=True)
        bcol[it] = jnp.sum(jnp.where(sel_l == hg * hb + h, beta_tm[rows[c], :], 0.0), -1, keepdims=True)
        grow[it] = jnp.sum(jnp.where(sel_s == lane, gc_t[c], 0.0), 0, keepdims=True)
        glast[it] = gcol[it][CHUNK - 1:CHUNK, :]
        egc[it] = jnp.exp(gcol[it])
        k = head_rows(1, c, h)
        kt[it] = k.T
        kb[it] = k * bcol[it]
    kq = {it: _dot(jnp.concatenate([kb[it], head_rows(0, *it)], 0).astype(BF16),
                   kt[it].astype(BF16)) for it in items}
    amat, m, inv = {}, {}, {}
    for it in items:
        decay = jnp.exp(jnp.where(causal, gcol[it] - grow[it], -jnp.inf))
        amat[it] = (kq[it][CHUNK:] * decay).astype(BF16)
        m[it] = -jnp.where(strict, kq[it][:CHUNK] * decay, 0.0)
        inv[it] = eye + m[it]
    for it in items:
        mb = m[it].astype(BF16)
        m[it] = _dot(mb, mb)
    n_fac = int(round(math.log2(CHUNK))) - 1
    for t in range(n_fac):
        for it in items:
            mb = m[it].astype(BF16)
            if t < n_fac - 1:
                r2 = _dot(jnp.concatenate([inv[it], m[it]], 0).astype(BF16), mb)
                inv[it] = inv[it] + r2[:CHUNK]
                m[it] = r2[CHUNK:]
            else:
                inv[it] = inv[it] + _dot(inv[it].astype(BF16), mb)
    uw = {}
    for it in items:
        c, h = it
        vb = head_rows(2, c, h) * bcol[it]
        uw[it] = _dot(inv[it].astype(BF16), jnp.concatenate([vb, kb[it] * egc[it]], 1).astype(BF16))
    for c in range(nc):
        its = [(c, h) for h in range(hb)]
        st = {it: state_s[it[1]] for it in its}
        wq = {it: _dot(jnp.concatenate([uw[it][:, GDN_DV:], head_rows(0, *it) * egc[it]], 0).astype(BF16),
                       st[it].astype(BF16)) for it in its}
        vnb = {it: (uw[it][:, :GDN_DV] - wq[it][:CHUNK]).astype(BF16) for it in its}
        o = {it: wq[it][CHUNK:] + _dot(amat[it], vnb[it]) for it in its}
        for it in its:
            h = it[1]
            kdt = kt[it] * jnp.exp(glast[it] - grow[it])
            state_s[h] = st[it] * jnp.exp(glast[it]) + _dot(kdt.astype(BF16), vnb[it])
        for it in its:
            h = it[1]
            on = o[it] * lax.rsqrt(jnp.mean(o[it] * o[it], -1, keepdims=True) + RMS_EPS) * onorm
            zz = z_ref[0, rows[c], h * GDN_DV:(h + 1) * GDN_DV]
            o_ref[0, rows[c], h * GDN_DV:(h + 1) * GDN_DV] = (on * (zz * _sigmoid(zz))).astype(o_ref.dtype)


def _gdn(main, w_conv, small, small_t, a_log, dt_bias, o_norm):
    b, s, _ = main.shape
    hb = GDN_HEAD_BLOCK
    rb = _tile(s, GDN_ROW_BLOCK)
    ng = GDN_HEADS // hb
    zoff = 3 * GDN_W // (hb * 128)
    pad = lambda a, lo: jnp.zeros((128,), F32).at[lo:lo + GDN_HEADS].set(a.astype(F32))
    arow = pad(a_log, GDN_HEADS).reshape(1, 128)
    dtrow = pad(dt_bias, GDN_HEADS).reshape(1, 128)
    acol = arow[0, :32].reshape(32, 1)
    dtcol = dtrow[0, :32].reshape(32, 1)
    kern = functools.partial(_gdn_kernel, hb=hb, rb=rb)
    return pl.pallas_call(
        kern,
        grid=(b, ng, s // rb),
        in_specs=[
            pl.BlockSpec((1, rb, hb * 128), lambda bi, g, r: (bi, r, g)),
            pl.BlockSpec((1, rb, hb * 128), lambda bi, g, r: (bi, r, ng + g)),
            pl.BlockSpec((1, rb, hb * 128), lambda bi, g, r: (bi, r, 2 * ng + g)),
            pl.BlockSpec((CONV_K, hb * 128), lambda bi, g, r: (0, g)),
            pl.BlockSpec((CONV_K, hb * 128), lambda bi, g, r: (0, ng + g)),
            pl.BlockSpec((CONV_K, hb * 128), lambda bi, g, r: (0, 2 * ng + g)),
            pl.BlockSpec((1, rb, hb * 128), lambda bi, g, r: (bi, r, zoff + g)),
            pl.BlockSpec((1, rb, 128), lambda bi, g, r: (bi, r, SMALL_GATE_BLOCK)),
            pl.BlockSpec((1, 32, rb), lambda bi, g, r: (bi, 0, r)),
            pl.BlockSpec((1, 128), lambda bi, g, r: (0, 0)),
            pl.BlockSpec((1, 128), lambda bi, g, r: (0, 0)),
            pl.BlockSpec((32, 1), lambda bi, g, r: (0, 0)),
            pl.BlockSpec((32, 1), lambda bi, g, r: (0, 0)),
            pl.BlockSpec((1, 128), lambda bi, g, r: (0, 0)),
        ],
        out_specs=pl.BlockSpec((1, rb, hb * 128), lambda bi, g, r: (bi, r, g)),
        out_shape=jax.ShapeDtypeStruct((b, s, GDN_W), BF16),
        scratch_shapes=[pltpu.VMEM((hb, GDN_DK, GDN_DV), F32), pltpu.VMEM((3, rb + 8, hb * 128), F32),
                        pltpu.VMEM((3, rb, hb * 128), F32)],
        compiler_params=_params("parallel", "parallel", "arbitrary"),
        name="gdn_delta_rule",
    )(main, main, main, w_conv, w_conv, w_conv, main, small, small_t, arow, dtrow, acol, dtcol,
      o_norm.reshape(1, GDN_DV).astype(F32))


def _merge_kernel(om_ref, og_ref, gm_ref, gg_ref, wm_ref, wg_ref, o_ref):
    ym = _dot(om_ref[...], wm_ref[...])
    yg = _dot(og_ref[...], wg_ref[...])
    o_ref[...] = (_sigmoid(gm_ref[...]) * ym + _sigmoid(gg_ref[...]) * yg).astype(o_ref.dtype)


def _merge(o_mla, o_gdn, main, w_mla, w_gdn):
    t, d = o_mla.shape
    tm, tn = _tile(t, 512), _tile(d, 1024)
    goff = (4 * GDN_W) // tn
    nn = d // tn
    return pl.pallas_call(
        _merge_kernel,
        grid=(nn, t // tm),
        in_specs=[
            pl.BlockSpec((tm, d), lambda j, i: (i, 0)),
            pl.BlockSpec((tm, d), lambda j, i: (i, 0)),
            pl.BlockSpec((tm, tn), lambda j, i: (i, goff + j)),
            pl.BlockSpec((tm, tn), lambda j, i: (i, goff + nn + j)),
            pl.BlockSpec((d, tn), lambda j, i: (0, j)),
            pl.BlockSpec((d, tn), lambda j, i: (0, j)),
        ],
        out_specs=pl.BlockSpec((tm, tn), lambda j, i: (i, j)),
        out_shape=jax.ShapeDtypeStruct((t, d), BF16),
        compiler_params=_params("parallel", "parallel"),
        name="mixer_merge",
    )(o_mla, o_gdn, main, main, w_mla, w_gdn)


def _proj_ln_kernel(a_ref, w_ref, x_ref, g_ref, b_ref, o_ref, obf_ref):
    y = DN_ALPHA * x_ref[...] + _dot(a_ref[...], w_ref[...])
    out = _layer_norm(y, g_ref[...], b_ref[...])
    o_ref[...] = out
    obf_ref[...] = out.astype(BF16)


def _proj_ln(a, w, x, g, bias):
    t, k = a.shape
    d = w.shape[1]
    tm = _tile(t, 512)
    return pl.pallas_call(
        _proj_ln_kernel,
        grid=(t // tm,),
        in_specs=[
            pl.BlockSpec((tm, k), lambda i: (i, 0)),
            pl.BlockSpec((k, d), lambda i: (0, 0)),
            pl.BlockSpec((tm, d), lambda i: (i, 0)),
            pl.BlockSpec((1, d), lambda i: (0, 0)),
            pl.BlockSpec((1, d), lambda i: (0, 0)),
        ],
        out_specs=[pl.BlockSpec((tm, d), lambda i: (i, 0)), pl.BlockSpec((tm, d), lambda i: (i, 0))],
        out_shape=[jax.ShapeDtypeStruct((t, d), F32), jax.ShapeDtypeStruct((t, d), BF16)],
        compiler_params=_params("parallel"),
        name="proj_residual_ln",
    )(a, w, x, g.reshape(1, d), bias.reshape(1, d))


def _xattn_kernel(xbf_ref, x_ref, wq_ref, kv_ref, wo_ref, g_ref, b_ref, o_ref, opk_ref):
    q = _dot(xbf_ref[0], wq_ref[...])
    scale = X_HEAD_DIM ** -0.5
    outs = []
    for h in range(X_HEADS):
        qh = q[:, h * X_HEAD_DIM:(h + 1) * X_HEAD_DIM].astype(BF16)
        kh = kv_ref[0, :, h * X_HEAD_DIM:(h + 1) * X_HEAD_DIM]
        vh = kv_ref[0, :, X_W + h * X_HEAD_DIM:X_W + (h + 1) * X_HEAD_DIM]
        s = _dot_nt(qh, kh) * scale
        e = jnp.exp(s - jnp.max(s, -1, keepdims=True))
        p = e / jnp.sum(e, -1, keepdims=True)
        outs.append(_dot(p.astype(BF16), vh))
    o = jnp.concatenate(outs, -1).astype(BF16)
    y = DN_ALPHA * x_ref[0] + _dot(o, wo_ref[...])
    out = _layer_norm(y, g_ref[...], b_ref[...])
    o_ref[0] = out
    opk_ref[0] = _pack_halves(out)


def _xattn(x_bf, x, wq, kv, wo, g, bias):
    b, s, d = x.shape
    tm = _tile(s, 512)
    ml = kv.shape[1]
    return pl.pallas_call(
        _xattn_kernel,
        grid=(b, s // tm),
        in_specs=[
            pl.BlockSpec((1, tm, d), lambda bi, i: (bi, i, 0)),
            pl.BlockSpec((1, tm, d), lambda bi, i: (bi, i, 0)),
            pl.BlockSpec((d, X_W), lambda bi, i: (0, 0)),
            pl.BlockSpec((1, ml, 2 * X_W), lambda bi, i: (bi, 0, 0)),
            pl.BlockSpec((X_W, d), lambda bi, i: (0, 0)),
            pl.BlockSpec((1, d), lambda bi, i: (0, 0)),
            pl.BlockSpec((1, d), lambda bi, i: (0, 0)),
        ],
        out_specs=[pl.BlockSpec((1, tm, d), lambda bi, i: (bi, i, 0)),
                   pl.BlockSpec((1, tm, d // 2), lambda bi, i: (bi, i, 0))],
        out_shape=[jax.ShapeDtypeStruct((b, s, d), F32), jax.ShapeDtypeStruct((b, s, d // 2), jnp.uint32)],
        compiler_params=_params("parallel", "parallel"),
        name="memory_cross_attention",
    )(x_bf, x, wq, kv, wo, g.reshape(1, d), bias.reshape(1, d))


def _first_argmax(vals, iota, n):
    m = jnp.max(vals, 0, keepdims=True)
    idx = jnp.min(jnp.where(vals == m, iota, n), 0, keepdims=True)
    return m, idx


def _router_kernel(x_ref, wr_ref, bias_ref, idx_ref, gate_ref, rank_ref, cnt_ref, carry_s, *, tm):
    i = pl.program_id(0)

    @pl.when(i == 0)
    def _():
        carry_s[...] = jnp.zeros(carry_s.shape, F32)

    logits = _dot_nt(wr_ref[...].astype(BF16), x_ref[...].astype(BF16))
    scores = 1.0 / (1.0 + jnp.exp(-logits))
    sel = scores + bias_ref[...]
    epg = EXPERTS_PER_GROUP
    io8 = lax.broadcasted_iota(jnp.int32, (epg, tm), 0)
    best_score = None
    for g in range(N_GROUPS):
        sg = sel[g * epg:(g + 1) * epg, :]
        m1, i1 = _first_argmax(sg, io8, epg)
        m2 = jnp.max(jnp.where(io8 == i1, -jnp.inf, sg), 0, keepdims=True)
        gs = m1 + m2
        if g == 0:
            best_score, best = gs, jnp.zeros((1, tm), jnp.int32)
            in_sel, in_sc = sg, scores[0:epg, :]
        else:
            better = gs > best_score
            best_score = jnp.where(better, gs, best_score)
            best = jnp.where(better, g, best)
            in_sel = jnp.where(better, sg, in_sel)
            in_sc = jnp.where(better, scores[g * epg:(g + 1) * epg, :], in_sc)
    _, l1 = _first_argmax(in_sel, io8, epg)
    _, l2 = _first_argmax(jnp.where(io8 == l1, -jnp.inf, in_sel), io8, epg)
    g1 = jnp.sum(jnp.where(io8 == l1, in_sc, 0.0), 0, keepdims=True)
    g2 = jnp.sum(jnp.where(io8 == l2, in_sc, 0.0), 0, keepdims=True)
    e1 = best * epg + l1
    e2 = best * epg + l2
    idx_ref[0:1, :] = e1
    idx_ref[1:2, :] = e2
    gsum = g1 + g2
    gate_ref[0:1, :] = g1 / gsum
    gate_ref[1:2, :] = g2 / gsum

    ioe = lax.broadcasted_iota(jnp.int32, (N_EXPERTS, tm), 0)
    oh1 = ioe == e1
    oh2 = ioe == e2
    onehot = jnp.where(oh1, 1.0, 0.0) + jnp.where(oh2, 1.0, 0.0)
    rr = lax.broadcasted_iota(jnp.int32, (tm, tm), 0)
    cc = lax.broadcasted_iota(jnp.int32, (tm, tm), 1)
    upper = jnp.where(rr < cc, 1.0, 0.0).astype(BF16)
    tot = carry_s[...] + _dot(onehot.astype(BF16), upper)
    rank_ref[0:1, :] = jnp.sum(jnp.where(oh1, tot, 0.0), 0, keepdims=True).astype(jnp.int32)
    rank_ref[1:2, :] = jnp.sum(jnp.where(oh2, tot, 0.0), 0, keepdims=True).astype(jnp.int32)
    new = carry_s[...] + jnp.sum(onehot, -1, keepdims=True)
    carry_s[...] = new
    cnt_ref[...] = jnp.broadcast_to(new, cnt_ref.shape)


def _router(x, w_router_t, bias):
    t, d = x.shape
    tm = _tile(t, 512)
    return pl.pallas_call(
        functools.partial(_router_kernel, tm=tm),
        grid=(t // tm,),
        in_specs=[pl.BlockSpec((tm, d), lambda i: (i, 0)),
                  pl.BlockSpec((N_EXPERTS, d), lambda i: (0, 0)),
                  pl.BlockSpec((N_EXPERTS, 1), lambda i: (0, 0))],
        out_specs=[pl.BlockSpec((TOP_K, tm), lambda i: (0, i)),
                   pl.BlockSpec((TOP_K, tm), lambda i: (0, i)),
                   pl.BlockSpec((TOP_K, tm), lambda i: (0, i)),
                   pl.BlockSpec((N_EXPERTS, 128), lambda i: (0, 0))],
        out_shape=[jax.ShapeDtypeStruct((TOP_K, t), jnp.int32),
                   jax.ShapeDtypeStruct((TOP_K, t), F32),
                   jax.ShapeDtypeStruct((TOP_K, t), jnp.int32),
                   jax.ShapeDtypeStruct((N_EXPERTS, 128), F32)],
        scratch_shapes=[pltpu.VMEM((N_EXPERTS, 1), F32)],
        compiler_params=_params("arbitrary"),
        name="moe_router",
    )(x, w_router_t, bias)


def _slot_kernel(idx_ref, rank_ref, pstart_ref, dest_ref, *, tm):
    ioe = lax.broadcasted_iota(jnp.int32, (N_EXPERTS, tm), 0)
    pstart = pstart_ref[...]
    for k in range(TOP_K):
        start = jnp.sum(jnp.where(ioe == idx_ref[k:k + 1, :], pstart, 0.0), 0, keepdims=True)
        dest_ref[k:k + 1, :] = start.astype(jnp.int32) + rank_ref[k:k + 1, :]


def _slots(idx, rank, pstart):
    t = idx.shape[1]
    tm = _tile(t, 2048)
    spec = pl.BlockSpec((TOP_K, tm), lambda i: (0, i))
    return pl.pallas_call(
        functools.partial(_slot_kernel, tm=tm),
        grid=(t // tm,),
        in_specs=[spec, spec, pl.BlockSpec((N_EXPERTS, 1), lambda i: (0, 0))],
        out_specs=spec,
        out_shape=jax.ShapeDtypeStruct((TOP_K, t), jnp.int32),
        compiler_params=_params("parallel"),
        name="moe_slots",
    )(idx, rank, pstart)


def _dispatch_kernel(dest_ref, x_ref, xs_in_ref, xs_ref, sem, *, tm, t_total):
    del xs_in_ref
    base = pl.program_id(0) * tm

    def issue(t, carry):
        for k in range(TOP_K):
            d = dest_ref[k * t_total + base + t]
            pltpu.make_async_copy(x_ref.at[pl.ds(t, 1), :], xs_ref.at[pl.ds(d, 1), :], sem).start()
        return carry

    lax.fori_loop(0, tm, issue, 0, unroll=DMA_ISSUE_UNROLL)

    def drain(t, carry):
        for k in range(TOP_K):
            pltpu.make_async_copy(x_ref.at[pl.ds(0, 1), :], xs_ref.at[pl.ds(0, 1), :], sem).wait()
        return carry

    lax.fori_loop(0, tm, drain, 0, unroll=DMA_ISSUE_UNROLL)


def _dispatch(dest_flat, x, n_slots):
    t, d = x.shape
    tm = _tile(t, 512)
    xs0 = jnp.zeros((n_slots, d), x.dtype)
    return pl.pallas_call(
        functools.partial(_dispatch_kernel, tm=tm, t_total=t),
        grid_spec=pltpu.PrefetchScalarGridSpec(
            num_scalar_prefetch=1,
            grid=(t // tm,),
            in_specs=[pl.BlockSpec((tm, d), lambda i, dest: (i, 0)),
                      pl.BlockSpec(memory_space=pl.ANY)],
            out_specs=pl.BlockSpec(memory_space=pl.ANY),
            scratch_shapes=[pltpu.SemaphoreType.DMA(())],
        ),
        out_shape=jax.ShapeDtypeStruct((n_slots, d), x.dtype),
        input_output_aliases={2: 0},
        compiler_params=_params("arbitrary"),
        name="moe_dispatch",
    )(dest_flat, x, xs0)


def _expert_kernel(be_ref, nu_ref, slot_ref, nx_ref, xs_ref, wg_hbm, wu_hbm, wd_hbm, ys_ref,
                   fg_s, fu_s, fd_s, wg_s, wu_s, wd_s, sems, *, layer):
    i = pl.program_id(0)
    e = be_ref[i]

    def weight_copies(expert, slot):
        return (pltpu.make_async_copy(wg_hbm.at[layer, expert], fg_s.at[slot], sems.at[slot, 0]),
                pltpu.make_async_copy(wu_hbm.at[layer, expert], fu_s.at[slot], sems.at[slot, 1]),
                pltpu.make_async_copy(wd_hbm.at[layer, expert], fd_s.at[slot], sems.at[slot, 2]))

    @pl.when(i == 0)
    def _():
        for cp in weight_copies(e, 0):
            cp.start()

    @pl.when((i == 0) | (e != be_ref[jnp.maximum(i - 1, 0)]))
    def _():
        slot = slot_ref[i]
        for cp in weight_copies(e, slot):
            cp.wait()
        wg_s[...] = fg_s[slot].astype(BF16)
        wu_s[...] = fu_s[slot].astype(BF16)
        wd_s[...] = fd_s[slot].astype(BF16)

        @pl.when(nx_ref[i] >= 0)
        def _():
            for cp in weight_copies(nx_ref[i], 1 - slot):
                cp.start()

    @pl.when(i < nu_ref[0])
    def _():
        lo, hi = _unpack_halves(xs_ref[...])
        xb = jnp.concatenate([lo.astype(BF16), hi.astype(BF16)], -1)
        hg = _dot(xb, wg_s[...])
        hu = _dot(xb, wu_s[...])
        h = (hg * _sigmoid(hg)) * hu
        ys_ref[...] = _pack_halves(_dot(h.astype(BF16), wd_s[...]))

    @pl.when(i >= nu_ref[0])
    def _():
        ys_ref[...] = jnp.zeros(ys_ref.shape, ys_ref.dtype)


def _experts(block_expert, n_used, xs, wg, wu, wd, layer):
    n_slots, dp = xs.shape
    d = 2 * dp
    blk = MOE_BLOCK
    n_blocks = n_slots // blk
    de = wg.shape[-1]
    pos = jnp.arange(n_blocks, dtype=jnp.int32)
    first = jnp.concatenate([jnp.ones((1,), bool), block_expert[1:] != block_expert[:-1]])
    slot = ((jnp.cumsum(first.astype(jnp.int32)) - 1) % 2).astype(jnp.int32)
    first_pos = jnp.where(first, pos, n_blocks)
    next_first = lax.cummin(jnp.concatenate([first_pos[1:], jnp.full((1,), n_blocks, jnp.int32)]), reverse=True)
    next_expert = jnp.where(next_first < n_blocks, block_expert[jnp.minimum(next_first, n_blocks - 1)], -1)
    hbm = pl.BlockSpec(memory_space=pl.ANY)
    return pl.pallas_call(
        functools.partial(_expert_kernel, layer=layer),
        grid_spec=pltpu.PrefetchScalarGridSpec(
            num_scalar_prefetch=4,
            grid=(n_blocks,),
            in_specs=[
                pl.BlockSpec((blk, dp), lambda i, be, nu, sl, nx: (jnp.minimum(i, nu[0] - 1), 0)),
                hbm, hbm, hbm,
            ],
            out_specs=pl.BlockSpec((blk, dp), lambda i, be, nu, sl, nx: (i, 0)),
            scratch_shapes=[pltpu.VMEM((2, d, de), F32), pltpu.VMEM((2, d, de), F32), pltpu.VMEM((2, de, d), F32),
                            pltpu.VMEM((d, de), BF16), pltpu.VMEM((d, de), BF16), pltpu.VMEM((de, d), BF16),
                            pltpu.SemaphoreType.DMA((2, 3))],
        ),
        out_shape=jax.ShapeDtypeStruct((n_slots, dp), jnp.uint32),
        compiler_params=_params("arbitrary"),
        name="moe_experts",
    )(block_expert, n_used, slot, next_expert.astype(jnp.int32), xs, wg, wu, wd)


def _combine_kernel(dest_ref, ys_ref, x_ref, gate_ref, g_ref, b_ref, o_ref, buf_s, sem, *, tm, t_total):
    base = pl.program_id(0) * tm

    def issue(t, carry):
        for k in range(TOP_K):
            d = dest_ref[k * t_total + base + t]
            pltpu.make_async_copy(ys_ref.at[pl.ds(d, 1), :], buf_s.at[k, pl.ds(t, 1), :], sem).start()
        return carry

    lax.fori_loop(0, tm, issue, 0, unroll=DMA_ISSUE_UNROLL)

    def drain(t, carry):
        for k in range(TOP_K):
            pltpu.make_async_copy(ys_ref.at[pl.ds(0, 1), :], buf_s.at[0, pl.ds(0, 1), :], sem).wait()
        return carry

    lax.fori_loop(0, tm, drain, 0, unroll=DMA_ISSUE_UNROLL)
    gate = gate_ref[...]
    lo0, hi0 = _unpack_halves(buf_s[0])
    lo1, hi1 = _unpack_halves(buf_s[1])
    g0, g1 = gate[:, 0:1], gate[:, 1:2]
    y = jnp.concatenate([g0 * lo0 + g1 * lo1, g0 * hi0 + g1 * hi1], -1)
    o_ref[...] = _layer_norm(DN_ALPHA * x_ref[...] + y, g_ref[...], b_ref[...])


def _combine(dest_flat, ys, x, gate_tk, g, bias):
    t, d = x.shape
    tm = _tile(t, 512)
    return pl.pallas_call(
        functools.partial(_combine_kernel, tm=tm, t_total=t),
        grid_spec=pltpu.PrefetchScalarGridSpec(
            num_scalar_prefetch=1,
            grid=(t // tm,),
            in_specs=[
                pl.BlockSpec(memory_space=pl.ANY),
                pl.BlockSpec((tm, d), lambda i, dest: (i, 0)),
                pl.BlockSpec((tm, TOP_K), lambda i, dest: (i, 0)),
                pl.BlockSpec((1, d), lambda i, dest: (0, 0)),
                pl.BlockSpec((1, d), lambda i, dest: (0, 0)),
            ],
            out_specs=pl.BlockSpec((tm, d), lambda i, dest: (i, 0)),
            scratch_shapes=[pltpu.VMEM((TOP_K, tm, d // 2), jnp.uint32), pltpu.SemaphoreType.DMA(())],
        ),
        out_shape=jax.ShapeDtypeStruct((t, d), F32),
        compiler_params=_params("arbitrary"),
        name="moe_combine_ln",
    )(dest_flat, ys, x, gate_tk, g.reshape(1, d), bias.reshape(1, d))


def _moe(x, x_packed, w_router_t, router_bias, wg, wu, wd, layer, g, bias):
    t, d = x.shape
    idx, gate, rank, cnt = _router(x, w_router_t, router_bias)
    blk = MOE_BLOCK
    n_blocks = t * TOP_K // blk + N_EXPERTS
    counts = cnt[:, 0].astype(jnp.int32)
    padded = (counts + blk - 1) // blk * blk
    padded_end = jnp.cumsum(padded)
    padded_start = padded_end - padded
    dest = _slots(idx, rank, padded_start.astype(F32).reshape(N_EXPERTS, 1)).reshape(-1)
    n_used = (padded_end[-1] // blk).astype(jnp.int32)
    blocks = jnp.minimum(jnp.arange(n_blocks, dtype=jnp.int32), n_used - 1)
    block_expert = jnp.minimum(jnp.searchsorted(padded_end, blocks * blk, side='right'),
                               N_EXPERTS - 1).astype(jnp.int32)
    xs = _dispatch(dest, x_packed, n_blocks * blk)
    ys = _experts(block_expert, n_used.reshape(1), xs, wg, wu, wd, layer)
    return _combine(dest, ys, x, gate.T, g, bias)


def _in_proj_weights(w_in):
    sizes = (Q_LORA, KV_LORA, QK_ROPE, GDN_W, GDN_W, GDN_W, GDN_W, GDN_HEADS, GDN_HEADS, D_MODEL, D_MODEL)
    pts = np.cumsum(sizes)[:-1].tolist()
    c_q, c_kv, k_rope, g_q, g_k, g_v, g_z, g_b, g_a, gate_mla, gate_gdn = jnp.split(w_in, pts, axis=-1)
    half = QK_ROPE // 2
    k_rope_sw = jnp.concatenate([k_rope[:, half:], k_rope[:, :half]], -1)
    main = jnp.concatenate([g_q, g_k, g_v, g_z, gate_mla, gate_gdn], -1)
    used = Q_LORA + KV_LORA + 2 * QK_ROPE + 2 * GDN_HEADS
    small = jnp.concatenate([c_q, c_kv, k_rope, k_rope_sw, g_b, g_a,
                             jnp.zeros((w_in.shape[0], SMALL_W - used), w_in.dtype)], -1)
    return main.astype(BF16), small.astype(BF16)


def _mla_weights(w_uq, w_ukv):
    half = QK_ROPE // 2
    wq = w_uq.reshape(Q_LORA, MLA_HEADS, QK_DIM).transpose(1, 0, 2)
    pe = wq[..., QK_NOPE:]
    pe_sw = jnp.concatenate([pe[..., half:], pe[..., :half]], -1)
    wq = jnp.concatenate([wq, pe_sw], -1).astype(BF16)
    wkv = w_ukv.reshape(KV_LORA, MLA_HEADS, QK_NOPE + V_HEAD).transpose(1, 0, 2).astype(BF16)
    return wq, wkv


def _rope_table(positions):
    half = QK_ROPE // 2
    inv_freq = jnp.power(ROPE_BASE, -jnp.arange(half, dtype=F32) / half)
    ang = positions.astype(F32)[..., None] * inv_freq
    cos, sin = jnp.cos(ang), jnp.sin(ang)
    return jnp.concatenate([cos, cos, -sin, sin], -1)


def kernel(x, mem, positions, w_in, mla_q_norm, mla_kv_norm, w_uq, w_ukv, gdn_conv, gdn_a_log, gdn_dt_bias, gdn_o_norm, w_mla_proj, w_gdn_proj, w_mix_out, ln_mix_g, ln_mix_b, w_xq, w_xkv, w_xo, ln_x_g, ln_x_b, w_router, router_bias, w_gate, w_up, w_down, ln_moe_g, ln_moe_b):
    b, s, d = x.shape
    t = b * s
    cs = _rope_table(positions)
    mem2 = mem.reshape(-1, d)
    w_router_t = w_router.T.astype(F32)
    rbias = router_bias.reshape(N_EXPERTS, 1).astype(F32)
    for l in range(w_in.shape[0]):
        w_main, w_small = _in_proj_weights(w_in[l])
        wq, wkv = _mla_weights(w_uq[l], w_ukv[l])
        xt = x.reshape(t, d)
        main = _matmul(xt, w_main, F32, 1024, 2048, "in_proj_main").reshape(b, s, MAIN_W)
        small = _matmul(xt, w_small, F32, 1024, SMALL_W, "in_proj_small").reshape(b, s, SMALL_W)

        q, k, v = _mla_prep(small, cs, mla_q_norm[l].reshape(1, -1), mla_kv_norm[l].reshape(1, -1), wq, wkv)
        o_mla = _attention(q, k, v)

        small_t = small[:, :, SMALL_GATE_COL:SMALL_GATE_COL + 2 * GDN_HEADS].transpose(0, 2, 1)
        o_gdn = _gdn(main, gdn_conv[l], small, small_t, gdn_a_log[l], gdn_dt_bias[l], gdn_o_norm[l])

        merged = _merge(o_mla.reshape(t, d), o_gdn.reshape(t, d), main.reshape(t, MAIN_W),
                        w_mla_proj[l].astype(BF16), w_gdn_proj[l].astype(BF16))
        x1, x1_bf = _proj_ln(merged, w_mix_out[l].astype(BF16), x.reshape(t, d), ln_mix_g[l], ln_mix_b[l])

        kv = _matmul(mem2, w_xkv[l].astype(BF16), BF16, 1024, 1024, "mem_kv").reshape(b, -1, 2 * X_W)
        x2, x2_pk = _xattn(x1_bf.reshape(b, s, d), x1.reshape(b, s, d), w_xq[l].astype(BF16), kv,
                           w_xo[l].astype(BF16), ln_x_g[l], ln_x_b[l])

        x3 = _moe(x2.reshape(t, d), x2_pk.reshape(t, d // 2), w_router_t, rbias, w_gate, w_up, w_down, l,
                  ln_moe_g[l], ln_moe_b[l])
        x = x3.reshape(b, s, d)
    return x
```

```python
import functools
import math

import jax
import jax.numpy as jnp
import numpy as np
from jax import lax
from jax.experimental import pallas as pl
from jax.experimental.pallas import tpu as pltpu

D_MODEL = 2048
DEPTH = 2
MLA_HEADS = 16
Q_LORA = 512
KV_LORA = 512
QK_NOPE = 128
QK_ROPE = 64
V_HEAD = 128
QK_DIM = QK_NOPE + QK_ROPE
ROPE_BASE = 10000.0
GDN_HEADS = 16
GDN_DK = 128
GDN_DV = 128
CONV_K = 4
CHUNK = 64
GDN_W = GDN_HEADS * GDN_DK
X_HEADS = 4
X_HEAD_DIM = 128
X_W = X_HEADS * X_HEAD_DIM
N_EXPERTS = 64
N_GROUPS = 8
EXPERTS_PER_GROUP = N_EXPERTS // N_GROUPS
TOP_K = 2
D_EXPERT = 512
Q_PRESCALE = QK_DIM ** -0.5 * math.log2(math.e)
DN_ALPHA = (2.0 * DEPTH) ** 0.25
LN_EPS = 1e-5
RMS_EPS = 1e-6

MAIN_W = 3 * GDN_W + GDN_W + 2 * D_MODEL
SMALL_W = 1280
SMALL_ROPE_BLOCK = (Q_LORA + KV_LORA) // 128
SMALL_GATE_COL = Q_LORA + KV_LORA + 2 * QK_ROPE
SMALL_GATE_BLOCK = SMALL_GATE_COL // 128

GDN_HEAD_BLOCK = 16
GDN_ROW_BLOCK = 128
MOE_BLOCK = 256
DMA_ISSUE_UNROLL = 8
VMEM_LIMIT = 56 * 1024 * 1024

BF16 = jnp.bfloat16
F32 = jnp.float32


def _tile(n, pref):
    return pref if n % pref == 0 else n


def _params(*sem):
    return pltpu.CompilerParams(dimension_semantics=sem, vmem_limit_bytes=VMEM_LIMIT)


def _sigmoid(x):
    return 0.5 * jnp.tanh(0.5 * x) + 0.5


def _pack_halves(y):
    n = y.shape[1] // 2
    lo = pltpu.bitcast(y[:, :n].astype(BF16).astype(F32), jnp.uint32)
    hi = pltpu.bitcast(y[:, n:].astype(BF16).astype(F32), jnp.uint32)
    return (lo >> 16) | (hi & jnp.uint32(0xFFFF0000))


def _unpack_halves(p):
    lo = pltpu.bitcast(p << 16, F32)
    hi = pltpu.bitcast(p & jnp.uint32(0xFFFF0000), F32)
    return lo, hi


def _softplus(x):
    return jnp.maximum(x, 0.0) + jnp.log1p(jnp.exp(-jnp.abs(x)))


def _layer_norm(y, g, b):
    mu = jnp.mean(y, -1, keepdims=True)
    d = y - mu
    var = jnp.mean(d * d, -1, keepdims=True)
    return d * lax.rsqrt(var + LN_EPS) * g + b


def _dot(a, b):
    return jnp.dot(a, b, preferred_element_type=F32)


def _dot_nt(a, b):
    return lax.dot_general(a, b, (((1,), (1,)), ((), ())), preferred_element_type=F32)


def _dot_tn(a, b):
    return lax.dot_general(a, b, (((0,), (0,)), ((), ())), preferred_element_type=F32)


def _mm_kernel(a_ref, w_ref, o_ref, a_s):
    @pl.when(pl.program_id(1) == 0)
    def _():
        a_s[...] = a_ref[...].astype(BF16)

    o_ref[...] = _dot(a_s[...], w_ref[...]).astype(o_ref.dtype)


def _matmul(a, w, out_dtype, tm, tn, name):
    m, k = a.shape
    n = w.shape[1]
    tm, tn = _tile(m, tm), _tile(n, tn)
    return pl.pallas_call(
        _mm_kernel,
        grid=(m // tm, n // tn),
        in_specs=[pl.BlockSpec((tm, k), lambda i, j: (i, 0)),
                  pl.BlockSpec((k, tn), lambda i, j: (0, j))],
        out_specs=pl.BlockSpec((tm, tn), lambda i, j: (i, j)),
        out_shape=jax.ShapeDtypeStruct((m, n), out_dtype),
        scratch_shapes=[pltpu.VMEM((tm, k), BF16)],
        compiler_params=_params("parallel", "arbitrary"),
        name=name,
    )(a, w)


def _in_proj_main_kernel(a_ref, w_ref, wc_ref, o_ref, a_s, tail_s, xx_s, *, tiles_per_seq, conv_tiles):
    i = pl.program_id(0)
    j = pl.program_id(1)
    tm, tn = o_ref.shape

    @pl.when(j == 0)
    def _():
        a_s[...] = a_ref[...].astype(BF16)

    @pl.when(j >= conv_tiles)
    def _():
        o_ref[...] = _dot(a_s[...], w_ref[...])

    def conv_tile(kind):
        acc = _dot(a_s[...], w_ref[...])
        first = i % tiles_per_seq == 0
        xx_s[0:8, :] = jnp.where(first, 0.0, tail_s[j])
        xx_s[8:tm + 8, :] = acc
        tail_s[j] = acc[tm - 8:, :]
        wc = wc_ref[...]
        y = wc[CONV_K - 1:CONV_K, :] * acc
        for tap in range(CONV_K - 1):
            off = 8 - (CONV_K - 1) + tap
            y = y + wc[tap:tap + 1, :] * xx_s[off:off + tm, :]
        half = 0.5 * y
        y = half + half * jnp.tanh(half)
        if kind == "v":
            o_ref[...] = y
        else:
            scale = GDN_DK ** -0.5 if kind == "q" else 1.0
            for h in range(tn // GDN_DK):
                yh = y[:, h * GDN_DK:(h + 1) * GDN_DK]
                o_ref[:, h * GDN_DK:(h + 1) * GDN_DK] = yh * (
                    lax.rsqrt(jnp.sum(yh * yh, -1, keepdims=True) + RMS_EPS) * scale)

    third = conv_tiles // 3
    for n, kind in enumerate(("q", "k", "v")):
        pl.when((j >= n * third) & (j < (n + 1) * third))(functools.partial(conv_tile, kind))


def _in_proj_main(a, w, w_conv, seq_len):
    m, k = a.shape
    n = w.shape[1]
    tm, tn = _tile(seq_len, 1024), 1024
    conv_tiles = 3 * GDN_W // tn
    return pl.pallas_call(
        functools.partial(_in_proj_main_kernel, tiles_per_seq=seq_len // tm, conv_tiles=conv_tiles),
        grid=(m // tm, n // tn),
        in_specs=[pl.BlockSpec((tm, k), lambda i, j: (i, 0)),
                  pl.BlockSpec((k, tn), lambda i, j: (0, j)),
                  pl.BlockSpec((CONV_K, tn), lambda i, j: (0, jnp.minimum(j, conv_tiles - 1)))],
        out_specs=pl.BlockSpec((tm, tn), lambda i, j: (i, j)),
        out_shape=jax.ShapeDtypeStruct((m, n), F32),
        scratch_shapes=[pltpu.VMEM((tm, k), BF16), pltpu.VMEM((conv_tiles, 8, tn), F32),
                        pltpu.VMEM((tm + 8, tn), F32)],
        compiler_params=_params("arbitrary", "arbitrary"),
        name="in_proj_main",
    )(a, w, w_conv)


def _mla_prep_kernel(cq_ref, ckv_ref, kr_ref, cs_ref, qn_ref, kvn_ref, wq_ref, wkv_ref, q_ref, k_ref, v_ref):
    ts = cq_ref.shape[1]
    cq = cq_ref[0]
    cqn = (cq * lax.rsqrt(jnp.mean(cq * cq, -1, keepdims=True) + RMS_EPS) * qn_ref[...]).astype(BF16)
    ckv = ckv_ref[0]
    ckvn = (ckv * lax.rsqrt(jnp.mean(ckv * ckv, -1, keepdims=True) + RMS_EPS) * kvn_ref[...]).astype(BF16)
    cs = cs_ref[0]
    lane = lax.broadcasted_iota(jnp.int32, (ts, 128), 1)

    def rotate(t):
        u = t * cs
        return jnp.where(lane < QK_ROPE, u + pltpu.roll(u, QK_ROPE, 1), 0.0).astype(BF16)

    kpe = rotate(kr_ref[0])
    ones = jnp.ones((ts, V_HEAD), BF16)
    for h in range(MLA_HEADS):
        rq = _dot(cqn, wq_ref[h]) * Q_PRESCALE
        q_ref[0, h, :, :QK_NOPE] = rq[:, :QK_NOPE].astype(BF16)
        q_ref[0, h, :, QK_NOPE:] = rotate(rq[:, QK_NOPE:])
        rkv = _dot(ckvn, wkv_ref[h])
        k_ref[0, h, :, :QK_NOPE] = rkv[:, :QK_NOPE].astype(BF16)
        k_ref[0, h, :, QK_NOPE:] = kpe
        v_ref[0, h, :, :V_HEAD] = rkv[:, QK_NOPE:].astype(BF16)
        v_ref[0, h, :, V_HEAD:] = ones


def _mla_prep(small, cs, q_norm, kv_norm, wq, wkv):
    b, s, _ = small.shape
    ts = _tile(s, 512)
    hq = MLA_HEADS
    out_spec = pl.BlockSpec((1, hq, ts, 256), lambda bi, si: (bi, 0, si, 0))
    out_shape = jax.ShapeDtypeStruct((b, hq, s, 256), BF16)
    return pl.pallas_call(
        _mla_prep_kernel,
        grid=(b, s // ts),
        in_specs=[
            pl.BlockSpec((1, ts, Q_LORA), lambda bi, si: (bi, si, 0)),
            pl.BlockSpec((1, ts, KV_LORA), lambda bi, si: (bi, si, 1)),
            pl.BlockSpec((1, ts, 128), lambda bi, si: (bi, si, SMALL_ROPE_BLOCK)),
            pl.BlockSpec((1, ts, 128), lambda bi, si: (bi, si, 0)),
            pl.BlockSpec((1, Q_LORA), lambda bi, si: (0, 0)),
            pl.BlockSpec((1, KV_LORA), lambda bi, si: (0, 0)),
            pl.BlockSpec((hq, Q_LORA, 256), lambda bi, si: (0, 0, 0)),
            pl.BlockSpec((hq, KV_LORA, 256), lambda bi, si: (0, 0, 0)),
        ],
        out_specs=[out_spec, out_spec, out_spec],
        out_shape=[out_shape, out_shape, out_shape],
        compiler_params=_params("parallel", "parallel"),
        name="mla_prep",
    )(small, small, small, cs, q_norm, kv_norm, wq, wkv)


def _attn_kernel(q_ref, k_ref, v_ref, o_ref, sa_s, sb_s, m_s, acc_s, *, tq, rc):
    i = pl.program_id(2)
    dv2 = acc_s.shape[-1]
    m_s[...] = jnp.full(m_s.shape, -jnp.inf, F32)
    acc_s[...] = jnp.zeros(acc_s.shape, F32)

    def scores(j):
        start = pl.multiple_of(j * tq, tq)
        return _dot_nt(q_ref[0, 0], k_ref[0, 0, pl.ds(start, tq), :])

    def softmax_pv(s_ref, c, j, nk, masked):
        rows = slice(c * rc, (c + 1) * rc)
        start = pl.multiple_of(j * tq, tq)
        s = s_ref[rows, :nk]
        if masked:
            row = lax.broadcasted_iota(jnp.int32, (rc, nk), 0) + c * rc
            col = lax.broadcasted_iota(jnp.int32, (rc, nk), 1)
            s = jnp.where(col <= row, s, -jnp.inf)
        m_prev = m_s[rows, :]
        m_new = jnp.maximum(m_prev, jnp.max(s, -1, keepdims=True))
        alpha = jnp.exp2(m_prev - m_new)
        p = jnp.exp2(s - jnp.concatenate([m_new] * (nk // 128), -1))
        acc_s[rows, :] = (jnp.concatenate([alpha] * (dv2 // 128), -1) * acc_s[rows, :]
                          + _dot(p.astype(BF16), v_ref[0, 0, pl.ds(start, nk), :]))
        m_s[rows, :] = m_new

    nch = tq // rc

    def step(cur_ref, nxt_ref, j):
        nxt_ref[...] = scores(j + 1)
        for c in range(nch):
            softmax_pv(cur_ref, c, j, tq, False)

    def diagonal(cur_ref):
        for c in range(nch):
            softmax_pv(cur_ref, c, i, (c + 1) * rc, True)
        dv = dv2 // 2
        o_ref[0] = (acc_s[:, :dv] / acc_s[:, dv:]).astype(o_ref.dtype)

    sa_s[...] = scores(0)

    def pair(jj, carry):
        step(sa_s, sb_s, 2 * jj)
        step(sb_s, sa_s, 2 * jj + 1)
        return carry

    lax.fori_loop(0, i // 2, pair, 0)

    @pl.when(i % 2 == 0)
    def _():
        diagonal(sa_s)

    @pl.when(i % 2 == 1)
    def _():
        step(sa_s, sb_s, i - 1)
        diagonal(sb_s)


def _attention(q, k, v):
    b, h, s, dqk = q.shape
    dv2 = v.shape[-1]
    tq = _tile(s, 1024)
    rc = _tile(tq, 256)
    kern = functools.partial(_attn_kernel, tq=tq, rc=rc)
    return pl.pallas_call(
        kern,
        grid=(b, h, s // tq),
        in_specs=[
            pl.BlockSpec((1, 1, tq, dqk), lambda bi, hi, i: (bi, hi, i, 0)),
            pl.BlockSpec((1, 1, s, dqk), lambda bi, hi, i: (bi, hi, 0, 0)),
            pl.BlockSpec((1, 1, s, dv2), lambda bi, hi, i: (bi, hi, 0, 0)),
        ],
        out_specs=pl.BlockSpec((1, tq, dv2 // 2), lambda bi, hi, i: (bi, i, hi)),
        out_shape=jax.ShapeDtypeStruct((b, s, h * dv2 // 2), BF16),
        scratch_shapes=[pltpu.VMEM((tq, tq), F32), pltpu.VMEM((tq, tq), F32), pltpu.VMEM((tq, 128), F32),
                        pltpu.VMEM((tq, dv2), F32)],
        compiler_params=_params("parallel", "parallel", "arbitrary"),
        name="mla_attention",
    )(q, k, v)


def _gdn_kernel(q_ref, k_ref, v_ref, z_ref, gate_ref, gatet_ref, arow_ref, dtrow_ref,
                acol_ref, dtcol_ref, onorm_ref, o_ref, state_s, *, hb, rb):
    hg = pl.program_id(1)
    r = pl.program_id(2)

    @pl.when(r == 0)
    def _():
        state_s[...] = jnp.zeros(state_s.shape, F32)

    qkv_refs = (q_ref, k_ref, v_ref)

    def head_rows(ti, c, h):
        return qkv_refs[ti][0, c * CHUNK:(c + 1) * CHUNK, h * 128:(h + 1) * 128]

    blk = gate_ref[0]
    beta_tm = _sigmoid(blk)
    g_tm = -jnp.exp(arow_ref[...]) * _softplus(blk + dtrow_ref[...])
    g_t = -jnp.exp(acol_ref[...]) * _softplus(gatet_ref[0] + dtcol_ref[...])

    ri = lax.broadcasted_iota(jnp.int32, (CHUNK, CHUNK), 0)
    ci = lax.broadcasted_iota(jnp.int32, (CHUNK, CHUNK), 1)
    causal = ri >= ci
    strict = ri > ci
    tri = jnp.where(causal, 1.0, 0.0).astype(F32)
    tri_t = jnp.where(ri <= ci, 1.0, 0.0).astype(F32)
    eye = jnp.where(ri == ci, 1.0, 0.0).astype(F32)
    onorm = onorm_ref[...]

    nc = rb // CHUNK
    items = [(c, h) for c in range(nc) for h in range(hb)]
    rows = [slice(c * CHUNK, (c + 1) * CHUNK) for c in range(nc)]
    gc_tm = [jnp.dot(tri, g_tm[rows[c], :], precision=lax.Precision.HIGHEST, preferred_element_type=F32)
             for c in range(nc)]
    gc_t = [jnp.dot(g_t[:, rows[c]], tri_t, precision=lax.Precision.HIGHEST, preferred_element_type=F32)
            for c in range(nc)]
    sel_l = lax.broadcasted_iota(jnp.int32, (CHUNK, 128), 1)
    sel_s = lax.broadcasted_iota(jnp.int32, (32, CHUNK), 0)
    gcol, bcol, grow, glast, egc, kt, kb = {}, {}, {}, {}, {}, {}, {}
    for it in items:
        c, h = it
        lane = GDN_HEADS + hg * hb + h
        gcol[it] = jnp.sum(jnp.where(sel_l == lane, gc_tm[c], 0.0), -1, keepdims=True)
        bcol[it] = jnp.sum(jnp.where(sel_l == hg * hb + h, beta_tm[rows[c], :], 0.0), -1, keepdims=True)
        grow[it] = jnp.sum(jnp.where(sel_s == lane, gc_t[c], 0.0), 0, keepdims=True)
        glast[it] = gcol[it][CHUNK - 1:CHUNK, :]
        egc[it] = jnp.exp(gcol[it])
        k = head_rows(1, c, h)
        kt[it] = k.T
        kb[it] = k * bcol[it]
    kq = {it: _dot(jnp.concatenate([kb[it], head_rows(0, *it)], 0).astype(BF16),
                   kt[it].astype(BF16)) for it in items}
    amat, m, inv = {}, {}, {}
    for it in items:
        decay = jnp.exp(jnp.where(causal, gcol[it] - grow[it], -jnp.inf))
        amat[it] = (kq[it][CHUNK:] * decay).astype(BF16)
        m[it] = -jnp.where(strict, kq[it][:CHUNK] * decay, 0.0)
        inv[it] = eye + m[it]
    for it in items:
        mb = m[it].astype(BF16)
        m[it] = _dot(mb, mb)
    n_fac = int(round(math.log2(CHUNK))) - 1
    for t in range(n_fac):
        for it in items:
            mb = m[it].astype(BF16)
            if t < n_fac - 1:
                r2 = _dot(jnp.concatenate([inv[it], m[it]], 0).astype(BF16), mb)
                inv[it] = inv[it] + r2[:CHUNK]
                m[it] = r2[CHUNK:]
            else:
                inv[it] = inv[it] + _dot(inv[it].astype(BF16), mb)
    uw = {}
    for it in items:
        c, h = it
        vb = head_rows(2, c, h) * bcol[it]
        uw[it] = _dot(inv[it].astype(BF16), jnp.concatenate([vb, kb[it] * egc[it]], 1).astype(BF16))
    for c in range(nc):
        its = [(c, h) for h in range(hb)]
        st = {it: state_s[it[1]] for it in its}
        wq = {it: _dot(jnp.concatenate([uw[it][:, GDN_DV:], head_rows(0, *it) * egc[it]], 0).astype(BF16),
                       st[it].astype(BF16)) for it in its}
        vnb = {it: (uw[it][:, :GDN_DV] - wq[it][:CHUNK]).astype(BF16) for it in its}
        o = {it: wq[it][CHUNK:] + _dot(amat[it], vnb[it]) for it in its}
        for it in its:
            h = it[1]
            kdt = kt[it] * jnp.exp(glast[it] - grow[it])
            state_s[h] = st[it] * jnp.exp(glast[it]) + _dot(kdt.astype(BF16), vnb[it])
        for it in its:
            h = it[1]
            on = o[it] * lax.rsqrt(jnp.mean(o[it] * o[it], -1, keepdims=True) + RMS_EPS) * onorm
            zz = z_ref[0, rows[c], h * GDN_DV:(h + 1) * GDN_DV]
            o_ref[0, rows[c], h * GDN_DV:(h + 1) * GDN_DV] = (on * (zz * _sigmoid(zz))).astype(o_ref.dtype)


def _gdn(main, small, small_t, a_log, dt_bias, o_norm):
    b, s, _ = main.shape
    hb = GDN_HEAD_BLOCK
    rb = _tile(s, GDN_ROW_BLOCK)
    ng = GDN_HEADS // hb
    zoff = 3 * GDN_W // (hb * 128)
    pad = lambda a, lo: jnp.zeros((128,), F32).at[lo:lo + GDN_HEADS].set(a.astype(F32))
    arow = pad(a_log, GDN_HEADS).reshape(1, 128)
    dtrow = pad(dt_bias, GDN_HEADS).reshape(1, 128)
    acol = arow[0, :32].reshape(32, 1)
    dtcol = dtrow[0, :32].reshape(32, 1)
    kern = functools.partial(_gdn_kernel, hb=hb, rb=rb)
    return pl.pallas_call(
        kern,
        grid=(b, ng, s // rb),
        in_specs=[
            pl.BlockSpec((1, rb, hb * 128), lambda bi, g, r: (bi, r, g)),
            pl.BlockSpec((1, rb, hb * 128), lambda bi, g, r: (bi, r, ng + g)),
            pl.BlockSpec((1, rb, hb * 128), lambda bi, g, r: (bi, r, 2 * ng + g)),
            pl.BlockSpec((1, rb, hb * 128), lambda bi, g, r: (bi, r, zoff + g)),
            pl.BlockSpec((1, rb, 128), lambda bi, g, r: (bi, r, SMALL_GATE_BLOCK)),
            pl.BlockSpec((1, 32, rb), lambda bi, g, r: (bi, 0, r)),
            pl.BlockSpec((1, 128), lambda bi, g, r: (0, 0)),
            pl.BlockSpec((1, 128), lambda bi, g, r: (0, 0)),
            pl.BlockSpec((32, 1), lambda bi, g, r: (0, 0)),
            pl.BlockSpec((32, 1), lambda bi, g, r: (0, 0)),
            pl.BlockSpec((1, 128), lambda bi, g, r: (0, 0)),
        ],
        out_specs=pl.BlockSpec((1, rb, hb * 128), lambda bi, g, r: (bi, r, g)),
        out_shape=jax.ShapeDtypeStruct((b, s, GDN_W), BF16),
        scratch_shapes=[pltpu.VMEM((hb, GDN_DK, GDN_DV), F32)],
        compiler_params=_params("parallel", "parallel", "arbitrary"),
        name="gdn_delta_rule",
    )(main, main, main, main, small, small_t, arow, dtrow, acol, dtcol, o_norm.reshape(1, GDN_DV).astype(F32))


def _merge_kernel(om_ref, og_ref, gm_ref, gg_ref, wm_ref, wg_ref, o_ref):
    ym = _dot(om_ref[...], wm_ref[...])
    yg = _dot(og_ref[...], wg_ref[...])
    o_ref[...] = (_sigmoid(gm_ref[...]) * ym + _sigmoid(gg_ref[...]) * yg).astype(o_ref.dtype)


def _merge(o_mla, o_gdn, main, w_mla, w_gdn):
    t, d = o_mla.shape
    tm, tn = _tile(t, 512), _tile(d, 1024)
    goff = (4 * GDN_W) // tn
    nn = d // tn
    return pl.pallas_call(
        _merge_kernel,
        grid=(nn, t // tm),
        in_specs=[
            pl.BlockSpec((tm, d), lambda j, i: (i, 0)),
            pl.BlockSpec((tm, d), lambda j, i: (i, 0)),
            pl.BlockSpec((tm, tn), lambda j, i: (i, goff + j)),
            pl.BlockSpec((tm, tn), lambda j, i: (i, goff + nn + j)),
            pl.BlockSpec((d, tn), lambda j, i: (0, j)),
            pl.BlockSpec((d, tn), lambda j, i: (0, j)),
        ],
        out_specs=pl.BlockSpec((tm, tn), lambda j, i: (i, j)),
        out_shape=jax.ShapeDtypeStruct((t, d), BF16),
        compiler_params=_params("parallel", "parallel"),
        name="mixer_merge",
    )(o_mla, o_gdn, main, main, w_mla, w_gdn)


def _proj_ln_kernel(a_ref, w_ref, x_ref, g_ref, b_ref, o_ref, obf_ref):
    y = DN_ALPHA * x_ref[...] + _dot(a_ref[...], w_ref[...])
    out = _layer_norm(y, g_ref[...], b_ref[...])
    o_ref[...] = out
    obf_ref[...] = out.astype(BF16)


def _proj_ln(a, w, x, g, bias):
    t, k = a.shape
    d = w.shape[1]
    tm = _tile(t, 512)
    return pl.pallas_call(
        _proj_ln_kernel,
        grid=(t // tm,),
        in_specs=[
            pl.BlockSpec((tm, k), lambda i: (i, 0)),
            pl.BlockSpec((k, d), lambda i: (0, 0)),
            pl.BlockSpec((tm, d), lambda i: (i, 0)),
            pl.BlockSpec((1, d), lambda i: (0, 0)),
            pl.BlockSpec((1, d), lambda i: (0, 0)),
        ],
        out_specs=[pl.BlockSpec((tm, d), lambda i: (i, 0)), pl.BlockSpec((tm, d), lambda i: (i, 0))],
        out_shape=[jax.ShapeDtypeStruct((t, d), F32), jax.ShapeDtypeStruct((t, d), BF16)],
        compiler_params=_params("parallel"),
        name="proj_residual_ln",
    )(a, w, x, g.reshape(1, d), bias.reshape(1, d))


def _xattn_kernel(xbf_ref, x_ref, wq_ref, kv_ref, wo_ref, g_ref, b_ref, o_ref, opk_ref):
    q = _dot(xbf_ref[0], wq_ref[...])
    scale = X_HEAD_DIM ** -0.5
    outs = []
    for h in range(X_HEADS):
        qh = q[:, h * X_HEAD_DIM:(h + 1) * X_HEAD_DIM].astype(BF16)
        kh = kv_ref[0, :, h * X_HEAD_DIM:(h + 1) * X_HEAD_DIM]
        vh = kv_ref[0, :, X_W + h * X_HEAD_DIM:X_W + (h + 1) * X_HEAD_DIM]
        s = _dot_nt(qh, kh) * scale
        e = jnp.exp(s - jnp.max(s, -1, keepdims=True))
        p = e / jnp.sum(e, -1, keepdims=True)
        outs.append(_dot(p.astype(BF16), vh))
    o = jnp.concatenate(outs, -1).astype(BF16)
    y = DN_ALPHA * x_ref[0] + _dot(o, wo_ref[...])
    out = _layer_norm(y, g_ref[...], b_ref[...])
    o_ref[0] = out
    opk_ref[0] = _pack_halves(out)


def _xattn(x_bf, x, wq, kv, wo, g, bias):
    b, s, d = x.shape
    tm = _tile(s, 512)
    ml = kv.shape[1]
    return pl.pallas_call(
        _xattn_kernel,
        grid=(b, s // tm),
        in_specs=[
            pl.BlockSpec((1, tm, d), lambda bi, i: (bi, i, 0)),
            pl.BlockSpec((1, tm, d), lambda bi, i: (bi, i, 0)),
            pl.BlockSpec((d, X_W), lambda bi, i: (0, 0)),
            pl.BlockSpec((1, ml, 2 * X_W), lambda bi, i: (bi, 0, 0)),
            pl.BlockSpec((X_W, d), lambda bi, i: (0, 0)),
            pl.BlockSpec((1, d), lambda bi, i: (0, 0)),
            pl.BlockSpec((1, d), lambda bi, i: (0, 0)),
        ],
        out_specs=[pl.BlockSpec((1, tm, d), lambda bi, i: (bi, i, 0)),
                   pl.BlockSpec((1, tm, d // 2), lambda bi, i: (bi, i, 0))],
        out_shape=[jax.ShapeDtypeStruct((b, s, d), F32), jax.ShapeDtypeStruct((b, s, d // 2), jnp.uint32)],
        compiler_params=_params("parallel", "parallel"),
        name="memory_cross_attention",
    )(x_bf, x, wq, kv, wo, g.reshape(1, d), bias.reshape(1, d))


def _first_argmax(vals, iota, n):
    m = jnp.max(vals, 0, keepdims=True)
    idx = jnp.min(jnp.where(vals == m, iota, n), 0, keepdims=True)
    return m, idx


def _router_kernel(x_ref, wr_ref, bias_ref, idx_ref, gate_ref, rank_ref, cnt_ref, carry_s, *, tm):
    i = pl.program_id(0)

    @pl.when(i == 0)
    def _():
        carry_s[...] = jnp.zeros(carry_s.shape, F32)

    logits = _dot_nt(wr_ref[...].astype(BF16), x_ref[...].astype(BF16))
    scores = 1.0 / (1.0 + jnp.exp(-logits))
    sel = scores + bias_ref[...]
    epg = EXPERTS_PER_GROUP
    io8 = lax.broadcasted_iota(jnp.int32, (epg, tm), 0)
    best_score = None
    for g in range(N_GROUPS):
        sg = sel[g * epg:(g + 1) * epg, :]
        m1, i1 = _first_argmax(sg, io8, epg)
        m2 = jnp.max(jnp.where(io8 == i1, -jnp.inf, sg), 0, keepdims=True)
        gs = m1 + m2
        if g == 0:
            best_score, best = gs, jnp.zeros((1, tm), jnp.int32)
            in_sel, in_sc = sg, scores[0:epg, :]
        else:
            better = gs > best_score
            best_score = jnp.where(better, gs, best_score)
            best = jnp.where(better, g, best)
            in_sel = jnp.where(better, sg, in_sel)
            in_sc = jnp.where(better, scores[g * epg:(g + 1) * epg, :], in_sc)
    _, l1 = _first_argmax(in_sel, io8, epg)
    _, l2 = _first_argmax(jnp.where(io8 == l1, -jnp.inf, in_sel), io8, epg)
    g1 = jnp.sum(jnp.where(io8 == l1, in_sc, 0.0), 0, keepdims=True)
    g2 = jnp.sum(jnp.where(io8 == l2, in_sc, 0.0), 0, keepdims=True)
    e1 = best * epg + l1
    e2 = best * epg + l2
    idx_ref[0:1, :] = e1
    idx_ref[1:2, :] = e2
    gsum = g1 + g2
    gate_ref[0:1, :] = g1 / gsum
    gate_ref[1:2, :] = g2 / gsum

    ioe = lax.broadcasted_iota(jnp.int32, (N_EXPERTS, tm), 0)
    oh1 = ioe == e1
    oh2 = ioe == e2
    onehot = jnp.where(oh1, 1.0, 0.0) + jnp.where(oh2, 1.0, 0.0)
    rr = lax.broadcasted_iota(jnp.int32, (tm, tm), 0)
    cc = lax.broadcasted_iota(jnp.int32, (tm, tm), 1)
    upper = jnp.where(rr < cc, 1.0, 0.0).astype(BF16)
    tot = carry_s[...] + _dot(onehot.astype(BF16), upper)
    rank_ref[0:1, :] = jnp.sum(jnp.where(oh1, tot, 0.0), 0, keepdims=True).astype(jnp.int32)
    rank_ref[1:2, :] = jnp.sum(jnp.where(oh2, tot, 0.0), 0, keepdims=True).astype(jnp.int32)
    new = carry_s[...] + jnp.sum(onehot, -1, keepdims=True)
    carry_s[...] = new
    cnt_ref[...] = jnp.broadcast_to(new, cnt_ref.shape)


def _router(x, w_router_t, bias):
    t, d = x.shape
    tm = _tile(t, 512)
    return pl.pallas_call(
        functools.partial(_router_kernel, tm=tm),
        grid=(t // tm,),
        in_specs=[pl.BlockSpec((tm, d), lambda i: (i, 0)),
                  pl.BlockSpec((N_EXPERTS, d), lambda i: (0, 0)),
                  pl.BlockSpec((N_EXPERTS, 1), lambda i: (0, 0))],
        out_specs=[pl.BlockSpec((TOP_K, tm), lambda i: (0, i)),
                   pl.BlockSpec((TOP_K, tm), lambda i: (0, i)),
                   pl.BlockSpec((TOP_K, tm), lambda i: (0, i)),
                   pl.BlockSpec((N_EXPERTS, 128), lambda i: (0, 0))],
        out_shape=[jax.ShapeDtypeStruct((TOP_K, t), jnp.int32),
                   jax.ShapeDtypeStruct((TOP_K, t), F32),
                   jax.ShapeDtypeStruct((TOP_K, t), jnp.int32),
                   jax.ShapeDtypeStruct((N_EXPERTS, 128), F32)],
        scratch_shapes=[pltpu.VMEM((N_EXPERTS, 1), F32)],
        compiler_params=_params("arbitrary"),
        name="moe_router",
    )(x, w_router_t, bias)


def _slot_kernel(idx_ref, rank_ref, pstart_ref, dest_ref, *, tm):
    ioe = lax.broadcasted_iota(jnp.int32, (N_EXPERTS, tm), 0)
    pstart = pstart_ref[...]
    for k in range(TOP_K):
        start = jnp.sum(jnp.where(ioe == idx_ref[k:k + 1, :], pstart, 0.0), 0, keepdims=True)
        dest_ref[k:k + 1, :] = start.astype(jnp.int32) + rank_ref[k:k + 1, :]


def _slots(idx, rank, pstart):
    t = idx.shape[1]
    tm = _tile(t, 2048)
    spec = pl.BlockSpec((TOP_K, tm), lambda i: (0, i))
    return pl.pallas_call(
        functools.partial(_slot_kernel, tm=tm),
        grid=(t // tm,),
        in_specs=[spec, spec, pl.BlockSpec((N_EXPERTS, 1), lambda i: (0, 0))],
        out_specs=spec,
        out_shape=jax.ShapeDtypeStruct((TOP_K, t), jnp.int32),
        compiler_params=_params("parallel"),
        name="moe_slots",
    )(idx, rank, pstart)


def _dispatch_kernel(dest_ref, x_ref, xs_in_ref, xs_ref, sem, *, tm, t_total):
    del xs_in_ref
    base = pl.program_id(0) * tm

    def issue(t, carry):
        for k in range(TOP_K):
            d = dest_ref[k * t_total + base + t]
            pltpu.make_async_copy(x_ref.at[pl.ds(t, 1), :], xs_ref.at[pl.ds(d, 1), :], sem).start()
        return carry

    lax.fori_loop(0, tm, issue, 0, unroll=DMA_ISSUE_UNROLL)

    def drain(t, carry):
        for k in range(TOP_K):
            pltpu.make_async_copy(x_ref.at[pl.ds(0, 1), :], xs_ref.at[pl.ds(0, 1), :], sem).wait()
        return carry

    lax.fori_loop(0, tm, drain, 0, unroll=DMA_ISSUE_UNROLL)


def _dispatch(dest_flat, x, n_slots):
    t, d = x.shape
    tm = _tile(t, 512)
    xs0 = jnp.zeros((n_slots, d), x.dtype)
    return pl.pallas_call(
        functools.partial(_dispatch_kernel, tm=tm, t_total=t),
        grid_spec=pltpu.PrefetchScalarGridSpec(
            num_scalar_prefetch=1,
            grid=(t // tm,),
            in_specs=[pl.BlockSpec((tm, d), lambda i, dest: (i, 0)),
                      pl.BlockSpec(memory_space=pl.ANY)],
            out_specs=pl.BlockSpec(memory_space=pl.ANY),
            scratch_shapes=[pltpu.SemaphoreType.DMA(())],
        ),
        out_shape=jax.ShapeDtypeStruct((n_slots, d), x.dtype),
        input_output_aliases={2: 0},
        compiler_params=_params("arbitrary"),
        name="moe_dispatch",
    )(dest_flat, x, xs0)


def _expert_kernel(be_ref, nu_ref, slot_ref, nx_ref, xs_ref, wg_hbm, wu_hbm, wd_hbm, ys_ref,
                   fg_s, fu_s, fd_s, wg_s, wu_s, wd_s, sems, *, layer):
    i = pl.program_id(0)
    e = be_ref[i]

    def weight_copies(expert, slot):
        return (pltpu.make_async_copy(wg_hbm.at[layer, expert], fg_s.at[slot], sems.at[slot, 0]),
                pltpu.make_async_copy(wu_hbm.at[layer, expert], fu_s.at[slot], sems.at[slot, 1]),
                pltpu.make_async_copy(wd_hbm.at[layer, expert], fd_s.at[slot], sems.at[slot, 2]))

    @pl.when(i == 0)
    def _():
        for cp in weight_copies(e, 0):
            cp.start()

    @pl.when((i == 0) | (e != be_ref[jnp.maximum(i - 1, 0)]))
    def _():
        slot = slot_ref[i]
        for cp in weight_copies(e, slot):
            cp.wait()
        wg_s[...] = fg_s[slot].astype(BF16)
        wu_s[...] = fu_s[slot].astype(BF16)
        wd_s[...] = fd_s[slot].astype(BF16)

        @pl.when(nx_ref[i] >= 0)
        def _():
            for cp in weight_copies(nx_ref[i], 1 - slot):
                cp.start()

    @pl.when(i < nu_ref[0])
    def _():
        lo, hi = _unpack_halves(xs_ref[...])
        xb = jnp.concatenate([lo.astype(BF16), hi.astype(BF16)], -1)
        hg = _dot(xb, wg_s[...])
        hu = _dot(xb, wu_s[...])
        h = (hg * _sigmoid(hg)) * hu
        ys_ref[...] = _pack_halves(_dot(h.astype(BF16), wd_s[...]))

    @pl.when(i >= nu_ref[0])
    def _():
        ys_ref[...] = jnp.zeros(ys_ref.shape, ys_ref.dtype)


def _experts(block_expert, n_used, xs, wg, wu, wd, layer):
    n_slots, dp = xs.shape
    d = 2 * dp
    blk = MOE_BLOCK
    n_blocks = n_slots // blk
    de = wg.shape[-1]
    pos = jnp.arange(n_blocks, dtype=jnp.int32)
    first = jnp.concatenate([jnp.ones((1,), bool), block_expert[1:] != block_expert[:-1]])
    slot = ((jnp.cumsum(first.astype(jnp.int32)) - 1) % 2).astype(jnp.int32)
    first_pos = jnp.where(first, pos, n_blocks)
    next_first = lax.cummin(jnp.concatenate([first_pos[1:], jnp.full((1,), n_blocks, jnp.int32)]), reverse=True)
    next_expert = jnp.where(next_first < n_blocks, block_expert[jnp.minimum(next_first, n_blocks - 1)], -1)
    hbm = pl.BlockSpec(memory_space=pl.ANY)
    return pl.pallas_call(
        functools.partial(_expert_kernel, layer=layer),
        grid_spec=pltpu.PrefetchScalarGridSpec(
            num_scalar_prefetch=4,
            grid=(n_blocks,),
            in_specs=[
                pl.BlockSpec((blk, dp), lambda i, be, nu, sl, nx: (jnp.minimum(i, nu[0] - 1), 0)),
                hbm, hbm, hbm,
            ],
            out_specs=pl.BlockSpec((blk, dp), lambda i, be, nu, sl, nx: (i, 0)),
            scratch_shapes=[pltpu.VMEM((2, d, de), F32), pltpu.VMEM((2, d, de), F32), pltpu.VMEM((2, de, d), F32),
                            pltpu.VMEM((d, de), BF16), pltpu.VMEM((d, de), BF16), pltpu.VMEM((de, d), BF16),
                            pltpu.SemaphoreType.DMA((2, 3))],
        ),
        out_shape=jax.ShapeDtypeStruct((n_slots, dp), jnp.uint32),
        compiler_params=_params("arbitrary"),
        name="moe_experts",
    )(block_expert, n_used, slot, next_expert.astype(jnp.int32), xs, wg, wu, wd)


def _combine_kernel(dest_ref, ys_ref, x_ref, gate_ref, g_ref, b_ref, o_ref, buf_s, sem, *, tm, t_total):
    base = pl.program_id(0) * tm

    def issue(t, carry):
        for k in range(TOP_K):
            d = dest_ref[k * t_total + base + t]
            pltpu.make_async_copy(ys_ref.at[pl.ds(d, 1), :], buf_s.at[k, pl.ds(t, 1), :], sem).start()
        return carry

    lax.fori_loop(0, tm, issue, 0, unroll=DMA_ISSUE_UNROLL)

    def drain(t, carry):
        for k in range(TOP_K):
            pltpu.make_async_copy(ys_ref.at[pl.ds(0, 1), :], buf_s.at[0, pl.ds(0, 1), :], sem).wait()
        return carry

    lax.fori_loop(0, tm, drain, 0, unroll=DMA_ISSUE_UNROLL)
    gate = gate_ref[...]
    lo0, hi0 = _unpack_halves(buf_s[0])
    lo1, hi1 = _unpack_halves(buf_s[1])
    g0, g1 = gate[:, 0:1], gate[:, 1:2]
    y = jnp.concatenate([g0 * lo0 + g1 * lo1, g0 * hi0 + g1 * hi1], -1)
    o_ref[...] = _layer_norm(DN_ALPHA * x_ref[...] + y, g_ref[...], b_ref[...])


def _combine(dest_flat, ys, x, gate_tk, g, bias):
    t, d = x.shape
    tm = _tile(t, 512)
    return pl.pallas_call(
        functools.partial(_combine_kernel, tm=tm, t_total=t),
        grid_spec=pltpu.PrefetchScalarGridSpec(
            num_scalar_prefetch=1,
            grid=(t // tm,),
            in_specs=[
                pl.BlockSpec(memory_space=pl.ANY),
                pl.BlockSpec((tm, d), lambda i, dest: (i, 0)),
                pl.BlockSpec((tm, TOP_K), lambda i, dest: (i, 0)),
                pl.BlockSpec((1, d), lambda i, dest: (0, 0)),
                pl.BlockSpec((1, d), lambda i, dest: (0, 0)),
            ],
            out_specs=pl.BlockSpec((tm, d), lambda i, dest: (i, 0)),
            scratch_shapes=[pltpu.VMEM((TOP_K, tm, d // 2), jnp.uint32), pltpu.SemaphoreType.DMA(())],
        ),
        out_shape=jax.ShapeDtypeStruct((t, d), F32),
        compiler_params=_params("arbitrary"),
        name="moe_combine_ln",
    )(dest_flat, ys, x, gate_tk, g.reshape(1, d), bias.reshape(1, d))


def _moe(x, x_packed, w_router_t, router_bias, wg, wu, wd, layer, g, bias):
    t, d = x.shape
    idx, gate, rank, cnt = _router(x, w_router_t, router_bias)
    blk = MOE_BLOCK
    n_blocks = t * TOP_K // blk + N_EXPERTS
    counts = cnt[:, 0].astype(jnp.int32)
    padded = (counts + blk - 1) // blk * blk
    padded_end = jnp.cumsum(padded)
    padded_start = padded_end - padded
    dest = _slots(idx, rank, padded_start.astype(F32).reshape(N_EXPERTS, 1)).reshape(-1)
    n_used = (padded_end[-1] // blk).astype(jnp.int32)
    blocks = jnp.minimum(jnp.arange(n_blocks, dtype=jnp.int32), n_used - 1)
    block_expert = jnp.minimum(jnp.searchsorted(padded_end, blocks * blk, side='right'),
                               N_EXPERTS - 1).astype(jnp.int32)
    xs = _dispatch(dest, x_packed, n_blocks * blk)
    ys = _experts(block_expert, n_used.reshape(1), xs, wg, wu, wd, layer)
    return _combine(dest, ys, x, gate.T, g, bias)


def _in_proj_weights(w_in):
    sizes = (Q_LORA, KV_LORA, QK_ROPE, GDN_W, GDN_W, GDN_W, GDN_W, GDN_HEADS, GDN_HEADS, D_MODEL, D_MODEL)
    pts = np.cumsum(sizes)[:-1].tolist()
    c_q, c_kv, k_rope, g_q, g_k, g_v, g_z, g_b, g_a, gate_mla, gate_gdn = jnp.split(w_in, pts, axis=-1)
    half = QK_ROPE // 2
    k_rope_sw = jnp.concatenate([k_rope[:, half:], k_rope[:, :half]], -1)
    main = jnp.concatenate([g_q, g_k, g_v, g_z, gate_mla, gate_gdn], -1)
    used = Q_LORA + KV_LORA + 2 * QK_ROPE + 2 * GDN_HEADS
    small = jnp.concatenate([c_q, c_kv, k_rope, k_rope_sw, g_b, g_a,
                             jnp.zeros((w_in.shape[0], SMALL_W - used), w_in.dtype)], -1)
    return main.astype(BF16), small.astype(BF16)


def _mla_weights(w_uq, w_ukv):
    half = QK_ROPE // 2
    wq = w_uq.reshape(Q_LORA, MLA_HEADS, QK_DIM).transpose(1, 0, 2)
    pe = wq[..., QK_NOPE:]
    pe_sw = jnp.concatenate([pe[..., half:], pe[..., :half]], -1)
    wq = jnp.concatenate([wq, pe_sw], -1).astype(BF16)
    wkv = w_ukv.reshape(KV_LORA, MLA_HEADS, QK_NOPE + V_HEAD).transpose(1, 0, 2).astype(BF16)
    return wq, wkv


def _rope_table(positions):
    half = QK_ROPE // 2
    inv_freq = jnp.power(ROPE_BASE, -jnp.arange(half, dtype=F32) / half)
    ang = positions.astype(F32)[..., None] * inv_freq
    cos, sin = jnp.cos(ang), jnp.sin(ang)
    return jnp.concatenate([cos, cos, -sin, sin], -1)


def kernel(x, mem, positions, w_in, mla_q_norm, mla_kv_norm, w_uq, w_ukv, gdn_conv, gdn_a_log, gdn_dt_bias, gdn_o_norm, w_mla_proj, w_gdn_proj, w_mix_out, ln_mix_g, ln_mix_b, w_xq, w_xkv, w_xo, ln_x_g, ln_x_b, w_router, router_bias, w_gate, w_up, w_down, ln_moe_g, ln_moe_b):
    b, s, d = x.shape
    t = b * s
    cs = _rope_table(positions)
    mem2 = mem.reshape(-1, d)
    w_router_t = w_router.T.astype(F32)
    rbias = router_bias.reshape(N_EXPERTS, 1).astype(F32)
    for l in range(w_in.shape[0]):
        w_main, w_small = _in_proj_weights(w_in[l])
        wq, wkv = _mla_weights(w_uq[l], w_ukv[l])
        xt = x.reshape(t, d)
        main = _in_proj_main(xt, w_main, gdn_conv[l], s).reshape(b, s, MAIN_W)
        small = _matmul(xt, w_small, F32, 1024, SMALL_W, "in_proj_small").reshape(b, s, SMALL_W)

        q, k, v = _mla_prep(small, cs, mla_q_norm[l].reshape(1, -1), mla_kv_norm[l].reshape(1, -1), wq, wkv)
        o_mla = _attention(q, k, v)

        small_t = small[:, :, SMALL_GATE_COL:SMALL_GATE_COL + 2 * GDN_HEADS].transpose(0, 2, 1)
        o_gdn = _gdn(main, small, small_t, gdn_a_log[l], gdn_dt_bias[l], gdn_o_norm[l])

        merged = _merge(o_mla.reshape(t, d), o_gdn.reshape(t, d), main.reshape(t, MAIN_W),
                        w_mla_proj[l].astype(BF16), w_gdn_proj[l].astype(BF16))
        x1, x1_bf = _proj_ln(merged, w_mix_out[l].astype(BF16), x.reshape(t, d), ln_mix_g[l], ln_mix_b[l])

        kv = _matmul(mem2, w_xkv[l].astype(BF16), BF16, 1024, 1024, "mem_kv").reshape(b, -1, 2 * X_W)
        x2, x2_pk = _xattn(x1_bf.reshape(b, s, d), x1.reshape(b, s, d), w_xq[l].astype(BF16), kv,
                           w_xo[l].astype(BF16), ln_x_g[l], ln_x_b[l])

        x3 = _moe(x2.reshape(t, d), x2_pk.reshape(t, d // 2), w_router_t, rbias, w_gate, w_up, w_down, l,
                  ln_moe_g[l], ln_moe_b[l])
        x = x3.reshape(b, s, d)
    return x
```

```python
import functools
import math

import jax
import jax.numpy as jnp
import numpy as np
from jax import lax
from jax.experimental import pallas as pl
from jax.experimental.pallas import tpu as pltpu

D_MODEL = 2048
DEPTH = 2
MLA_HEADS = 16
Q_LORA = 512
KV_LORA = 512
QK_NOPE = 128
QK_ROPE = 64
V_HEAD = 128
QK_DIM = QK_NOPE + QK_ROPE
ROPE_BASE = 10000.0
GDN_HEADS = 16
GDN_DK = 128
GDN_DV = 128
CONV_K = 4
CHUNK = 64
GDN_W = GDN_HEADS * GDN_DK
X_HEADS = 4
X_HEAD_DIM = 128
X_W = X_HEADS * X_HEAD_DIM
N_EXPERTS = 64
N_GROUPS = 8
EXPERTS_PER_GROUP = N_EXPERTS // N_GROUPS
TOP_K = 2
D_EXPERT = 512
Q_PRESCALE = QK_DIM ** -0.5 * math.log2(math.e)
DN_ALPHA = (2.0 * DEPTH) ** 0.25
LN_EPS = 1e-5
RMS_EPS = 1e-6

MAIN_W = 3 * GDN_W + GDN_W + 2 * D_MODEL
SMALL_W = 1280
SMALL_ROPE_BLOCK = (Q_LORA + KV_LORA) // 128
SMALL_GATE_COL = Q_LORA + KV_LORA + 2 * QK_ROPE
SMALL_GATE_BLOCK = SMALL_GATE_COL // 128

GDN_HEAD_BLOCK = 16
GDN_ROW_BLOCK = 128
MOE_BLOCK = 256
DMA_ISSUE_UNROLL = 8
VMEM_LIMIT = 56 * 1024 * 1024

BF16 = jnp.bfloat16
F32 = jnp.float32


def _tile(n, pref):
    return pref if n % pref == 0 else n


def _params(*sem):
    return pltpu.CompilerParams(dimension_semantics=sem, vmem_limit_bytes=VMEM_LIMIT)


def _sigmoid(x):
    return 0.5 * jnp.tanh(0.5 * x) + 0.5


def _pack_halves(y):
    n = y.shape[1] // 2
    lo = pltpu.bitcast(y[:, :n].astype(BF16).astype(F32), jnp.uint32)
    hi = pltpu.bitcast(y[:, n:].astype(BF16).astype(F32), jnp.uint32)
    return (lo >> 16) | (hi & jnp.uint32(0xFFFF0000))


def _unpack_halves(p):
    lo = pltpu.bitcast(p << 16, F32)
    hi = pltpu.bitcast(p & jnp.uint32(0xFFFF0000), F32)
    return lo, hi


def _softplus(x):
    return jnp.maximum(x, 0.0) + jnp.log1p(jnp.exp(-jnp.abs(x)))


def _layer_norm(y, g, b):
    mu = jnp.mean(y, -1, keepdims=True)
    d = y - mu
    var = jnp.mean(d * d, -1, keepdims=True)
    return d * lax.rsqrt(var + LN_EPS) * g + b


def _dot(a, b):
    return jnp.dot(a, b, preferred_element_type=F32)


def _dot_nt(a, b):
    return lax.dot_general(a, b, (((1,), (1,)), ((), ())), preferred_element_type=F32)


def _dot_tn(a, b):
    return lax.dot_general(a, b, (((0,), (0,)), ((), ())), preferred_element_type=F32)


def _mm_kernel(a_ref, w_ref, o_ref, a_s):
    @pl.when(pl.program_id(1) == 0)
    def _():
        a_s[...] = a_ref[...].astype(BF16)

    o_ref[...] = _dot(a_s[...], w_ref[...]).astype(o_ref.dtype)


def _matmul(a, w, out_dtype, tm, tn, name):
    m, k = a.shape
    n = w.shape[1]
    tm, tn = _tile(m, tm), _tile(n, tn)
    return pl.pallas_call(
        _mm_kernel,
        grid=(m // tm, n // tn),
        in_specs=[pl.BlockSpec((tm, k), lambda i, j: (i, 0)),
                  pl.BlockSpec((k, tn), lambda i, j: (0, j))],
        out_specs=pl.BlockSpec((tm, tn), lambda i, j: (i, j)),
        out_shape=jax.ShapeDtypeStruct((m, n), out_dtype),
        scratch_shapes=[pltpu.VMEM((tm, k), BF16)],
        compiler_params=_params("parallel", "arbitrary"),
        name=name,
    )(a, w)


def _mla_prep_kernel(cq_ref, ckv_ref, kr_ref, cs_ref, qn_ref, kvn_ref, wq_ref, wkv_ref, q_ref, k_ref, v_ref):
    ts = cq_ref.shape[1]
    cq = cq_ref[0]
    cqn = (cq * lax.rsqrt(jnp.mean(cq * cq, -1, keepdims=True) + RMS_EPS) * qn_ref[...]).astype(BF16)
    ckv = ckv_ref[0]
    ckvn = (ckv * lax.rsqrt(jnp.mean(ckv * ckv, -1, keepdims=True) + RMS_EPS) * kvn_ref[...]).astype(BF16)
    cs = cs_ref[0]
    lane = lax.broadcasted_iota(jnp.int32, (ts, 128), 1)

    def rotate(t):
        u = t * cs
        return jnp.where(lane < QK_ROPE, u + pltpu.roll(u, QK_ROPE, 1), 0.0).astype(BF16)

    kpe = rotate(kr_ref[0])
    ones = jnp.ones((ts, V_HEAD), BF16)
    for h in range(MLA_HEADS):
        rq = _dot(cqn, wq_ref[h]) * Q_PRESCALE
        q_ref[0, h, :, :QK_NOPE] = rq[:, :QK_NOPE].astype(BF16)
        q_ref[0, h, :, QK_NOPE:] = rotate(rq[:, QK_NOPE:])
        rkv = _dot(ckvn, wkv_ref[h])
        k_ref[0, h, :, :QK_NOPE] = rkv[:, :QK_NOPE].astype(BF16)
        k_ref[0, h, :, QK_NOPE:] = kpe
        v_ref[0, h, :, :V_HEAD] = rkv[:, QK_NOPE:].astype(BF16)
        v_ref[0, h, :, V_HEAD:] = ones


def _mla_prep(small, cs, q_norm, kv_norm, wq, wkv):
    b, s, _ = small.shape
    ts = _tile(s, 512)
    hq = MLA_HEADS
    out_spec = pl.BlockSpec((1, hq, ts, 256), lambda bi, si: (bi, 0, si, 0))
    out_shape = jax.ShapeDtypeStruct((b, hq, s, 256), BF16)
    return pl.pallas_call(
        _mla_prep_kernel,
        grid=(b, s // ts),
        in_specs=[
            pl.BlockSpec((1, ts, Q_LORA), lambda bi, si: (bi, si, 0)),
            pl.BlockSpec((1, ts, KV_LORA), lambda bi, si: (bi, si, 1)),
            pl.BlockSpec((1, ts, 128), lambda bi, si: (bi, si, SMALL_ROPE_BLOCK)),
            pl.BlockSpec((1, ts, 128), lambda bi, si: (bi, si, 0)),
            pl.BlockSpec((1, Q_LORA), lambda bi, si: (0, 0)),
            pl.BlockSpec((1, KV_LORA), lambda bi, si: (0, 0)),
            pl.BlockSpec((hq, Q_LORA, 256), lambda bi, si: (0, 0, 0)),
            pl.BlockSpec((hq, KV_LORA, 256), lambda bi, si: (0, 0, 0)),
        ],
        out_specs=[out_spec, out_spec, out_spec],
        out_shape=[out_shape, out_shape, out_shape],
        compiler_params=_params("parallel", "parallel"),
        name="mla_prep",
    )(small, small, small, cs, q_norm, kv_norm, wq, wkv)


def _attn_kernel(q_ref, k_ref, v_ref, o_ref, sa_s, sb_s, m_s, acc_s, *, tq, rc):
    i = pl.program_id(2)
    dv2 = acc_s.shape[-1]
    m_s[...] = jnp.full(m_s.shape, -jnp.inf, F32)
    acc_s[...] = jnp.zeros(acc_s.shape, F32)

    def scores(j):
        start = pl.multiple_of(j * tq, tq)
        return _dot_nt(q_ref[0, 0], k_ref[0, 0, pl.ds(start, tq), :])

    def softmax_pv(s_ref, c, j, nk, masked):
        rows = slice(c * rc, (c + 1) * rc)
        start = pl.multiple_of(j * tq, tq)
        s = s_ref[rows, :nk]
        if masked:
            row = lax.broadcasted_iota(jnp.int32, (rc, nk), 0) + c * rc
            col = lax.broadcasted_iota(jnp.int32, (rc, nk), 1)
            s = jnp.where(col <= row, s, -jnp.inf)
        m_prev = m_s[rows, :]
        m_new = jnp.maximum(m_prev, jnp.max(s, -1, keepdims=True))
        alpha = jnp.exp2(m_prev - m_new)
        p = jnp.exp2(s - jnp.concatenate([m_new] * (nk // 128), -1))
        acc_s[rows, :] = (jnp.concatenate([alpha] * (dv2 // 128), -1) * acc_s[rows, :]
                          + _dot(p.astype(BF16), v_ref[0, 0, pl.ds(start, nk), :]))
        m_s[rows, :] = m_new

    nch = tq // rc

    def step(cur_ref, nxt_ref, j):
        nxt_ref[...] = scores(j + 1)
        for c in range(nch):
            softmax_pv(cur_ref, c, j, tq, False)

    def diagonal(cur_ref):
        for c in range(nch):
            softmax_pv(cur_ref, c, i, (c + 1) * rc, True)
        dv = dv2 // 2
        o_ref[0] = (acc_s[:, :dv] / acc_s[:, dv:]).astype(o_ref.dtype)

    sa_s[...] = scores(0)

    def pair(jj, carry):
        step(sa_s, sb_s, 2 * jj)
        step(sb_s, sa_s, 2 * jj + 1)
        return carry

    lax.fori_loop(0, i // 2, pair, 0)

    @pl.when(i % 2 == 0)
    def _():
        diagonal(sa_s)

    @pl.when(i % 2 == 1)
    def _():
        step(sa_s, sb_s, i - 1)
        diagonal(sb_s)


def _attention(q, k, v):
    b, h, s, dqk = q.shape
    dv2 = v.shape[-1]
    tq = _tile(s, 1024)
    rc = _tile(tq, 256)
    kern = functools.partial(_attn_kernel, tq=tq, rc=rc)
    return pl.pallas_call(
        kern,
        grid=(b, h, s // tq),
        in_specs=[
            pl.BlockSpec((1, 1, tq, dqk), lambda bi, hi, i: (bi, hi, i, 0)),
            pl.BlockSpec((1, 1, s, dqk), lambda bi, hi, i: (bi, hi, 0, 0)),
            pl.BlockSpec((1, 1, s, dv2), lambda bi, hi, i: (bi, hi, 0, 0)),
        ],
        out_specs=pl.BlockSpec((1, tq, dv2 // 2), lambda bi, hi, i: (bi, i, hi)),
        out_shape=jax.ShapeDtypeStruct((b, s, h * dv2 // 2), BF16),
        scratch_shapes=[pltpu.VMEM((tq, tq), F32), pltpu.VMEM((tq, tq), F32), pltpu.VMEM((tq, 128), F32),
                        pltpu.VMEM((tq, dv2), F32)],
        compiler_params=_params("parallel", "parallel", "arbitrary"),
        name="mla_attention",
    )(q, k, v)


def _gdn_kernel(qr_ref, kr_ref, vr_ref, wq_ref, wk_ref, wv_ref, z_ref, gate_ref, gatet_ref, arow_ref, dtrow_ref,
                acol_ref, dtcol_ref, onorm_ref, o_ref, state_s, xx_s, qkv_s, *, hb, rb):
    hg = pl.program_id(1)
    r = pl.program_id(2)

    @pl.when(r == 0)
    def _():
        state_s[...] = jnp.zeros(state_s.shape, F32)
        xx_s[:, 0:8, :] = jnp.zeros((3, 8, hb * 128), F32)

    @pl.when(r > 0)
    def _():
        xx_s[:, 0:8, :] = xx_s[:, rb:rb + 8, :]

    for ti, (x_ref, w_ref) in enumerate(((qr_ref, wq_ref), (kr_ref, wk_ref), (vr_ref, wv_ref))):
        xx_s[ti, 8:rb + 8, :] = x_ref[0]
        wc = w_ref[...]
        y = wc[CONV_K - 1:CONV_K, :] * xx_s[ti, 8:rb + 8, :]
        for j in range(CONV_K - 1):
            off = 8 - (CONV_K - 1) + j
            y = y + wc[j:j + 1, :] * xx_s[ti, off:off + rb, :]
        y = y * _sigmoid(y)
        if ti == 2:
            qkv_s[ti] = y
        else:
            for h in range(hb):
                yh = y[:, h * 128:(h + 1) * 128]
                yn = yh * lax.rsqrt(jnp.sum(yh * yh, -1, keepdims=True) + RMS_EPS)
                qkv_s[ti, :, h * 128:(h + 1) * 128] = yn * (GDN_DK ** -0.5) if ti == 0 else yn

    def head_rows(ti, c, h):
        return qkv_s[ti, c * CHUNK:(c + 1) * CHUNK, h * 128:(h + 1) * 128]

    blk = gate_ref[0]
    beta_tm = _sigmoid(blk)
    g_tm = -jnp.exp(arow_ref[...]) * _softplus(blk + dtrow_ref[...])
    g_t = -jnp.exp(acol_ref[...]) * _softplus(gatet_ref[0] + dtcol_ref[...])

    ri = lax.broadcasted_iota(jnp.int32, (CHUNK, CHUNK), 0)
    ci = lax.broadcasted_iota(jnp.int32, (CHUNK, CHUNK), 1)
    causal = ri >= ci
    strict = ri > ci
    tri = jnp.where(causal, 1.0, 0.0).astype(F32)
    tri_t = jnp.where(ri <= ci, 1.0, 0.0).astype(F32)
    eye = jnp.where(ri == ci, 1.0, 0.0).astype(F32)
    onorm = onorm_ref[...]

    nc = rb // CHUNK
    items = [(c, h) for c in range(nc) for h in range(hb)]
    rows = [slice(c * CHUNK, (c + 1) * CHUNK) for c in range(nc)]
    gc_tm = [jnp.dot(tri, g_tm[rows[c], :], precision=lax.Precision.HIGHEST, preferred_element_type=F32)
             for c in range(nc)]
    gc_t = [jnp.dot(g_t[:, rows[c]], tri_t, precision=lax.Precision.HIGHEST, preferred_element_type=F32)
            for c in range(nc)]
    sel_l = lax.broadcasted_iota(jnp.int32, (CHUNK, 128), 1)
    sel_s = lax.broadcasted_iota(jnp.int32, (32, CHUNK), 0)
    gcol, bcol, grow, glast, egc, kt, kb = {}, {}, {}, {}, {}, {}, {}
    for it in items:
        c, h = it
        lane = GDN_HEADS + hg * hb + h
        gcol[it] = jnp.sum(jnp.where(sel_l == lane, gc_tm[c], 0.0), -1, keepdims=True)
        bcol[it] = jnp.sum(jnp.where(sel_l == hg * hb + h, beta_tm[rows[c], :], 0.0), -1, keepdims=True)
        grow[it] = jnp.sum(jnp.where(sel_s == lane, gc_t[c], 0.0), 0, keepdims=True)
        glast[it] = gcol[it][CHUNK - 1:CHUNK, :]
        egc[it] = jnp.exp(gcol[it])
        k = head_rows(1, c, h)
        kt[it] = k.T
        kb[it] = k * bcol[it]
    kq = {it: _dot(jnp.concatenate([kb[it], head_rows(0, *it)], 0).astype(BF16),
                   kt[it].astype(BF16)) for it in items}
    amat, m, inv = {}, {}, {}
    for it in items:
        decay = jnp.exp(jnp.where(causal, gcol[it] - grow[it], -jnp.inf))
        amat[it] = (kq[it][CHUNK:] * decay).astype(BF16)
        m[it] = -jnp.where(strict, kq[it][:CHUNK] * decay, 0.0)
        inv[it] = eye + m[it]
    for it in items:
        mb = m[it].astype(BF16)
        m[it] = _dot(mb, mb)
    n_fac = int(round(math.log2(CHUNK))) - 1
    for t in range(n_fac):
        for it in items:
            mb = m[it].astype(BF16)
            if t < n_fac - 1:
                r2 = _dot(jnp.concatenate([inv[it], m[it]], 0).astype(BF16), mb)
                inv[it] = inv[it] + r2[:CHUNK]
                m[it] = r2[CHUNK:]
            else:
                inv[it] = inv[it] + _dot(inv[it].astype(BF16), mb)
    uw = {}
    for it in items:
        c, h = it
        vb = head_rows(2, c, h) * bcol[it]
        uw[it] = _dot(inv[it].astype(BF16), jnp.concatenate([vb, kb[it] * egc[it]], 1).astype(BF16))
    for c in range(nc):
        its = [(c, h) for h in range(hb)]
        st = {it: state_s[it[1]] for it in its}
        wq = {it: _dot(jnp.concatenate([uw[it][:, GDN_DV:], head_rows(0, *it) * egc[it]], 0).astype(BF16),
                       st[it].astype(BF16)) for it in its}
        vnb = {it: (uw[it][:, :GDN_DV] - wq[it][:CHUNK]).astype(BF16) for it in its}
        o = {it: wq[it][CHUNK:] + _dot(amat[it], vnb[it]) for it in its}
        for it in its:
            h = it[1]
            kdt = kt[it] * jnp.exp(glast[it] - grow[it])
            state_s[h] = st[it] * jnp.exp(glast[it]) + _dot(kdt.astype(BF16), vnb[it])
        for it in its:
            h = it[1]
            on = o[it] * lax.rsqrt(jnp.mean(o[it] * o[it], -1, keepdims=True) + RMS_EPS) * onorm
            zz = z_ref[0, rows[c], h * GDN_DV:(h + 1) * GDN_DV]
            o_ref[0, rows[c], h * GDN_DV:(h + 1) * GDN_DV] = (on * (zz * _sigmoid(zz))).astype(o_ref.dtype)


def _gdn(main, w_conv, small, small_t, a_log, dt_bias, o_norm):
    b, s, _ = main.shape
    hb = GDN_HEAD_BLOCK
    rb = _tile(s, GDN_ROW_BLOCK)
    ng = GDN_HEADS // hb
    zoff = 3 * GDN_W // (hb * 128)
    pad = lambda a, lo: jnp.zeros((128,), F32).at[lo:lo + GDN_HEADS].set(a.astype(F32))
    arow = pad(a_log, GDN_HEADS).reshape(1, 128)
    dtrow = pad(dt_bias, GDN_HEADS).reshape(1, 128)
    acol = arow[0, :32].reshape(32, 1)
    dtcol = dtrow[0, :32].reshape(32, 1)
    kern = functools.partial(_gdn_kernel, hb=hb, rb=rb)
    return pl.pallas_call(
        kern,
        grid=(b, ng, s // rb),
        in_specs=[
            pl.BlockSpec((1, rb, hb * 128), lambda bi, g, r: (bi, r, g)),
            pl.BlockSpec((1, rb, hb * 128), lambda bi, g, r: (bi, r, ng + g)),
            pl.BlockSpec((1, rb, hb * 128), lambda bi, g, r: (bi, r, 2 * ng + g)),
            pl.BlockSpec((CONV_K, hb * 128), lambda bi, g, r: (0, g)),
            pl.BlockSpec((CONV_K, hb * 128), lambda bi, g, r: (0, ng + g)),
            pl.BlockSpec((CONV_K, hb * 128), lambda bi, g, r: (0, 2 * ng + g)),
            pl.BlockSpec((1, rb, hb * 128), lambda bi, g, r: (bi, r, zoff + g)),
            pl.BlockSpec((1, rb, 128), lambda bi, g, r: (bi, r, SMALL_GATE_BLOCK)),
            pl.BlockSpec((1, 32, rb), lambda bi, g, r: (bi, 0, r)),
            pl.BlockSpec((1, 128), lambda bi, g, r: (0, 0)),
            pl.BlockSpec((1, 128), lambda bi, g, r: (0, 0)),
            pl.BlockSpec((32, 1), lambda bi, g, r: (0, 0)),
            pl.BlockSpec((32, 1), lambda bi, g, r: (0, 0)),
            pl.BlockSpec((1, 128), lambda bi, g, r: (0, 0)),
        ],
        out_specs=pl.BlockSpec((1, rb, hb * 128), lambda bi, g, r: (bi, r, g)),
        out_shape=jax.ShapeDtypeStruct((b, s, GDN_W), BF16),
        scratch_shapes=[pltpu.VMEM((hb, GDN_DK, GDN_DV), F32), pltpu.VMEM((3, rb + 8, hb * 128), F32),
                        pltpu.VMEM((3, rb, hb * 128), F32)],
        compiler_params=_params("parallel", "parallel", "arbitrary"),
        name="gdn_delta_rule",
    )(main, main, main, w_conv, w_conv, w_conv, main, small, small_t, arow, dtrow, acol, dtcol,
      o_norm.reshape(1, GDN_DV).astype(F32))


def _merge_kernel(om_ref, og_ref, gm_ref, gg_ref, wm_ref, wg_ref, o_ref):
    ym = _dot(om_ref[...], wm_ref[...])
    yg = _dot(og_ref[...], wg_ref[...])
    o_ref[...] = (_sigmoid(gm_ref[...]) * ym + _sigmoid(gg_ref[...]) * yg).astype(o_ref.dtype)


def _merge(o_mla, o_gdn, main, w_mla, w_gdn):
    t, d = o_mla.shape
    tm, tn = _tile(t, 512), _tile(d, 1024)
    goff = (4 * GDN_W) // tn
    nn = d // tn
    return pl.pallas_call(
        _merge_kernel,
        grid=(nn, t // tm),
        in_specs=[
            pl.BlockSpec((tm, d), lambda j, i: (i, 0)),
            pl.BlockSpec((tm, d), lambda j, i: (i, 0)),
            pl.BlockSpec((tm, tn), lambda j, i: (i, goff + j)),
            pl.BlockSpec((tm, tn), lambda j, i: (i, goff + nn + j)),
            pl.BlockSpec((d, tn), lambda j, i: (0, j)),
            pl.BlockSpec((d, tn), lambda j, i: (0, j)),
        ],
        out_specs=pl.BlockSpec((tm, tn), lambda j, i: (i, j)),
        out_shape=jax.ShapeDtypeStruct((t, d), BF16),
        compiler_params=_params("parallel", "parallel"),
        name="mixer_merge",
    )(o_mla, o_gdn, main, main, w_mla, w_gdn)


def _proj_ln_kernel(a_ref, w_ref, x_ref, g_ref, b_ref, o_ref, obf_ref):
    y = DN_ALPHA * x_ref[...] + _dot(a_ref[...], w_ref[...])
    out = _layer_norm(y, g_ref[...], b_ref[...])
    o_ref[...] = out
    obf_ref[...] = out.astype(BF16)


def _proj_ln(a, w, x, g, bias):
    t, k = a.shape
    d = w.shape[1]
    tm = _tile(t, 512)
    return pl.pallas_call(
        _proj_ln_kernel,
        grid=(t // tm,),
        in_specs=[
            pl.BlockSpec((tm, k), lambda i: (i, 0)),
            pl.BlockSpec((k, d), lambda i: (0, 0)),
            pl.BlockSpec((tm, d), lambda i: (i, 0)),
            pl.BlockSpec((1, d), lambda i: (0, 0)),
            pl.BlockSpec((1, d), lambda i: (0, 0)),
        ],
        out_specs=[pl.BlockSpec((tm, d), lambda i: (i, 0)), pl.BlockSpec((tm, d), lambda i: (i, 0))],
        out_shape=[jax.ShapeDtypeStruct((t, d), F32), jax.ShapeDtypeStruct((t, d), BF16)],
        compiler_params=_params("parallel"),
        name="proj_residual_ln",
    )(a, w, x, g.reshape(1, d), bias.reshape(1, d))


def _xattn_kernel(xbf_ref, x_ref, wq_ref, kv_ref, wo_ref, g_ref, b_ref, o_ref, opk_ref):
    q = _dot(xbf_ref[0], wq_ref[...])
    scale = X_HEAD_DIM ** -0.5
    outs = []
    for h in range(X_HEADS):
        qh = q[:, h * X_HEAD_DIM:(h + 1) * X_HEAD_DIM].astype(BF16)
        kh = kv_ref[0, :, h * X_HEAD_DIM:(h + 1) * X_HEAD_DIM]
        vh = kv_ref[0, :, X_W + h * X_HEAD_DIM:X_W + (h + 1) * X_HEAD_DIM]
        s = _dot_nt(qh, kh) * scale
        e = jnp.exp(s - jnp.max(s, -1, keepdims=True))
        p = e / jnp.sum(e, -1, keepdims=True)
        outs.append(_dot(p.astype(BF16), vh))
    o = jnp.concatenate(outs, -1).astype(BF16)
    y = DN_ALPHA * x_ref[0] + _dot(o, wo_ref[...])
    out = _layer_norm(y, g_ref[...], b_ref[...])
    o_ref[0] = out
    opk_ref[0] = _pack_halves(out)


def _xattn(x_bf, x, wq, kv, wo, g, bias):
    b, s, d = x.shape
    tm = _tile(s, 512)
    ml = kv.shape[1]
    return pl.pallas_call(
        _xattn_kernel,
        grid=(b, s // tm),
        in_specs=[
            pl.BlockSpec((1, tm, d), lambda bi, i: (bi, i, 0)),
            pl.BlockSpec((1, tm, d), lambda bi, i: (bi, i, 0)),
            pl.BlockSpec((d, X_W), lambda bi, i: (0, 0)),
            pl.BlockSpec((1, ml, 2 * X_W), lambda bi, i: (bi, 0, 0)),
            pl.BlockSpec((X_W, d), lambda bi, i: (0, 0)),
            pl.BlockSpec((1, d), lambda bi, i: (0, 0)),
            pl.BlockSpec((1, d), lambda bi, i: (0, 0)),
        ],
        out_specs=[pl.BlockSpec((1, tm, d), lambda bi, i: (bi, i, 0)),
                   pl.BlockSpec((1, tm, d // 2), lambda bi, i: (bi, i, 0))],
        out_shape=[jax.ShapeDtypeStruct((b, s, d), F32), jax.ShapeDtypeStruct((b, s, d // 2), jnp.uint32)],
        compiler_params=_params("parallel", "parallel"),
        name="memory_cross_attention",
    )(x_bf, x, wq, kv, wo, g.reshape(1, d), bias.reshape(1, d))


def _first_argmax(vals, iota, n):
    m = jnp.max(vals, 0, keepdims=True)
    idx = jnp.min(jnp.where(vals == m, iota, n), 0, keepdims=True)
    return m, idx


def _router_kernel(x_ref, wr_ref, bias_ref, idx_ref, gate_ref, rank_ref, cnt_ref, carry_s, *, tm):
    i = pl.program_id(0)

    @pl.when(i == 0)
    def _():
        carry_s[...] = jnp.zeros(carry_s.shape, F32)

    logits = _dot_nt(wr_ref[...].astype(BF16), x_ref[...].astype(BF16))
    scores = 1.0 / (1.0 + jnp.exp(-logits))
    sel = scores + bias_ref[...]
    epg = EXPERTS_PER_GROUP
    io8 = lax.broadcasted_iota(jnp.int32, (epg, tm), 0)
    best_score = None
    for g in range(N_GROUPS):
        sg = sel[g * epg:(g + 1) * epg, :]
        m1, i1 = _first_argmax(sg, io8, epg)
        m2 = jnp.max(jnp.where(io8 == i1, -jnp.inf, sg), 0, keepdims=True)
        gs = m1 + m2
        if g == 0:
            best_score, best = gs, jnp.zeros((1, tm), jnp.int32)
            in_sel, in_sc = sg, scores[0:epg, :]
        else:
            better = gs > best_score
            best_score = jnp.where(better, gs, best_score)
            best = jnp.where(better, g, best)
            in_sel = jnp.where(better, sg, in_sel)
            in_sc = jnp.where(better, scores[g * epg:(g + 1) * epg, :], in_sc)
    _, l1 = _first_argmax(in_sel, io8, epg)
    _, l2 = _first_argmax(jnp.where(io8 == l1, -jnp.inf, in_sel), io8, epg)
    g1 = jnp.sum(jnp.where(io8 == l1, in_sc, 0.0), 0, keepdims=True)
    g2 = jnp.sum(jnp.where(io8 == l2, in_sc, 0.0), 0, keepdims=True)
    e1 = best * epg + l1
    e2 = best * epg + l2
    idx_ref[0:1, :] = e1
    idx_ref[1:2, :] = e2
    gsum = g1 + g2
    gate_ref[0:1, :] = g1 / gsum
    gate_ref[1:2, :] = g2 / gsum

    ioe = lax.broadcasted_iota(jnp.int32, (N_EXPERTS, tm), 0)
    oh1 = ioe == e1
    oh2 = ioe == e2
    onehot = jnp.where(oh1, 1.0, 0.0) + jnp.where(oh2, 1.0, 0.0)
    rr = lax.broadcasted_iota(jnp.int32, (tm, tm), 0)
    cc = lax.broadcasted_iota(jnp.int32, (tm, tm), 1)
    upper = jnp.where(rr < cc, 1.0, 0.0).astype(BF16)
    tot = carry_s[...] + _dot(onehot.astype(BF16), upper)
    rank_ref[0:1, :] = jnp.sum(jnp.where(oh1, tot, 0.0), 0, keepdims=True).astype(jnp.int32)
    rank_ref[1:2, :] = jnp.sum(jnp.where(oh2, tot, 0.0), 0, keepdims=True).astype(jnp.int32)
    new = carry_s[...] + jnp.sum(onehot, -1, keepdims=True)
    carry_s[...] = new
    cnt_ref[...] = jnp.broadcast_to(new, cnt_ref.shape)


def _router(x, w_router_t, bias):
    t, d = x.shape
    tm = _tile(t, 512)
    return pl.pallas_call(
        functools.partial(_router_kernel, tm=tm),
        grid=(t // tm,),
        in_specs=[pl.BlockSpec((tm, d), lambda i: (i, 0)),
                  pl.BlockSpec((N_EXPERTS, d), lambda i: (0, 0)),
                  pl.BlockSpec((N_EXPERTS, 1), lambda i: (0, 0))],
        out_specs=[pl.BlockSpec((TOP_K, tm), lambda i: (0, i)),
                   pl.BlockSpec((TOP_K, tm), lambda i: (0, i)),
                   pl.BlockSpec((TOP_K, tm), lambda i: (0, i)),
                   pl.BlockSpec((N_EXPERTS, 128), lambda i: (0, 0))],
        out_shape=[jax.ShapeDtypeStruct((TOP_K, t), jnp.int32),
                   jax.ShapeDtypeStruct((TOP_K, t), F32),
                   jax.ShapeDtypeStruct((TOP_K, t), jnp.int32),
                   jax.ShapeDtypeStruct((N_EXPERTS, 128), F32)],
        scratch_shapes=[pltpu.VMEM((N_EXPERTS, 1), F32)],
        compiler_params=_params("arbitrary"),
        name="moe_router",
    )(x, w_router_t, bias)


def _slot_kernel(idx_ref, rank_ref, pstart_ref, dest_ref, *, tm):
    ioe = lax.broadcasted_iota(jnp.int32, (N_EXPERTS, tm), 0)
    pstart = pstart_ref[...]
    for k in range(TOP_K):
        start = jnp.sum(jnp.where(ioe == idx_ref[k:k + 1, :], pstart, 0.0), 0, keepdims=True)
        dest_ref[k:k + 1, :] = start.astype(jnp.int32) + rank_ref[k:k + 1, :]


def _slots(idx, rank, pstart):
    t = idx.shape[1]
    tm = _tile(t, 2048)
    spec = pl.BlockSpec((TOP_K, tm), lambda i: (0, i))
    return pl.pallas_call(
        functools.partial(_slot_kernel, tm=tm),
        grid=(t // tm,),
        in_specs=[spec, spec, pl.BlockSpec((N_EXPERTS, 1), lambda i: (0, 0))],
        out_specs=spec,
        out_shape=jax.ShapeDtypeStruct((TOP_K, t), jnp.int32),
        compiler_params=_params("parallel"),
        name="moe_slots",
    )(idx, rank, pstart)


def _dispatch_kernel(dest_ref, x_ref, xs_in_ref, xs_ref, sem, *, tm, t_total):
    del xs_in_ref
    base = pl.program_id(0) * tm

    def issue(t, carry):
        for k in range(TOP_K):
            d = dest_ref[k * t_total + base + t]
            pltpu.make_async_copy(x_ref.at[pl.ds(t, 1), :], xs_ref.at[pl.ds(d, 1), :], sem).start(priority=k % 2)
        return carry

    lax.fori_loop(0, tm, issue, 0, unroll=DMA_ISSUE_UNROLL)

    def drain(t, carry):
        for k in range(TOP_K):
            pltpu.make_async_copy(x_ref.at[pl.ds(0, 1), :], xs_ref.at[pl.ds(0, 1), :], sem).wait()
        return carry

    lax.fori_loop(0, tm, drain, 0, unroll=DMA_ISSUE_UNROLL)


def _dispatch(dest_flat, x, n_slots):
    t, d = x.shape
    tm = _tile(t, 512)
    xs0 = jnp.zeros((n_slots, d), x.dtype)
    return pl.pallas_call(
        functools.partial(_dispatch_kernel, tm=tm, t_total=t),
        grid_spec=pltpu.PrefetchScalarGridSpec(
            num_scalar_prefetch=1,
            grid=(t // tm,),
            in_specs=[pl.BlockSpec((tm, d), lambda i, dest: (i, 0)),
                      pl.BlockSpec(memory_space=pl.ANY)],
            out_specs=pl.BlockSpec(memory_space=pl.ANY),
            scratch_shapes=[pltpu.SemaphoreType.DMA(())],
        ),
        out_shape=jax.ShapeDtypeStruct((n_slots, d), x.dtype),
        input_output_aliases={2: 0},
        compiler_params=_params("arbitrary"),
        name="moe_dispatch",
    )(dest_flat, x, xs0)


def _expert_kernel(be_ref, nu_ref, slot_ref, nx_ref, xs_ref, wg_hbm, wu_hbm, wd_hbm, ys_ref,
                   fg_s, fu_s, fd_s, wg_s, wu_s, wd_s, sems, *, layer):
    i = pl.program_id(0)
    e = be_ref[i]

    def weight_copies(expert, slot):
        return (pltpu.make_async_copy(wg_hbm.at[layer, expert], fg_s.at[slot], sems.at[slot, 0]),
                pltpu.make_async_copy(wu_hbm.at[layer, expert], fu_s.at[slot], sems.at[slot, 1]),
                pltpu.make_async_copy(wd_hbm.at[layer, expert], fd_s.at[slot], sems.at[slot, 2]))

    @pl.when(i == 0)
    def _():
        for cp in weight_copies(e, 0):
            cp.start()

    @pl.when((i == 0) | (e != be_ref[jnp.maximum(i - 1, 0)]))
    def _():
        slot = slot_ref[i]
        for cp in weight_copies(e, slot):
            cp.wait()
        wg_s[...] = fg_s[slot].astype(BF16)
        wu_s[...] = fu_s[slot].astype(BF16)
        wd_s[...] = fd_s[slot].astype(BF16)

        @pl.when(nx_ref[i] >= 0)
        def _():
            for cp in weight_copies(nx_ref[i], 1 - slot):
                cp.start()

    @pl.when(i < nu_ref[0])
    def _():
        lo, hi = _unpack_halves(xs_ref[...])
        xb = jnp.concatenate([lo.astype(BF16), hi.astype(BF16)], -1)
        hg = _dot(xb, wg_s[...])
        hu = _dot(xb, wu_s[...])
        h = (hg * _sigmoid(hg)) * hu
        ys_ref[...] = _pack_halves(_dot(h.astype(BF16), wd_s[...]))

    @pl.when(i >= nu_ref[0])
    def _():
        ys_ref[...] = jnp.zeros(ys_ref.shape, ys_ref.dtype)


def _experts(block_expert, n_used, xs, wg, wu, wd, layer):
    n_slots, dp = xs.shape
    d = 2 * dp
    blk = MOE_BLOCK
    n_blocks = n_slots // blk
    de = wg.shape[-1]
    pos = jnp.arange(n_blocks, dtype=jnp.int32)
    first = jnp.concatenate([jnp.ones((1,), bool), block_expert[1:] != block_expert[:-1]])
    slot = ((jnp.cumsum(first.astype(jnp.int32)) - 1) % 2).astype(jnp.int32)
    first_pos = jnp.where(first, pos, n_blocks)
    next_first = lax.cummin(jnp.concatenate([first_pos[1:], jnp.full((1,), n_blocks, jnp.int32)]), reverse=True)
    next_expert = jnp.where(next_first < n_blocks, block_expert[jnp.minimum(next_first, n_blocks - 1)], -1)
    hbm = pl.BlockSpec(memory_space=pl.ANY)
    return pl.pallas_call(
        functools.partial(_expert_kernel, layer=layer),
        grid_spec=pltpu.PrefetchScalarGridSpec(
            num_scalar_prefetch=4,
            grid=(n_blocks,),
            in_specs=[
                pl.BlockSpec((blk, dp), lambda i, be, nu, sl, nx: (jnp.minimum(i, nu[0] - 1), 0)),
                hbm, hbm, hbm,
            ],
            out_specs=pl.BlockSpec((blk, dp), lambda i, be, nu, sl, nx: (i, 0)),
            scratch_shapes=[pltpu.VMEM((2, d, de), F32), pltpu.VMEM((2, d, de), F32), pltpu.VMEM((2, de, d), F32),
                            pltpu.VMEM((d, de), BF16), pltpu.VMEM((d, de), BF16), pltpu.VMEM((de, d), BF16),
                            pltpu.SemaphoreType.DMA((2, 3))],
        ),
        out_shape=jax.ShapeDtypeStruct((n_slots, dp), jnp.uint32),
        compiler_params=_params("arbitrary"),
        name="moe_experts",
    )(block_expert, n_used, slot, next_expert.astype(jnp.int32), xs, wg, wu, wd)


def _combine_kernel(dest_ref, ys_ref, x_ref, gate_ref, g_ref, b_ref, o_ref, buf_s, sem, *, tm, t_total):
    base = pl.program_id(0) * tm

    def issue(t, carry):
        for k in range(TOP_K):
            d = dest_ref[k * t_total + base + t]
            pltpu.make_async_copy(ys_ref.at[pl.ds(d, 1), :], buf_s.at[k, pl.ds(t, 1), :], sem).start(priority=k % 2)
        return carry

    lax.fori_loop(0, tm, issue, 0, unroll=DMA_ISSUE_UNROLL)

    def drain(t, carry):
        for k in range(TOP_K):
            pltpu.make_async_copy(ys_ref.at[pl.ds(0, 1), :], buf_s.at[0, pl.ds(0, 1), :], sem).wait()
        return carry

    lax.fori_loop(0, tm, drain, 0, unroll=DMA_ISSUE_UNROLL)
    gate = gate_ref[...]
    lo0, hi0 = _unpack_halves(buf_s[0])
    lo1, hi1 = _unpack_halves(buf_s[1])
    g0, g1 = gate[:, 0:1], gate[:, 1:2]
    y = jnp.concatenate([g0 * lo0 + g1 * lo1, g0 * hi0 + g1 * hi1], -1)
    o_ref[...] = _layer_norm(DN_ALPHA * x_ref[...] + y, g_ref[...], b_ref[...])


def _combine(dest_flat, ys, x, gate_tk, g, bias):
    t, d = x.shape
    tm = _tile(t, 512)
    return pl.pallas_call(
        functools.partial(_combine_kernel, tm=tm, t_total=t),
        grid_spec=pltpu.PrefetchScalarGridSpec(
            num_scalar_prefetch=1,
            grid=(t // tm,),
            in_specs=[
                pl.BlockSpec(memory_space=pl.ANY),
                pl.BlockSpec((tm, d), lambda i, dest: (i, 0)),
                pl.BlockSpec((tm, TOP_K), lambda i, dest: (i, 0)),
                pl.BlockSpec((1, d), lambda i, dest: (0, 0)),
                pl.BlockSpec((1, d), lambda i, dest: (0, 0)),
            ],
            out_specs=pl.BlockSpec((tm, d), lambda i, dest: (i, 0)),
            scratch_shapes=[pltpu.VMEM((TOP_K, tm, d // 2), jnp.uint32), pltpu.SemaphoreType.DMA(())],
        ),
        out_shape=jax.ShapeDtypeStruct((t, d), F32),
        compiler_params=_params("arbitrary"),
        name="moe_combine_ln",
    )(dest_flat, ys, x, gate_tk, g.reshape(1, d), bias.reshape(1, d))


def _moe(x, x_packed, w_router_t, router_bias, wg, wu, wd, layer, g, bias):
    t, d = x.shape
    idx, gate, rank, cnt = _router(x, w_router_t, router_bias)
    blk = MOE_BLOCK
    n_blocks = t * TOP_K // blk + N_EXPERTS
    counts = cnt[:, 0].astype(jnp.int32)
    padded = (counts + blk - 1) // blk * blk
    padded_end = jnp.cumsum(padded)
    padded_start = padded_end - padded
    dest = _slots(idx, rank, padded_start.astype(F32).reshape(N_EXPERTS, 1)).reshape(-1)
    n_used = (padded_end[-1] // blk).astype(jnp.int32)
    blocks = jnp.minimum(jnp.arange(n_blocks, dtype=jnp.int32), n_used - 1)
    block_expert = jnp.minimum(jnp.searchsorted(padded_end, blocks * blk, side='right'),
                               N_EXPERTS - 1).astype(jnp.int32)
    xs = _dispatch(dest, x_packed, n_blocks * blk)
    ys = _experts(block_expert, n_used.reshape(1), xs, wg, wu, wd, layer)
    return _combine(dest, ys, x, gate.T, g, bias)


def _in_proj_weights(w_in):
    sizes = (Q_LORA, KV_LORA, QK_ROPE, GDN_W, GDN_W, GDN_W, GDN_W, GDN_HEADS, GDN_HEADS, D_MODEL, D_MODEL)
    pts = np.cumsum(sizes)[:-1].tolist()
    c_q, c_kv, k_rope, g_q, g_k, g_v, g_z, g_b, g_a, gate_mla, gate_gdn = jnp.split(w_in, pts, axis=-1)
    half = QK_ROPE // 2
    k_rope_sw = jnp.concatenate([k_rope[:, half:], k_rope[:, :half]], -1)
    main = jnp.concatenate([g_q, g_k, g_v, g_z, gate_mla, gate_gdn], -1)
    used = Q_LORA + KV_LORA + 2 * QK_ROPE + 2 * GDN_HEADS
    small = jnp.concatenate([c_q, c_kv, k_rope, k_rope_sw, g_b, g_a,
                             jnp.zeros((w_in.shape[0], SMALL_W - used), w_in.dtype)], -1)
    return main.astype(BF16), small.astype(BF16)


def _mla_weights(w_uq, w_ukv):
    half = QK_ROPE // 2
    wq = w_uq.reshape(Q_LORA, MLA_HEADS, QK_DIM).transpose(1, 0, 2)
    pe = wq[..., QK_NOPE:]
    pe_sw = jnp.concatenate([pe[..., half:], pe[..., :half]], -1)
    wq = jnp.concatenate([wq, pe_sw], -1).astype(BF16)
    wkv = w_ukv.reshape(KV_LORA, MLA_HEADS, QK_NOPE + V_HEAD).transpose(1, 0, 2).astype(BF16)
    return wq, wkv


def _rope_table(positions):
    half = QK_ROPE // 2
    inv_freq = jnp.power(ROPE_BASE, -jnp.arange(half, dtype=F32) / half)
    ang = positions.astype(F32)[..., None] * inv_freq
    cos, sin = jnp.cos(ang), jnp.sin(ang)
    return jnp.concatenate([cos, cos, -sin, sin], -1)


def kernel(x, mem, positions, w_in, mla_q_norm, mla_kv_norm, w_uq, w_ukv, gdn_conv, gdn_a_log, gdn_dt_bias, gdn_o_norm, w_mla_proj, w_gdn_proj, w_mix_out, ln_mix_g, ln_mix_b, w_xq, w_xkv, w_xo, ln_x_g, ln_x_b, w_router, router_bias, w_gate, w_up, w_down, ln_moe_g, ln_moe_b):
    b, s, d = x.shape
    t = b * s
    cs = _rope_table(positions)
    mem2 = mem.reshape(-1, d)
    w_router_t = w_router.T.astype(F32)
    rbias = router_bias.reshape(N_EXPERTS, 1).astype(F32)
    for l in range(w_in.shape[0]):
        w_main, w_small = _in_proj_weights(w_in[l])
        wq, wkv = _mla_weights(w_uq[l], w_ukv[l])
        xt = x.reshape(t, d)
        main = _matmul(xt, w_main, F32, 1024, 2048, "in_proj_main").reshape(b, s, MAIN_W)
        small = _matmul(xt, w_small, F32, 1024, SMALL_W, "in_proj_small").reshape(b, s, SMALL_W)

        q, k, v = _mla_prep(small, cs, mla_q_norm[l].reshape(1, -1), mla_kv_norm[l].reshape(1, -1), wq, wkv)
        o_mla = _attention(q, k, v)

        small_t = small[:, :, SMALL_GATE_COL:SMALL_GATE_COL + 2 * GDN_HEADS].transpose(0, 2, 1)
        o_gdn = _gdn(main, gdn_conv[l], small, small_t, gdn_a_log[l], gdn_dt_bias[l], gdn_o_norm[l])

        merged = _merge(o_mla.reshape(t, d), o_gdn.reshape(t, d), main.reshape(t, MAIN_W),
                        w_mla_proj[l].astype(BF16), w_gdn_proj[l].astype(BF16))
        x1, x1_bf = _proj_ln(merged, w_mix_out[l].astype(BF16), x.reshape(t, d), ln_mix_g[l], ln_mix_b[l])

        kv = _matmul(mem2, w_xkv[l].astype(BF16), BF16, 1024, 1024, "mem_kv").reshape(b, -1, 2 * X_W)
        x2, x2_pk = _xattn(x1_bf.reshape(b, s, d), x1.reshape(b, s, d), w_xq[l].astype(BF16), kv,
                           w_xo[l].astype(BF16), ln_x_g[l], ln_x_b[l])

        x3 = _moe(x2.reshape(t, d), x2_pk.reshape(t, d // 2), w_router_t, rbias, w_gate, w_up, w_down, l,
                  ln_moe_g[l], ln_moe_b[l])
        x = x3.reshape(b, s, d)
    return x
```

```python
import functools
import math

import jax
import jax.numpy as jnp
import numpy as np
from jax import lax
from jax.experimental import pallas as pl
from jax.experimental.pallas import tpu as pltpu

D_MODEL = 2048
DEPTH = 2
MLA_HEADS = 16
Q_LORA = 512
KV_LORA = 512
QK_NOPE = 128
QK_ROPE = 64
V_HEAD = 128
QK_DIM = QK_NOPE + QK_ROPE
ROPE_BASE = 10000.0
GDN_HEADS = 16
GDN_DK = 128
GDN_DV = 128
CONV_K = 4
CHUNK = 64
GDN_W = GDN_HEADS * GDN_DK
X_HEADS = 4
X_HEAD_DIM = 128
X_W = X_HEADS * X_HEAD_DIM
N_EXPERTS = 64
N_GROUPS = 8
EXPERTS_PER_GROUP = N_EXPERTS // N_GROUPS
TOP_K = 2
D_EXPERT = 512
Q_PRESCALE = QK_DIM ** -0.5 * math.log2(math.e)
DN_ALPHA = (2.0 * DEPTH) ** 0.25
LN_EPS = 1e-5
RMS_EPS = 1e-6

MAIN_W = 3 * GDN_W + GDN_W + 2 * D_MODEL
SMALL_W = 1280
SMALL_ROPE_BLOCK = (Q_LORA + KV_LORA) // 128
SMALL_GATE_COL = Q_LORA + KV_LORA + 2 * QK_ROPE
SMALL_GATE_BLOCK = SMALL_GATE_COL // 128

GDN_HEAD_BLOCK = 16
GDN_ROW_BLOCK = 128
MOE_BLOCK = 256
DMA_ISSUE_UNROLL = 8
VMEM_LIMIT = 56 * 1024 * 1024

BF16 = jnp.bfloat16
F32 = jnp.float32


def _tile(n, pref):
    return pref if n % pref == 0 else n


def _params(*sem):
    return pltpu.CompilerParams(dimension_semantics=sem, vmem_limit_bytes=VMEM_LIMIT)


def _sigmoid(x):
    return 0.5 * jnp.tanh(0.5 * x) + 0.5


def _silu(x):
    h = 0.5 * x
    return h + h * jnp.tanh(h)


def _pack_halves(y):
    n = y.shape[1] // 2
    lo = pltpu.bitcast(y[:, :n].astype(BF16).astype(F32), jnp.uint32)
    hi = pltpu.bitcast(y[:, n:].astype(BF16).astype(F32), jnp.uint32)
    return (lo >> 16) | (hi & jnp.uint32(0xFFFF0000))


def _unpack_halves(p):
    lo = pltpu.bitcast(p << 16, F32)
    hi = pltpu.bitcast(p & jnp.uint32(0xFFFF0000), F32)
    return lo, hi


def _softplus(x):
    return jnp.maximum(x, 0.0) + jnp.log1p(jnp.exp(-jnp.abs(x)))


def _layer_norm(y, g, b):
    mu = jnp.mean(y, -1, keepdims=True)
    d = y - mu
    var = jnp.mean(d * d, -1, keepdims=True)
    return d * lax.rsqrt(var + LN_EPS) * g + b


def _dot(a, b):
    return jnp.dot(a, b, preferred_element_type=F32)


def _dot_nt(a, b):
    return lax.dot_general(a, b, (((1,), (1,)), ((), ())), preferred_element_type=F32)


def _mm_kernel(a_ref, w_ref, o_ref, a_s):
    @pl.when(pl.program_id(1) == 0)
    def _():
        a_s[...] = a_ref[...].astype(BF16)

    o_ref[...] = _dot(a_s[...], w_ref[...]).astype(o_ref.dtype)


def _matmul(a, w, out_dtype, tm, tn, name):
    m, k = a.shape
    n = w.shape[1]
    tm, tn = _tile(m, tm), _tile(n, tn)
    return pl.pallas_call(
        _mm_kernel,
        grid=(m // tm, n // tn),
        in_specs=[pl.BlockSpec((tm, k), lambda i, j: (i, 0)),
                  pl.BlockSpec((k, tn), lambda i, j: (0, j))],
        out_specs=pl.BlockSpec((tm, tn), lambda i, j: (i, j)),
        out_shape=jax.ShapeDtypeStruct((m, n), out_dtype),
        scratch_shapes=[pltpu.VMEM((tm, k), BF16)],
        compiler_params=_params("parallel", "arbitrary"),
        name=name,
    )(a, w)


def _mla_prep_kernel(cq_ref, ckv_ref, kr_ref, cs_ref, qn_ref, kvn_ref, wq_ref, wkv_ref, q_ref, k_ref, v_ref):
    ts = cq_ref.shape[1]
    cq = cq_ref[0]
    cqn = (cq * lax.rsqrt(jnp.mean(cq * cq, -1, keepdims=True) + RMS_EPS) * qn_ref[...]).astype(BF16)
    ckv = ckv_ref[0]
    ckvn = (ckv * lax.rsqrt(jnp.mean(ckv * ckv, -1, keepdims=True) + RMS_EPS) * kvn_ref[...]).astype(BF16)
    cs = cs_ref[0]
    lane = lax.broadcasted_iota(jnp.int32, (ts, 128), 1)

    def rotate(t):
        u = t * cs
        return jnp.where(lane < QK_ROPE, u + pltpu.roll(u, QK_ROPE, 1), 0.0).astype(BF16)

    kpe = rotate(kr_ref[0])
    ones = jnp.ones((ts, V_HEAD), BF16)
    for h in range(MLA_HEADS):
        rq = _dot(cqn, wq_ref[h]) * Q_PRESCALE
        q_ref[0, h, :, :QK_NOPE] = rq[:, :QK_NOPE].astype(BF16)
        q_ref[0, h, :, QK_NOPE:] = rotate(rq[:, QK_NOPE:])
        rkv = _dot(ckvn, wkv_ref[h])
        k_ref[0, h, :, :QK_NOPE] = rkv[:, :QK_NOPE].astype(BF16)
        k_ref[0, h, :, QK_NOPE:] = kpe
        v_ref[0, h, :, :V_HEAD] = rkv[:, QK_NOPE:].astype(BF16)
        v_ref[0, h, :, V_HEAD:] = ones


def _mla_prep(small, cs, q_norm, kv_norm, wq, wkv):
    b, s, _ = small.shape
    ts = _tile(s, 512)
    hq = MLA_HEADS
    out_spec = pl.BlockSpec((1, hq, ts, 256), lambda bi, si: (bi, 0, si, 0))
    out_shape = jax.ShapeDtypeStruct((b, hq, s, 256), BF16)
    return pl.pallas_call(
        _mla_prep_kernel,
        grid=(b, s // ts),
        in_specs=[
            pl.BlockSpec((1, ts, Q_LORA), lambda bi, si: (bi, si, 0)),
            pl.BlockSpec((1, ts, KV_LORA), lambda bi, si: (bi, si, 1)),
            pl.BlockSpec((1, ts, 128), lambda bi, si: (bi, si, SMALL_ROPE_BLOCK)),
            pl.BlockSpec((1, ts, 128), lambda bi, si: (bi, si, 0)),
            pl.BlockSpec((1, Q_LORA), lambda bi, si: (0, 0)),
            pl.BlockSpec((1, KV_LORA), lambda bi, si: (0, 0)),
            pl.BlockSpec((hq, Q_LORA, 256), lambda bi, si: (0, 0, 0)),
            pl.BlockSpec((hq, KV_LORA, 256), lambda bi, si: (0, 0, 0)),
        ],
        out_specs=[out_spec, out_spec, out_spec],
        out_shape=[out_shape, out_shape, out_shape],
        compiler_params=_params("parallel", "parallel"),
        name="mla_prep",
    )(small, small, small, cs, q_norm, kv_norm, wq, wkv)


def _attn_kernel(q_ref, k_ref, v_ref, o_ref, sa_s, sb_s, m_s, acc_s, *, tq, rc):
    i = pl.program_id(2)
    dv2 = acc_s.shape[-1]
    m_s[...] = jnp.full(m_s.shape, -jnp.inf, F32)
    acc_s[...] = jnp.zeros(acc_s.shape, F32)

    def scores(j):
        start = pl.multiple_of(j * tq, tq)
        return _dot_nt(q_ref[0, 0], k_ref[0, 0, pl.ds(start, tq), :])

    def softmax_pv(s_ref, c, j, nk, masked):
        rows = slice(c * rc, (c + 1) * rc)
        start = pl.multiple_of(j * tq, tq)
        s = s_ref[rows, :nk]
        if masked:
            row = lax.broadcasted_iota(jnp.int32, (rc, nk), 0) + c * rc
            col = lax.broadcasted_iota(jnp.int32, (rc, nk), 1)
            s = jnp.where(col <= row, s, -jnp.inf)
        m_prev = m_s[rows, :]
        m_new = jnp.maximum(m_prev, jnp.max(s, -1, keepdims=True))
        alpha = jnp.exp2(m_prev - m_new)
        p = jnp.exp2(s - jnp.concatenate([m_new] * (nk // 128), -1))
        acc_s[rows, :] = (jnp.concatenate([alpha] * (dv2 // 128), -1) * acc_s[rows, :]
                          + _dot(p.astype(BF16), v_ref[0, 0, pl.ds(start, nk), :]))
        m_s[rows, :] = m_new

    nch = tq // rc

    def step(cur_ref, nxt_ref, j):
        nxt_ref[...] = scores(j + 1)
        for c in range(nch):
            softmax_pv(cur_ref, c, j, tq, False)

    def diagonal(cur_ref):
        for c in range(nch):
            softmax_pv(cur_ref, c, i, (c + 1) * rc, True)
        dv = dv2 // 2
        o_ref[0] = (acc_s[:, :dv] / acc_s[:, dv:]).astype(o_ref.dtype)

    sa_s[...] = scores(0)

    def pair(jj, carry):
        step(sa_s, sb_s, 2 * jj)
        step(sb_s, sa_s, 2 * jj + 1)
        return carry

    lax.fori_loop(0, i // 2, pair, 0)

    @pl.when(i % 2 == 0)
    def _():
        diagonal(sa_s)

    @pl.when(i % 2 == 1)
    def _():
        step(sa_s, sb_s, i - 1)
        diagonal(sb_s)


def _attention(q, k, v):
    b, h, s, dqk = q.shape
    dv2 = v.shape[-1]
    tq = _tile(s, 1024)
    rc = _tile(tq, 256)
    kern = functools.partial(_attn_kernel, tq=tq, rc=rc)
    return pl.pallas_call(
        kern,
        grid=(b, h, s // tq),
        in_specs=[
            pl.BlockSpec((1, 1, tq, dqk), lambda bi, hi, i: (bi, hi, i, 0)),
            pl.BlockSpec((1, 1, s, dqk), lambda bi, hi, i: (bi, hi, 0, 0)),
            pl.BlockSpec((1, 1, s, dv2), lambda bi, hi, i: (bi, hi, 0, 0)),
        ],
        out_specs=pl.BlockSpec((1, tq, dv2 // 2), lambda bi, hi, i: (bi, i, hi)),
        out_shape=jax.ShapeDtypeStruct((b, s, h * dv2 // 2), BF16),
        scratch_shapes=[pltpu.VMEM((tq, tq), F32), pltpu.VMEM((tq, tq), F32), pltpu.VMEM((tq, 128), F32),
                        pltpu.VMEM((tq, dv2), F32)],
        compiler_params=_params("parallel", "parallel", "arbitrary"),
        name="mla_attention",
    )(q, k, v)


def _gdn_kernel(qr_ref, kr_ref, vr_ref, wq_ref, wk_ref, wv_ref, z_ref, gate_ref, gatet_ref, arow_ref, dtrow_ref,
                acol_ref, dtcol_ref, onorm_ref, o_ref, state_s, xx_s, qkv_s, *, hb, rb):
    hg = pl.program_id(1)
    r = pl.program_id(2)

    @pl.when(r == 0)
    def _():
        state_s[...] = jnp.zeros(state_s.shape, F32)
        xx_s[:, 0:8, :] = jnp.zeros((3, 8, hb * 128), F32)

    @pl.when(r > 0)
    def _():
        xx_s[:, 0:8, :] = xx_s[:, rb:rb + 8, :]

    for ti, (x_ref, w_ref) in enumerate(((qr_ref, wq_ref), (kr_ref, wk_ref), (vr_ref, wv_ref))):
        xx_s[ti, 8:rb + 8, :] = x_ref[0]
        wc = w_ref[...]
        y = wc[CONV_K - 1:CONV_K, :] * xx_s[ti, 8:rb + 8, :]
        for j in range(CONV_K - 1):
            off = 8 - (CONV_K - 1) + j
            y = y + wc[j:j + 1, :] * xx_s[ti, off:off + rb, :]
        y = _silu(y)
        if ti == 2:
            qkv_s[ti] = y
        else:
            scale = GDN_DK ** -0.5 if ti == 0 else 1.0
            for h in range(hb):
                yh = y[:, h * 128:(h + 1) * 128]
                qkv_s[ti, :, h * 128:(h + 1) * 128] = yh * (
                    lax.rsqrt(jnp.sum(yh * yh, -1, keepdims=True) + RMS_EPS) * scale)

    def head_rows(ti, c, h):
        return qkv_s[ti, c * CHUNK:(c + 1) * CHUNK, h * 128:(h + 1) * 128]

    blk = gate_ref[0]
    beta_tm = _sigmoid(blk)
    g_tm = -jnp.exp(arow_ref[...]) * _softplus(blk + dtrow_ref[...])
    g_t = -jnp.exp(acol_ref[...]) * _softplus(gatet_ref[0] + dtcol_ref[...])

    ri = lax.broadcasted_iota(jnp.int32, (CHUNK, CHUNK), 0)
    ci = lax.broadcasted_iota(jnp.int32, (CHUNK, CHUNK), 1)
    causal = ri >= ci
    strict = ri > ci
    tri = jnp.where(causal, 1.0, 0.0).astype(F32)
    tri_t = jnp.where(ri <= ci, 1.0, 0.0).astype(F32)
    eye = jnp.where(ri == ci, 1.0, 0.0).astype(F32)
    onorm = onorm_ref[...]

    nc = rb // CHUNK
    items = [(c, h) for c in range(nc) for h in range(hb)]
    rows = [slice(c * CHUNK, (c + 1) * CHUNK) for c in range(nc)]
    gc_tm = [jnp.dot(tri, g_tm[rows[c], :], precision=lax.Precision.HIGHEST, preferred_element_type=F32)
             for c in range(nc)]
    gc_t = [jnp.dot(g_t[:, rows[c]], tri_t, precision=lax.Precision.HIGHEST, preferred_element_type=F32)
            for c in range(nc)]
    sel_l = lax.broadcasted_iota(jnp.int32, (CHUNK, 128), 1)
    sel_s = lax.broadcasted_iota(jnp.int32, (32, CHUNK), 0)
    gcol, bcol, grow, glast, egc, kt, kb = {}, {}, {}, {}, {}, {}, {}
    for it in items:
        c, h = it
        lane = GDN_HEADS + hg * hb + h
        gcol[it] = jnp.sum(jnp.where(sel_l == lane, gc_tm[c], 0.0), -1, keepdims=True)
        bcol[it] = jnp.sum(jnp.where(sel_l == hg * hb + h, beta_tm[rows[c], :], 0.0), -1, keepdims=True)
        grow[it] = jnp.sum(jnp.where(sel_s == lane, gc_t[c], 0.0), 0, keepdims=True)
        glast[it] = gcol[it][CHUNK - 1:CHUNK, :]
        egc[it] = jnp.exp(gcol[it])
        k = head_rows(1, c, h)
        kt[it] = k.T
        kb[it] = k * bcol[it]
    kq = {it: _dot(jnp.concatenate([kb[it], head_rows(0, *it)], 0).astype(BF16),
                   kt[it].astype(BF16)) for it in items}
    amat, m, inv = {}, {}, {}
    for it in items:
        decay = jnp.exp(jnp.where(causal, gcol[it] - grow[it], -jnp.inf))
        amat[it] = (kq[it][CHUNK:] * decay).astype(BF16)
        m[it] = -jnp.where(strict, kq[it][:CHUNK] * decay, 0.0)
        inv[it] = eye + m[it]
    for it in items:
        mb = m[it].astype(BF16)
        m[it] = _dot(mb, mb)
    n_fac = int(round(math.log2(CHUNK))) - 1
    for t in range(n_fac):
        for it in items:
            mb = m[it].astype(BF16)
            if t < n_fac - 1:
                r2 = _dot(jnp.concatenate([inv[it], m[it]], 0).astype(BF16), mb)
                inv[it] = inv[it] + r2[:CHUNK]
                m[it] = r2[CHUNK:]
            else:
                inv[it] = inv[it] + _dot(inv[it].astype(BF16), mb)
    uw = {}
    for it in items:
        c, h = it
        vb = head_rows(2, c, h) * bcol[it]
        uw[it] = _dot(inv[it].astype(BF16), jnp.concatenate([vb, kb[it] * egc[it]], 1).astype(BF16))
    for c in range(nc):
        its = [(c, h) for h in range(hb)]
        st = {it: state_s[it[1]] for it in its}
        wq = {it: _dot(jnp.concatenate([uw[it][:, GDN_DV:], head_rows(0, *it) * egc[it]], 0).astype(BF16),
                       st[it].astype(BF16)) for it in its}
        vnb = {it: (uw[it][:, :GDN_DV] - wq[it][:CHUNK]).astype(BF16) for it in its}
        o = {it: wq[it][CHUNK:] + _dot(amat[it], vnb[it]) for it in its}
        for it in its:
            h = it[1]
            kdt = kt[it] * jnp.exp(glast[it] - grow[it])
            state_s[h] = st[it] * jnp.exp(glast[it]) + _dot(kdt.astype(BF16), vnb[it])
        for it in its:
            h = it[1]
            on = o[it] * lax.rsqrt(jnp.mean(o[it] * o[it], -1, keepdims=True) + RMS_EPS) * onorm
            zz = z_ref[0, rows[c], h * GDN_DV:(h + 1) * GDN_DV]
            o_ref[0, rows[c], h * GDN_DV:(h + 1) * GDN_DV] = (on * _silu(zz)).astype(o_ref.dtype)


def _gdn(main, w_conv, small, small_t, a_log, dt_bias, o_norm):
    b, s, _ = main.shape
    hb = GDN_HEAD_BLOCK
    rb = _tile(s, GDN_ROW_BLOCK)
    ng = GDN_HEADS // hb
    zoff = 3 * GDN_W // (hb * 128)
    pad = lambda a, lo: jnp.zeros((128,), F32).at[lo:lo + GDN_HEADS].set(a.astype(F32))
    arow = pad(a_log, GDN_HEADS).reshape(1, 128)
    dtrow = pad(dt_bias, GDN_HEADS).reshape(1, 128)
    acol = arow[0, :32].reshape(32, 1)
    dtcol = dtrow[0, :32].reshape(32, 1)
    kern = functools.partial(_gdn_kernel, hb=hb, rb=rb)
    return pl.pallas_call(
        kern,
        grid=(b, ng, s // rb),
        in_specs=[
            pl.BlockSpec((1, rb, hb * 128), lambda bi, g, r: (bi, r, g)),
            pl.BlockSpec((1, rb, hb * 128), lambda bi, g, r: (bi, r, ng + g)),
            pl.BlockSpec((1, rb, hb * 128), lambda bi, g, r: (bi, r, 2 * ng + g)),
            pl.BlockSpec((CONV_K, hb * 128), lambda bi, g, r: (0, g)),
            pl.BlockSpec((CONV_K, hb * 128), lambda bi, g, r: (0, ng + g)),
            pl.BlockSpec((CONV_K, hb * 128), lambda bi, g, r: (0, 2 * ng + g)),
            pl.BlockSpec((1, rb, hb * 128), lambda bi, g, r: (bi, r, zoff + g)),
            pl.BlockSpec((1, rb, 128), lambda bi, g, r: (bi, r, SMALL_GATE_BLOCK)),
            pl.BlockSpec((1, 32, rb), lambda bi, g, r: (bi, 0, r)),
            pl.BlockSpec((1, 128), lambda bi, g, r: (0, 0)),
            pl.BlockSpec((1, 128), lambda bi, g, r: (0, 0)),
            pl.BlockSpec((32, 1), lambda bi, g, r: (0, 0)),
            pl.BlockSpec((32, 1), lambda bi, g, r: (0, 0)),
            pl.BlockSpec((1, 128), lambda bi, g, r: (0, 0)),
        ],
        out_specs=pl.BlockSpec((1, rb, hb * 128), lambda bi, g, r: (bi, r, g)),
        out_shape=jax.ShapeDtypeStruct((b, s, GDN_W), BF16),
        scratch_shapes=[pltpu.VMEM((hb, GDN_DK, GDN_DV), F32), pltpu.VMEM((3, rb + 8, hb * 128), F32),
                        pltpu.VMEM((3, rb, hb * 128), F32)],
        compiler_params=_params("parallel", "parallel", "arbitrary"),
        name="gdn_delta_rule",
    )(main, main, main, w_conv, w_conv, w_conv, main, small, small_t, arow, dtrow, acol, dtcol,
      o_norm.reshape(1, GDN_DV).astype(F32))


def _merge_kernel(om_ref, og_ref, gm_ref, gg_ref, wm_ref, wg_ref, o_ref):
    ym = _dot(om_ref[...], wm_ref[...])
    yg = _dot(og_ref[...], wg_ref[...])
    o_ref[...] = (_sigmoid(gm_ref[...]) * ym + _sigmoid(gg_ref[...]) * yg).astype(o_ref.dtype)


def _merge(o_mla, o_gdn, main, w_mla, w_gdn):
    t, d = o_mla.shape
    tm, tn = _tile(t, 512), _tile(d, 1024)
    goff = (4 * GDN_W) // tn
    nn = d // tn
    return pl.pallas_call(
        _merge_kernel,
        grid=(nn, t // tm),
        in_specs=[
            pl.BlockSpec((tm, d), lambda j, i: (i, 0)),
            pl.BlockSpec((tm, d), lambda j, i: (i, 0)),
            pl.BlockSpec((tm, tn), lambda j, i: (i, goff + j)),
            pl.BlockSpec((tm, tn), lambda j, i: (i, goff + nn + j)),
            pl.BlockSpec((d, tn), lambda j, i: (0, j)),
            pl.BlockSpec((d, tn), lambda j, i: (0, j)),
        ],
        out_specs=pl.BlockSpec((tm, tn), lambda j, i: (i, j)),
        out_shape=jax.ShapeDtypeStruct((t, d), BF16),
        compiler_params=_params("parallel", "parallel"),
        name="mixer_merge",
    )(o_mla, o_gdn, main, main, w_mla, w_gdn)


def _proj_ln_kernel(a_ref, w_ref, x_ref, g_ref, b_ref, o_ref, obf_ref):
    y = DN_ALPHA * x_ref[...] + _dot(a_ref[...], w_ref[...])
    out = _layer_norm(y, g_ref[...], b_ref[...])
    o_ref[...] = out
    obf_ref[...] = out.astype(BF16)


def _proj_ln(a, w, x, g, bias):
    t, k = a.shape
    d = w.shape[1]
    tm = _tile(t, 512)
    return pl.pallas_call(
        _proj_ln_kernel,
        grid=(t // tm,),
        in_specs=[
            pl.BlockSpec((tm, k), lambda i: (i, 0)),
            pl.BlockSpec((k, d), lambda i: (0, 0)),
            pl.BlockSpec((tm, d), lambda i: (i, 0)),
            pl.BlockSpec((1, d), lambda i: (0, 0)),
            pl.BlockSpec((1, d), lambda i: (0, 0)),
        ],
        out_specs=[pl.BlockSpec((tm, d), lambda i: (i, 0)), pl.BlockSpec((tm, d), lambda i: (i, 0))],
        out_shape=[jax.ShapeDtypeStruct((t, d), F32), jax.ShapeDtypeStruct((t, d), BF16)],
        compiler_params=_params("parallel"),
        name="proj_residual_ln",
    )(a, w, x, g.reshape(1, d), bias.reshape(1, d))


def _xattn_kernel(xbf_ref, x_ref, wq_ref, kv_ref, wo_ref, g_ref, b_ref, wr_ref, rb_ref,
                  o_ref, opk_ref, idx_ref, gate_ref, rank_ref, cnt_ref, carry_s):
    q = _dot(xbf_ref[0], wq_ref[...])
    scale = X_HEAD_DIM ** -0.5
    outs = []
    for h in range(X_HEADS):
        qh = q[:, h * X_HEAD_DIM:(h + 1) * X_HEAD_DIM].astype(BF16)
        kh = kv_ref[0, :, h * X_HEAD_DIM:(h + 1) * X_HEAD_DIM]
        vh = kv_ref[0, :, X_W + h * X_HEAD_DIM:X_W + (h + 1) * X_HEAD_DIM]
        s = _dot_nt(qh, kh) * scale
        e = jnp.exp(s - jnp.max(s, -1, keepdims=True))
        p = e / jnp.sum(e, -1, keepdims=True)
        outs.append(_dot(p.astype(BF16), vh))
    o = jnp.concatenate(outs, -1).astype(BF16)
    y = DN_ALPHA * x_ref[0] + _dot(o, wo_ref[...])
    out = _layer_norm(y, g_ref[...], b_ref[...])
    o_ref[0] = out
    opk_ref[0] = _pack_halves(out)
    first = (pl.program_id(0) == 0) & (pl.program_id(1) == 0)
    _route_tile(out, first, wr_ref, rb_ref, idx_ref, gate_ref, rank_ref, cnt_ref, carry_s)


def _xattn(x_bf, x, wq, kv, wo, g, bias, w_router_t, router_bias):
    b, s, d = x.shape
    tm = _tile(s, 512)
    nt = s // tm
    ml = kv.shape[1]
    t = b * s
    tok = pl.BlockSpec((TOP_K, tm), lambda bi, i: (0, bi * nt + i))
    return pl.pallas_call(
        _xattn_kernel,
        grid=(b, nt),
        in_specs=[
            pl.BlockSpec((1, tm, d), lambda bi, i: (bi, i, 0)),
            pl.BlockSpec((1, tm, d), lambda bi, i: (bi, i, 0)),
            pl.BlockSpec((d, X_W), lambda bi, i: (0, 0)),
            pl.BlockSpec((1, ml, 2 * X_W), lambda bi, i: (bi, 0, 0)),
            pl.BlockSpec((X_W, d), lambda bi, i: (0, 0)),
            pl.BlockSpec((1, d), lambda bi, i: (0, 0)),
            pl.BlockSpec((1, d), lambda bi, i: (0, 0)),
            pl.BlockSpec((N_EXPERTS, d), lambda bi, i: (0, 0)),
            pl.BlockSpec((N_EXPERTS, 1), lambda bi, i: (0, 0)),
        ],
        out_specs=[pl.BlockSpec((1, tm, d), lambda bi, i: (bi, i, 0)),
                   pl.BlockSpec((1, tm, d // 2), lambda bi, i: (bi, i, 0)),
                   tok, tok, tok,
                   pl.BlockSpec((N_EXPERTS, 128), lambda bi, i: (0, 0))],
        out_shape=[jax.ShapeDtypeStruct((b, s, d), F32), jax.ShapeDtypeStruct((b, s, d // 2), jnp.uint32),
                   jax.ShapeDtypeStruct((TOP_K, t), jnp.int32), jax.ShapeDtypeStruct((TOP_K, t), F32),
                   jax.ShapeDtypeStruct((TOP_K, t), jnp.int32), jax.ShapeDtypeStruct((N_EXPERTS, 128), F32)],
        scratch_shapes=[pltpu.VMEM((N_EXPERTS, 1), F32)],
        compiler_params=_params("arbitrary", "arbitrary"),
        name="memory_cross_attention",
    )(x_bf, x, wq, kv, wo, g.reshape(1, d), bias.reshape(1, d), w_router_t, router_bias)


def _first_argmax(vals, iota, n):
    m = jnp.max(vals, 0, keepdims=True)
    idx = jnp.min(jnp.where(vals == m, iota, n), 0, keepdims=True)
    return m, idx


def _route_tile(x, first, wr_ref, bias_ref, idx_ref, gate_ref, rank_ref, cnt_ref, carry_s):
    tm = x.shape[0]

    @pl.when(first)
    def _():
        carry_s[...] = jnp.zeros(carry_s.shape, F32)

    logits = _dot_nt(wr_ref[...].astype(BF16), x.astype(BF16))
    scores = 1.0 / (1.0 + jnp.exp(-logits))
    sel = scores + bias_ref[...]
    epg = EXPERTS_PER_GROUP
    io8 = lax.broadcasted_iota(jnp.int32, (epg, tm), 0)
    best_score = None
    for g in range(N_GROUPS):
        sg = sel[g * epg:(g + 1) * epg, :]
        m1, i1 = _first_argmax(sg, io8, epg)
        m2 = jnp.max(jnp.where(io8 == i1, -jnp.inf, sg), 0, keepdims=True)
        gs = m1 + m2
        if g == 0:
            best_score, best = gs, jnp.zeros((1, tm), jnp.int32)
            in_sel, in_sc = sg, scores[0:epg, :]
        else:
            better = gs > best_score
            best_score = jnp.where(better, gs, best_score)
            best = jnp.where(better, g, best)
            in_sel = jnp.where(better, sg, in_sel)
            in_sc = jnp.where(better, scores[g * epg:(g + 1) * epg, :], in_sc)
    _, l1 = _first_argmax(in_sel, io8, epg)
    _, l2 = _first_argmax(jnp.where(io8 == l1, -jnp.inf, in_sel), io8, epg)
    g1 = jnp.sum(jnp.where(io8 == l1, in_sc, 0.0), 0, keepdims=True)
    g2 = jnp.sum(jnp.where(io8 == l2, in_sc, 0.0), 0, keepdims=True)
    e1 = best * epg + l1
    e2 = best * epg + l2
    idx_ref[0:1, :] = e1
    idx_ref[1:2, :] = e2
    gsum = g1 + g2
    gate_ref[0:1, :] = g1 / gsum
    gate_ref[1:2, :] = g2 / gsum

    ioe = lax.broadcasted_iota(jnp.int32, (N_EXPERTS, tm), 0)
    oh1 = ioe == e1
    oh2 = ioe == e2
    onehot = jnp.where(oh1, 1.0, 0.0) + jnp.where(oh2, 1.0, 0.0)
    rr = lax.broadcasted_iota(jnp.int32, (tm, tm), 0)
    cc = lax.broadcasted_iota(jnp.int32, (tm, tm), 1)
    upper = jnp.where(rr < cc, 1.0, 0.0).astype(BF16)
    tot = carry_s[...] + _dot(onehot.astype(BF16), upper)
    rank_ref[0:1, :] = jnp.sum(jnp.where(oh1, tot, 0.0), 0, keepdims=True).astype(jnp.int32)
    rank_ref[1:2, :] = jnp.sum(jnp.where(oh2, tot, 0.0), 0, keepdims=True).astype(jnp.int32)
    new = carry_s[...] + jnp.sum(onehot, -1, keepdims=True)
    carry_s[...] = new
    cnt_ref[...] = jnp.broadcast_to(new, cnt_ref.shape)


def _slot_kernel(idx_ref, rank_ref, pstart_ref, dest_ref, *, tm):
    ioe = lax.broadcasted_iota(jnp.int32, (N_EXPERTS, tm), 0)
    pstart = pstart_ref[...]
    for k in range(TOP_K):
        start = jnp.sum(jnp.where(ioe == idx_ref[k:k + 1, :], pstart, 0.0), 0, keepdims=True)
        dest_ref[k:k + 1, :] = start.astype(jnp.int32) + rank_ref[k:k + 1, :]


def _slots(idx, rank, pstart):
    t = idx.shape[1]
    tm = _tile(t, 2048)
    spec = pl.BlockSpec((TOP_K, tm), lambda i: (0, i))
    return pl.pallas_call(
        functools.partial(_slot_kernel, tm=tm),
        grid=(t // tm,),
        in_specs=[spec, spec, pl.BlockSpec((N_EXPERTS, 1), lambda i: (0, 0))],
        out_specs=spec,
        out_shape=jax.ShapeDtypeStruct((TOP_K, t), jnp.int32),
        compiler_params=_params("parallel"),
        name="moe_slots",
    )(idx, rank, pstart)


def _dispatch_kernel(dest_ref, x_ref, xs_in_ref, xs_ref, sem, *, tm, t_total):
    del xs_in_ref
    base = pl.program_id(0) * tm

    def issue(t, carry):
        for k in range(TOP_K):
            d = dest_ref[k * t_total + base + t]
            pltpu.make_async_copy(x_ref.at[pl.ds(t, 1), :], xs_ref.at[pl.ds(d, 1), :], sem).start(priority=k % 2)
        return carry

    lax.fori_loop(0, tm, issue, 0, unroll=DMA_ISSUE_UNROLL)

    def drain(t, carry):
        for k in range(TOP_K):
            pltpu.make_async_copy(x_ref.at[pl.ds(0, 1), :], xs_ref.at[pl.ds(0, 1), :], sem).wait()
        return carry

    lax.fori_loop(0, tm, drain, 0, unroll=DMA_ISSUE_UNROLL)


def _dispatch(dest_flat, x, n_slots):
    t, d = x.shape
    tm = _tile(t, 512)
    xs0 = jnp.zeros((n_slots, d), x.dtype)
    return pl.pallas_call(
        functools.partial(_dispatch_kernel, tm=tm, t_total=t),
        grid_spec=pltpu.PrefetchScalarGridSpec(
            num_scalar_prefetch=1,
            grid=(t // tm,),
            in_specs=[pl.BlockSpec((tm, d), lambda i, dest: (i, 0)),
                      pl.BlockSpec(memory_space=pl.ANY)],
            out_specs=pl.BlockSpec(memory_space=pl.ANY),
            scratch_shapes=[pltpu.SemaphoreType.DMA(())],
        ),
        out_shape=jax.ShapeDtypeStruct((n_slots, d), x.dtype),
        input_output_aliases={2: 0},
        compiler_params=_params("arbitrary"),
        name="moe_dispatch",
    )(dest_flat, x, xs0)


def _expert_kernel(be_ref, nu_ref, slot_ref, nx_ref, xs_ref, wg_hbm, wu_hbm, wd_hbm, ys_ref,
                   fg_s, fu_s, fd_s, wg_s, wu_s, wd_s, sems, *, layer):
    i = pl.program_id(0)
    e = be_ref[i]

    def weight_copies(expert, slot):
        return (pltpu.make_async_copy(wg_hbm.at[layer, expert], fg_s.at[slot], sems.at[slot, 0]),
                pltpu.make_async_copy(wu_hbm.at[layer, expert], fu_s.at[slot], sems.at[slot, 1]),
                pltpu.make_async_copy(wd_hbm.at[layer, expert], fd_s.at[slot], sems.at[slot, 2]))

    @pl.when(i == 0)
    def _():
        for cp in weight_copies(e, 0):
            cp.start()

    @pl.when((i == 0) | (e != be_ref[jnp.maximum(i - 1, 0)]))
    def _():
        slot = slot_ref[i]
        for cp in weight_copies(e, slot):
            cp.wait()
        wg_s[...] = fg_s[slot].astype(BF16)
        wu_s[...] = fu_s[slot].astype(BF16)
        wd_s[...] = fd_s[slot].astype(BF16)

        @pl.when(nx_ref[i] >= 0)
        def _():
            for cp in weight_copies(nx_ref[i], 1 - slot):
                cp.start()

    @pl.when(i < nu_ref[0])
    def _():
        lo, hi = _unpack_halves(xs_ref[...])
        xb = jnp.concatenate([lo.astype(BF16), hi.astype(BF16)], -1)
        hg = _dot(xb, wg_s[...])
        hu = _dot(xb, wu_s[...])
        h = _silu(hg) * hu
        ys_ref[...] = _pack_halves(_dot(h.astype(BF16), wd_s[...]))

    @pl.when(i >= nu_ref[0])
    def _():
        ys_ref[...] = jnp.zeros(ys_ref.shape, ys_ref.dtype)


def _experts(block_expert, n_used, xs, wg, wu, wd, layer):
    n_slots, dp = xs.shape
    d = 2 * dp
    blk = MOE_BLOCK
    n_blocks = n_slots // blk
    de = wg.shape[-1]
    pos = jnp.arange(n_blocks, dtype=jnp.int32)
    first = jnp.concatenate([jnp.ones((1,), bool), block_expert[1:] != block_expert[:-1]])
    slot = ((jnp.cumsum(first.astype(jnp.int32)) - 1) % 2).astype(jnp.int32)
    first_pos = jnp.where(first, pos, n_blocks)
    next_first = lax.cummin(jnp.concatenate([first_pos[1:], jnp.full((1,), n_blocks, jnp.int32)]), reverse=True)
    next_expert = jnp.where(next_first < n_blocks, block_expert[jnp.minimum(next_first, n_blocks - 1)], -1)
    hbm = pl.BlockSpec(memory_space=pl.ANY)
    return pl.pallas_call(
        functools.partial(_expert_kernel, layer=layer),
        grid_spec=pltpu.PrefetchScalarGridSpec(
            num_scalar_prefetch=4,
            grid=(n_blocks,),
            in_specs=[
                pl.BlockSpec((blk, dp), lambda i, be, nu, sl, nx: (jnp.minimum(i, nu[0] - 1), 0)),
                hbm, hbm, hbm,
            ],
            out_specs=pl.BlockSpec((blk, dp), lambda i, be, nu, sl, nx: (i, 0)),
            scratch_shapes=[pltpu.VMEM((2, d, de), F32), pltpu.VMEM((2, d, de), F32), pltpu.VMEM((2, de, d), F32),
                            pltpu.VMEM((d, de), BF16), pltpu.VMEM((d, de), BF16), pltpu.VMEM((de, d), BF16),
                            pltpu.SemaphoreType.DMA((2, 3))],
        ),
        out_shape=jax.ShapeDtypeStruct((n_slots, dp), jnp.uint32),
        compiler_params=_params("arbitrary"),
        name="moe_experts",
    )(block_expert, n_used, slot, next_expert.astype(jnp.int32), xs, wg, wu, wd)


def _combine_kernel(dest_ref, ys_ref, x_ref, gate_ref, g_ref, b_ref, o_ref, buf_s, sem, *, tm, t_total):
    base = pl.program_id(0) * tm

    def issue(t, carry):
        for k in range(TOP_K):
            d = dest_ref[k * t_total + base + t]
            pltpu.make_async_copy(ys_ref.at[pl.ds(d, 1), :], buf_s.at[k, pl.ds(t, 1), :], sem).start(priority=k % 2)
        return carry

    lax.fori_loop(0, tm, issue, 0, unroll=DMA_ISSUE_UNROLL)

    def drain(t, carry):
        for k in range(TOP_K):
            pltpu.make_async_copy(ys_ref.at[pl.ds(0, 1), :], buf_s.at[0, pl.ds(0, 1), :], sem).wait()
        return carry

    lax.fori_loop(0, tm, drain, 0, unroll=DMA_ISSUE_UNROLL)
    gate = gate_ref[...]
    lo0, hi0 = _unpack_halves(buf_s[0])
    lo1, hi1 = _unpack_halves(buf_s[1])
    g0, g1 = gate[:, 0:1], gate[:, 1:2]
    y = jnp.concatenate([g0 * lo0 + g1 * lo1, g0 * hi0 + g1 * hi1], -1)
    o_ref[...] = _layer_norm(DN_ALPHA * x_ref[...] + y, g_ref[...], b_ref[...])


def _combine(dest_flat, ys, x, gate_tk, g, bias):
    t, d = x.shape
    tm = _tile(t, 512)
    return pl.pallas_call(
        functools.partial(_combine_kernel, tm=tm, t_total=t),
        grid_spec=pltpu.PrefetchScalarGridSpec(
            num_scalar_prefetch=1,
            grid=(t // tm,),
            in_specs=[
                pl.BlockSpec(memory_space=pl.ANY),
                pl.BlockSpec((tm, d), lambda i, dest: (i, 0)),
                pl.BlockSpec((tm, TOP_K), lambda i, dest: (i, 0)),
                pl.BlockSpec((1, d), lambda i, dest: (0, 0)),
                pl.BlockSpec((1, d), lambda i, dest: (0, 0)),
            ],
            out_specs=pl.BlockSpec((tm, d), lambda i, dest: (i, 0)),
            scratch_shapes=[pltpu.VMEM((TOP_K, tm, d // 2), jnp.uint32), pltpu.SemaphoreType.DMA(())],
        ),
        out_shape=jax.ShapeDtypeStruct((t, d), F32),
        compiler_params=_params("arbitrary"),
        name="moe_combine_ln",
    )(dest_flat, ys, x, gate_tk, g.reshape(1, d), bias.reshape(1, d))


def _moe(x, x_packed, routing, wg, wu, wd, layer, g, bias):
    t, d = x.shape
    idx, gate, rank, cnt = routing
    blk = MOE_BLOCK
    n_blocks = t * TOP_K // blk + N_EXPERTS
    counts = cnt[:, 0].astype(jnp.int32)
    padded = (counts + blk - 1) // blk * blk
    padded_end = jnp.cumsum(padded)
    padded_start = padded_end - padded
    dest = _slots(idx, rank, padded_start.astype(F32).reshape(N_EXPERTS, 1)).reshape(-1)
    n_used = (padded_end[-1] // blk).astype(jnp.int32)
    blocks = jnp.minimum(jnp.arange(n_blocks, dtype=jnp.int32), n_used - 1)
    block_expert = jnp.minimum(jnp.searchsorted(padded_end, blocks * blk, side='right'),
                               N_EXPERTS - 1).astype(jnp.int32)
    xs = _dispatch(dest, x_packed, n_blocks * blk)
    ys = _experts(block_expert, n_used.reshape(1), xs, wg, wu, wd, layer)
    return _combine(dest, ys, x, gate.T, g, bias)


def _in_proj_weights(w_in):
    sizes = (Q_LORA, KV_LORA, QK_ROPE, GDN_W, GDN_W, GDN_W, GDN_W, GDN_HEADS, GDN_HEADS, D_MODEL, D_MODEL)
    pts = np.cumsum(sizes)[:-1].tolist()
    c_q, c_kv, k_rope, g_q, g_k, g_v, g_z, g_b, g_a, gate_mla, gate_gdn = jnp.split(w_in, pts, axis=-1)
    half = QK_ROPE // 2
    k_rope_sw = jnp.concatenate([k_rope[:, half:], k_rope[:, :half]], -1)
    main = jnp.concatenate([g_q, g_k, g_v, g_z, gate_mla, gate_gdn], -1)
    used = Q_LORA + KV_LORA + 2 * QK_ROPE + 2 * GDN_HEADS
    small = jnp.concatenate([c_q, c_kv, k_rope, k_rope_sw, g_b, g_a,
                             jnp.zeros((w_in.shape[0], SMALL_W - used), w_in.dtype)], -1)
    return main.astype(BF16), small.astype(BF16)


def _mla_weights(w_uq, w_ukv):
    half = QK_ROPE // 2
    wq = w_uq.reshape(Q_LORA, MLA_HEADS, QK_DIM).transpose(1, 0, 2)
    pe = wq[..., QK_NOPE:]
    pe_sw = jnp.concatenate([pe[..., half:], pe[..., :half]], -1)
    wq = jnp.concatenate([wq, pe_sw], -1).astype(BF16)
    wkv = w_ukv.reshape(KV_LORA, MLA_HEADS, QK_NOPE + V_HEAD).transpose(1, 0, 2).astype(BF16)
    return wq, wkv


def _rope_table(positions):
    half = QK_ROPE // 2
    inv_freq = jnp.power(ROPE_BASE, -jnp.arange(half, dtype=F32) / half)
    ang = positions.astype(F32)[..., None] * inv_freq
    cos, sin = jnp.cos(ang), jnp.sin(ang)
    return jnp.concatenate([cos, cos, -sin, sin], -1)


def kernel(x, mem, positions, w_in, mla_q_norm, mla_kv_norm, w_uq, w_ukv, gdn_conv, gdn_a_log, gdn_dt_bias, gdn_o_norm, w_mla_proj, w_gdn_proj, w_mix_out, ln_mix_g, ln_mix_b, w_xq, w_xkv, w_xo, ln_x_g, ln_x_b, w_router, router_bias, w_gate, w_up, w_down, ln_moe_g, ln_moe_b):
    b, s, d = x.shape
    t = b * s
    cs = _rope_table(positions)
    mem2 = mem.reshape(-1, d)
    w_router_t = w_router.T.astype(F32)
    rbias = router_bias.reshape(N_EXPERTS, 1).astype(F32)
    for l in range(w_in.shape[0]):
        w_main, w_small = _in_proj_weights(w_in[l])
        wq, wkv = _mla_weights(w_uq[l], w_ukv[l])
        xt = x.reshape(t, d)
        main = _matmul(xt, w_main, F32, 1024, 2048, "in_proj_main").reshape(b, s, MAIN_W)
        small = _matmul(xt, w_small, F32, 1024, SMALL_W, "in_proj_small").reshape(b, s, SMALL_W)

        q, k, v = _mla_prep(small, cs, mla_q_norm[l].reshape(1, -1), mla_kv_norm[l].reshape(1, -1), wq, wkv)
        o_mla = _attention(q, k, v)

        small_t = small[:, :, SMALL_GATE_COL:SMALL_GATE_COL + 2 * GDN_HEADS].transpose(0, 2, 1)
        o_gdn = _gdn(main, gdn_conv[l], small, small_t, gdn_a_log[l], gdn_dt_bias[l], gdn_o_norm[l])

        merged = _merge(o_mla.reshape(t, d), o_gdn.reshape(t, d), main.reshape(t, MAIN_W),
                        w_mla_proj[l].astype(BF16), w_gdn_proj[l].astype(BF16))
        x1, x1_bf = _proj_ln(merged, w_mix_out[l].astype(BF16), x.reshape(t, d), ln_mix_g[l], ln_mix_b[l])

        kv = _matmul(mem2, w_xkv[l].astype(BF16), BF16, 1024, 1024, "mem_kv").reshape(b, -1, 2 * X_W)
        x2, x2_pk, *routing = _xattn(x1_bf.reshape(b, s, d), x1.reshape(b, s, d), w_xq[l].astype(BF16), kv,
                                     w_xo[l].astype(BF16), ln_x_g[l], ln_x_b[l], w_router_t, rbias)

        x3 = _moe(x2.reshape(t, d), x2_pk.reshape(t, d // 2), routing, w_gate, w_up, w_down, l,
                  ln_moe_g[l], ln_moe_b[l])
        x = x3.reshape(b, s, d)
    return x
```

```python
import functools
import math

import jax
import jax.numpy as jnp
from jax import lax
from jax.experimental import pallas as pl
from jax.experimental.pallas import tpu as pltpu

D_MODEL = 2048
DEPTH = 2
MLA_HEADS = 16
Q_LORA = 512
KV_LORA = 512
QK_NOPE = 128
QK_ROPE = 64
V_HEAD = 128
QK_DIM = QK_NOPE + QK_ROPE
ROPE_BASE = 10000.0
GDN_HEADS = 16
GDN_DK = 128
GDN_DV = 128
CONV_K = 4
CHUNK = 64
GDN_W = GDN_HEADS * GDN_DK
X_HEADS = 4
X_HEAD_DIM = 128
X_W = X_HEADS * X_HEAD_DIM
N_EXPERTS = 64
N_GROUPS = 8
EXPERTS_PER_GROUP = N_EXPERTS // N_GROUPS
TOP_K = 2
D_EXPERT = 512
Q_PRESCALE = QK_DIM ** -0.5 * math.log2(math.e)
DN_ALPHA = (2.0 * DEPTH) ** 0.25
LN_EPS = 1e-5
RMS_EPS = 1e-6

MAIN_W = 3 * GDN_W + GDN_W + 2 * D_MODEL
SMALL_W = 1280
SMALL_ROPE_BLOCK = (Q_LORA + KV_LORA) // 128
SMALL_GATE_COL = Q_LORA + KV_LORA + 2 * QK_ROPE
SMALL_GATE_BLOCK = SMALL_GATE_COL // 128

GDN_HEAD_BLOCK = 16
GDN_ROW_BLOCK = 128
MOE_BLOCK = 512
DMA_ISSUE_UNROLL = 8
VMEM_LIMIT = 56 * 1024 * 1024

BF16 = jnp.bfloat16
F32 = jnp.float32


def _tile(n, pref):
    return pref if n % pref == 0 else n


def _params(*sem):
    return pltpu.CompilerParams(dimension_semantics=sem, vmem_limit_bytes=VMEM_LIMIT)


def _sigmoid(x):
    return 0.5 * jnp.tanh(0.5 * x) + 0.5


def _silu(x):
    h = 0.5 * x
    return h + h * jnp.tanh(h)


def _pack_halves(y):
    n = y.shape[1] // 2
    lo = pltpu.bitcast(y[:, :n].astype(BF16).astype(F32), jnp.uint32)
    hi = pltpu.bitcast(y[:, n:].astype(BF16).astype(F32), jnp.uint32)
    return (lo >> 16) | (hi & jnp.uint32(0xFFFF0000))


def _unpack_halves(p):
    lo = pltpu.bitcast(p << 16, F32)
    hi = pltpu.bitcast(p & jnp.uint32(0xFFFF0000), F32)
    return lo, hi


def _softplus(x):
    return jnp.maximum(x, 0.0) + jnp.log1p(jnp.exp(-jnp.abs(x)))


def _layer_norm(y, g, b):
    mu = jnp.mean(y, -1, keepdims=True)
    d = y - mu
    var = jnp.mean(d * d, -1, keepdims=True)
    return d * lax.rsqrt(var + LN_EPS) * g + b


def _dot(a, b):
    return jnp.dot(a, b, preferred_element_type=F32)


def _dot_nt(a, b):
    return lax.dot_general(a, b, (((1,), (1,)), ((), ())), preferred_element_type=F32)


def _mm_kernel(a_ref, w_ref, o_ref, a_s):
    @pl.when(pl.program_id(1) == 0)
    def _():
        a_s[...] = a_ref[...].astype(BF16)

    o_ref[...] = _dot(a_s[...], w_ref[...]).astype(o_ref.dtype)


def _matmul(a, w, out_dtype, tm, tn, name):
    m, k = a.shape
    n = w.shape[1]
    tm, tn = _tile(m, tm), _tile(n, tn)
    return pl.pallas_call(
        _mm_kernel,
        grid=(m // tm, n // tn),
        in_specs=[pl.BlockSpec((tm, k), lambda i, j: (i, 0)),
                  pl.BlockSpec((k, tn), lambda i, j: (0, j))],
        out_specs=pl.BlockSpec((tm, tn), lambda i, j: (i, j)),
        out_shape=jax.ShapeDtypeStruct((m, n), out_dtype),
        scratch_shapes=[pltpu.VMEM((tm, k), BF16)],
        compiler_params=_params("parallel", "arbitrary"),
        name=name,
    )(a, w)


def _mla_prep_kernel(cq_ref, ckv_ref, kr_ref, cs_ref, qn_ref, kvn_ref, wq_ref, wkv_ref, q_ref, k_ref, v_ref):
    ts = cq_ref.shape[1]
    cq = cq_ref[0]
    cqn = (cq * lax.rsqrt(jnp.mean(cq * cq, -1, keepdims=True) + RMS_EPS) * qn_ref[...]).astype(BF16)
    ckv = ckv_ref[0]
    ckvn = (ckv * lax.rsqrt(jnp.mean(ckv * ckv, -1, keepdims=True) + RMS_EPS) * kvn_ref[...]).astype(BF16)
    cs = cs_ref[0]
    lane = lax.broadcasted_iota(jnp.int32, (ts, 128), 1)

    def rotate(t):
        u = t * cs
        return jnp.where(lane < QK_ROPE, u + pltpu.roll(u, QK_ROPE, 1), 0.0).astype(BF16)

    kpe = rotate(kr_ref[0])
    ones = jnp.ones((ts, V_HEAD), BF16)
    for h in range(MLA_HEADS):
        rq = _dot(cqn, wq_ref[h]) * Q_PRESCALE
        q_ref[0, h, :, :QK_NOPE] = rq[:, :QK_NOPE].astype(BF16)
        q_ref[0, h, :, QK_NOPE:] = rotate(rq[:, QK_NOPE:])
        rkv = _dot(ckvn, wkv_ref[h])
        k_ref[0, h, :, :QK_NOPE] = rkv[:, :QK_NOPE].astype(BF16)
        k_ref[0, h, :, QK_NOPE:] = kpe
        v_ref[0, h, :, :V_HEAD] = rkv[:, QK_NOPE:].astype(BF16)
        v_ref[0, h, :, V_HEAD:] = ones


def _mla_prep(small, cs, q_norm, kv_norm, wq, wkv):
    b, s, _ = small.shape
    ts = _tile(s, 512)
    hq = MLA_HEADS
    out_spec = pl.BlockSpec((1, hq, ts, 256), lambda bi, si: (bi, 0, si, 0))
    out_shape = jax.ShapeDtypeStruct((b, hq, s, 256), BF16)
    return pl.pallas_call(
        _mla_prep_kernel,
        grid=(b, s // ts),
        in_specs=[
            pl.BlockSpec((1, ts, Q_LORA), lambda bi, si: (bi, si, 0)),
            pl.BlockSpec((1, ts, KV_LORA), lambda bi, si: (bi, si, 1)),
            pl.BlockSpec((1, ts, 128), lambda bi, si: (bi, si, SMALL_ROPE_BLOCK)),
            pl.BlockSpec((1, ts, 128), lambda bi, si: (bi, si, 0)),
            pl.BlockSpec((1, Q_LORA), lambda bi, si: (0, 0)),
            pl.BlockSpec((1, KV_LORA), lambda bi, si: (0, 0)),
            pl.BlockSpec((hq, Q_LORA, 256), lambda bi, si: (0, 0, 0)),
            pl.BlockSpec((hq, KV_LORA, 256), lambda bi, si: (0, 0, 0)),
        ],
        out_specs=[out_spec, out_spec, out_spec],
        out_shape=[out_shape, out_shape, out_shape],
        compiler_params=_params("parallel", "parallel"),
        name="mla_prep",
    )(small, small, small, cs, q_norm, kv_norm, wq, wkv)


def _attn_kernel(q_ref, k_ref, v_ref, o_ref, sa_s, sb_s, m_s, acc_s, *, tq, rc):
    i = pl.program_id(2)
    dv2 = acc_s.shape[-1]
    m_s[...] = jnp.full(m_s.shape, -jnp.inf, F32)
    acc_s[...] = jnp.zeros(acc_s.shape, F32)

    def scores(j):
        start = pl.multiple_of(j * tq, tq)
        return _dot_nt(q_ref[0, 0], k_ref[0, 0, pl.ds(start, tq), :])

    def softmax_pv(s_ref, c, j, nk, masked):
        rows = slice(c * rc, (c + 1) * rc)
        start = pl.multiple_of(j * tq, tq)
        s = s_ref[rows, :nk]
        if masked:
            row = lax.broadcasted_iota(jnp.int32, (rc, nk), 0) + c * rc
            col = lax.broadcasted_iota(jnp.int32, (rc, nk), 1)
            s = jnp.where(col <= row, s, -jnp.inf)
        m_prev = m_s[rows, :]
        m_new = jnp.maximum(m_prev, jnp.max(s, -1, keepdims=True))
        alpha = jnp.exp2(m_prev - m_new)
        p = jnp.exp2(s - jnp.concatenate([m_new] * (nk // 128), -1))
        acc_s[rows, :] = (jnp.concatenate([alpha] * (dv2 // 128), -1) * acc_s[rows, :]
                          + _dot(p.astype(BF16), v_ref[0, 0, pl.ds(start, nk), :]))
        m_s[rows, :] = m_new

    nch = tq // rc

    def step(cur_ref, nxt_ref, j):
        nxt_ref[...] = scores(j + 1)
        for c in range(nch):
            softmax_pv(cur_ref, c, j, tq, False)

    def diagonal(cur_ref):
        for c in range(nch):
            softmax_pv(cur_ref, c, i, (c + 1) * rc, True)
        dv = dv2 // 2
        o_ref[0] = (acc_s[:, :dv] / acc_s[:, dv:]).astype(o_ref.dtype)

    sa_s[...] = scores(0)

    def pair(jj, carry):
        step(sa_s, sb_s, 2 * jj)
        step(sb_s, sa_s, 2 * jj + 1)
        return carry

    lax.fori_loop(0, i // 2, pair, 0)

    @pl.when(i % 2 == 0)
    def _():
        diagonal(sa_s)

    @pl.when(i % 2 == 1)
    def _():
        step(sa_s, sb_s, i - 1)
        diagonal(sb_s)


def _attention(q, k, v):
    b, h, s, dqk = q.shape
    dv2 = v.shape[-1]
    tq = _tile(s, 1024)
    rc = _tile(tq, 256)
    kern = functools.partial(_attn_kernel, tq=tq, rc=rc)
    return pl.pallas_call(
        kern,
        grid=(b, h, s // tq),
        in_specs=[
            pl.BlockSpec((1, 1, tq, dqk), lambda bi, hi, i: (bi, hi, i, 0)),
            pl.BlockSpec((1, 1, s, dqk), lambda bi, hi, i: (bi, hi, 0, 0)),
            pl.BlockSpec((1, 1, s, dv2), lambda bi, hi, i: (bi, hi, 0, 0)),
        ],
        out_specs=pl.BlockSpec((1, tq, dv2 // 2), lambda bi, hi, i: (bi, i, hi)),
        out_shape=jax.ShapeDtypeStruct((b, s, h * dv2 // 2), BF16),
        scratch_shapes=[pltpu.VMEM((tq, tq), F32), pltpu.VMEM((tq, tq), F32), pltpu.VMEM((tq, 128), F32),
                        pltpu.VMEM((tq, dv2), F32)],
        compiler_params=_params("parallel", "parallel", "arbitrary"),
        name="mla_attention",
    )(q, k, v)


def _gdn_kernel(qr_ref, kr_ref, vr_ref, wq_ref, wk_ref, wv_ref, z_ref, gate_ref, gatet_ref, arow_ref, dtrow_ref,
                acol_ref, dtcol_ref, onorm_ref, o_ref, state_s, xx_s, qkv_s, *, hb, rb):
    hg = pl.program_id(1)
    r = pl.program_id(2)

    @pl.when(r == 0)
    def _():
        state_s[...] = jnp.zeros(state_s.shape, F32)
        xx_s[:, 0:8, :] = jnp.zeros((3, 8, hb * 128), F32)

    @pl.when(r > 0)
    def _():
        xx_s[:, 0:8, :] = xx_s[:, rb:rb + 8, :]

    for ti, (x_ref, w_ref) in enumerate(((qr_ref, wq_ref), (kr_ref, wk_ref), (vr_ref, wv_ref))):
        xx_s[ti, 8:rb + 8, :] = x_ref[0]
        wc = w_ref[...]
        y = wc[CONV_K - 1:CONV_K, :] * xx_s[ti, 8:rb + 8, :]
        for j in range(CONV_K - 1):
            off = 8 - (CONV_K - 1) + j
            y = y + wc[j:j + 1, :] * xx_s[ti, off:off + rb, :]
        y = _silu(y)
        if ti == 2:
            qkv_s[ti] = y
        else:
            scale = GDN_DK ** -0.5 if ti == 0 else 1.0
            for h in range(hb):
                yh = y[:, h * 128:(h + 1) * 128]
                qkv_s[ti, :, h * 128:(h + 1) * 128] = yh * (
                    lax.rsqrt(jnp.sum(yh * yh, -1, keepdims=True) + RMS_EPS) * scale)

    def head_rows(ti, c, h):
        return qkv_s[ti, c * CHUNK:(c + 1) * CHUNK, h * 128:(h + 1) * 128]

    blk = gate_ref[0]
    beta_tm = _sigmoid(blk)
    g_tm = -jnp.exp(arow_ref[...]) * _softplus(blk + dtrow_ref[...])
    g_t = -jnp.exp(acol_ref[...]) * _softplus(gatet_ref[0] + dtcol_ref[...])

    ri = lax.broadcasted_iota(jnp.int32, (CHUNK, CHUNK), 0)
    ci = lax.broadcasted_iota(jnp.int32, (CHUNK, CHUNK), 1)
    causal = ri >= ci
    strict = ri > ci
    tri = jnp.where(causal, 1.0, 0.0).astype(F32)
    tri_t = jnp.where(ri <= ci, 1.0, 0.0).astype(F32)
    eye = jnp.where(ri == ci, 1.0, 0.0).astype(F32)
    onorm = onorm_ref[...]

    nc = rb // CHUNK
    items = [(c, h) for c in range(nc) for h in range(hb)]
    rows = [slice(c * CHUNK, (c + 1) * CHUNK) for c in range(nc)]
    gc_tm = [jnp.dot(tri, g_tm[rows[c], :], precision=lax.Precision.HIGHEST, preferred_element_type=F32)
             for c in range(nc)]
    gc_t = [jnp.dot(g_t[:, rows[c]], tri_t, precision=lax.Precision.HIGHEST, preferred_element_type=F32)
            for c in range(nc)]
    sel_l = lax.broadcasted_iota(jnp.int32, (CHUNK, 128), 1)
    sel_s = lax.broadcasted_iota(jnp.int32, (32, CHUNK), 0)
    gcol, bcol, grow, glast, egc, kt, kb = {}, {}, {}, {}, {}, {}, {}
    for it in items:
        c, h = it
        lane = GDN_HEADS + hg * hb + h
        gcol[it] = jnp.sum(jnp.where(sel_l == lane, gc_tm[c], 0.0), -1, keepdims=True)
        bcol[it] = jnp.sum(jnp.where(sel_l == hg * hb + h, beta_tm[rows[c], :], 0.0), -1, keepdims=True)
        grow[it] = jnp.sum(jnp.where(sel_s == lane, gc_t[c], 0.0), 0, keepdims=True)
        glast[it] = gcol[it][CHUNK - 1:CHUNK, :]
        egc[it] = jnp.exp(gcol[it])
        k = head_rows(1, c, h)
        kt[it] = k.T
        kb[it] = k * bcol[it]
    kq = {it: _dot(jnp.concatenate([kb[it], head_rows(0, *it)], 0).astype(BF16),
                   kt[it].astype(BF16)) for it in items}
    amat, m, inv = {}, {}, {}
    for it in items:
        decay = jnp.exp(jnp.where(causal, gcol[it] - grow[it], -jnp.inf))
        amat[it] = (kq[it][CHUNK:] * decay).astype(BF16)
        m[it] = -jnp.where(strict, kq[it][:CHUNK] * decay, 0.0)
        inv[it] = eye + m[it]
    for it in items:
        mb = m[it].astype(BF16)
        m[it] = _dot(mb, mb)
    n_fac = int(round(math.log2(CHUNK))) - 1
    for t in range(n_fac):
        for it in items:
            mb = m[it].astype(BF16)
            if t < n_fac - 1:
                r2 = _dot(jnp.concatenate([inv[it], m[it]], 0).astype(BF16), mb)
                inv[it] = inv[it] + r2[:CHUNK]
                m[it] = r2[CHUNK:]
            else:
                inv[it] = inv[it] + _dot(inv[it].astype(BF16), mb)
    uw = {}
    for it in items:
        c, h = it
        vb = head_rows(2, c, h) * bcol[it]
        uw[it] = _dot(inv[it].astype(BF16), jnp.concatenate([vb, kb[it] * egc[it]], 1).astype(BF16))
    for c in range(nc):
        its = [(c, h) for h in range(hb)]
        st = {it: state_s[it[1]] for it in its}
        wq = {it: _dot(jnp.concatenate([uw[it][:, GDN_DV:], head_rows(0, *it) * egc[it]], 0).astype(BF16),
                       st[it].astype(BF16)) for it in its}
        vnb = {it: (uw[it][:, :GDN_DV] - wq[it][:CHUNK]).astype(BF16) for it in its}
        o = {it: wq[it][CHUNK:] + _dot(amat[it], vnb[it]) for it in its}
        for it in its:
            h = it[1]
            kdt = kt[it] * jnp.exp(glast[it] - grow[it])
            state_s[h] = st[it] * jnp.exp(glast[it]) + _dot(kdt.astype(BF16), vnb[it])
        for it in its:
            h = it[1]
            on = o[it] * lax.rsqrt(jnp.mean(o[it] * o[it], -1, keepdims=True) + RMS_EPS) * onorm
            zz = z_ref[0, rows[c], h * GDN_DV:(h + 1) * GDN_DV]
            o_ref[0, rows[c], h * GDN_DV:(h + 1) * GDN_DV] = (on * _silu(zz)).astype(o_ref.dtype)


def _gdn(main, w_conv, small, small_t, a_log, dt_bias, o_norm):
    b, s, _ = main.shape
    hb = GDN_HEAD_BLOCK
    rb = _tile(s, GDN_ROW_BLOCK)
    ng = GDN_HEADS // hb
    zoff = 3 * GDN_W // (hb * 128)
    pad = lambda a, lo: jnp.zeros((128,), F32).at[lo:lo + GDN_HEADS].set(a.astype(F32))
    arow = pad(a_log, GDN_HEADS).reshape(1, 128)
    dtrow = pad(dt_bias, GDN_HEADS).reshape(1, 128)
    acol = arow[0, :32].reshape(32, 1)
    dtcol = dtrow[0, :32].reshape(32, 1)
    kern = functools.partial(_gdn_kernel, hb=hb, rb=rb)
    return pl.pallas_call(
        kern,
        grid=(b, ng, s // rb),
        in_specs=[
            pl.BlockSpec((1, rb, hb * 128), lambda bi, g, r: (bi, r, g)),
            pl.BlockSpec((1, rb, hb * 128), lambda bi, g, r: (bi, r, ng + g)),
            pl.BlockSpec((1, rb, hb * 128), lambda bi, g, r: (bi, r, 2 * ng + g)),
            pl.BlockSpec((CONV_K, hb * 128), lambda bi, g, r: (0, g)),
            pl.BlockSpec((CONV_K, hb * 128), lambda bi, g, r: (0, ng + g)),
            pl.BlockSpec((CONV_K, hb * 128), lambda bi, g, r: (0, 2 * ng + g)),
            pl.BlockSpec((1, rb, hb * 128), lambda bi, g, r: (bi, r, zoff + g)),
            pl.BlockSpec((1, rb, 128), lambda bi, g, r: (bi, r, SMALL_GATE_BLOCK)),
            pl.BlockSpec((1, 32, rb), lambda bi, g, r: (bi, 0, r)),
            pl.BlockSpec((1, 128), lambda bi, g, r: (0, 0)),
            pl.BlockSpec((1, 128), lambda bi, g, r: (0, 0)),
            pl.BlockSpec((32, 1), lambda bi, g, r: (0, 0)),
            pl.BlockSpec((32, 1), lambda bi, g, r: (0, 0)),
            pl.BlockSpec((1, 128), lambda bi, g, r: (0, 0)),
        ],
        out_specs=pl.BlockSpec((1, rb, hb * 128), lambda bi, g, r: (bi, r, g)),
        out_shape=jax.ShapeDtypeStruct((b, s, GDN_W), BF16),
        scratch_shapes=[pltpu.VMEM((hb, GDN_DK, GDN_DV), F32), pltpu.VMEM((3, rb + 8, hb * 128), F32),
                        pltpu.VMEM((3, rb, hb * 128), F32)],
        compiler_params=_params("parallel", "parallel", "arbitrary"),
        name="gdn_delta_rule",
    )(main, main, main, w_conv, w_conv, w_conv, main, small, small_t, arow, dtrow, acol, dtcol,
      o_norm.reshape(1, GDN_DV).astype(F32))


def _merge_kernel(om_ref, og_ref, gm_ref, gg_ref, wm_ref, wg_ref, o_ref):
    ym = _dot(om_ref[...], wm_ref[...])
    yg = _dot(og_ref[...], wg_ref[...])
    o_ref[...] = (_sigmoid(gm_ref[...]) * ym + _sigmoid(gg_ref[...]) * yg).astype(o_ref.dtype)


def _merge(o_mla, o_gdn, main, w_mla, w_gdn):
    t, d = o_mla.shape
    tm, tn = _tile(t, 512), _tile(d, 1024)
    goff = (4 * GDN_W) // tn
    nn = d // tn
    return pl.pallas_call(
        _merge_kernel,
        grid=(nn, t // tm),
        in_specs=[
            pl.BlockSpec((tm, d), lambda j, i: (i, 0)),
            pl.BlockSpec((tm, d), lambda j, i: (i, 0)),
            pl.BlockSpec((tm, tn), lambda j, i: (i, goff + j)),
            pl.BlockSpec((tm, tn), lambda j, i: (i, goff + nn + j)),
            pl.BlockSpec((d, tn), lambda j, i: (0, j)),
            pl.BlockSpec((d, tn), lambda j, i: (0, j)),
        ],
        out_specs=pl.BlockSpec((tm, tn), lambda j, i: (i, j)),
        out_shape=jax.ShapeDtypeStruct((t, d), BF16),
        compiler_params=_params("parallel", "parallel"),
        name="mixer_merge",
    )(o_mla, o_gdn, main, main, w_mla, w_gdn)


def _proj_ln_kernel(a_ref, w_ref, x_ref, g_ref, b_ref, o_ref, obf_ref):
    y = DN_ALPHA * x_ref[...] + _dot(a_ref[...], w_ref[...])
    out = _layer_norm(y, g_ref[...], b_ref[...])
    o_ref[...] = out
    obf_ref[...] = out.astype(BF16)


def _proj_ln(a, w, x, g, bias):
    t, k = a.shape
    d = w.shape[1]
    tm = _tile(t, 512)
    return pl.pallas_call(
        _proj_ln_kernel,
        grid=(t // tm,),
        in_specs=[
            pl.BlockSpec((tm, k), lambda i: (i, 0)),
            pl.BlockSpec((k, d), lambda i: (0, 0)),
            pl.BlockSpec((tm, d), lambda i: (i, 0)),
            pl.BlockSpec((1, d), lambda i: (0, 0)),
            pl.BlockSpec((1, d), lambda i: (0, 0)),
        ],
        out_specs=[pl.BlockSpec((tm, d), lambda i: (i, 0)), pl.BlockSpec((tm, d), lambda i: (i, 0))],
        out_shape=[jax.ShapeDtypeStruct((t, d), F32), jax.ShapeDtypeStruct((t, d), BF16)],
        compiler_params=_params("parallel"),
        name="proj_residual_ln",
    )(a, w, x, g.reshape(1, d), bias.reshape(1, d))


def _xattn_kernel(xbf_ref, x_ref, wq_ref, kv_ref, wo_ref, g_ref, b_ref, wr_ref, rb_ref,
                  o_ref, opk_ref, idx_ref, gate_ref, rank_ref, cnt_ref, carry_s):
    q = _dot(xbf_ref[0], wq_ref[...])
    scale = X_HEAD_DIM ** -0.5
    outs = []
    for h in range(X_HEADS):
        qh = q[:, h * X_HEAD_DIM:(h + 1) * X_HEAD_DIM].astype(BF16)
        kh = kv_ref[0, :, h * X_HEAD_DIM:(h + 1) * X_HEAD_DIM]
        vh = kv_ref[0, :, X_W + h * X_HEAD_DIM:X_W + (h + 1) * X_HEAD_DIM]
        s = _dot_nt(qh, kh) * scale
        e = jnp.exp(s - jnp.max(s, -1, keepdims=True))
        p = e / jnp.sum(e, -1, keepdims=True)
        outs.append(_dot(p.astype(BF16), vh))
    o = jnp.concatenate(outs, -1).astype(BF16)
    y = DN_ALPHA * x_ref[0] + _dot(o, wo_ref[...])
    out = _layer_norm(y, g_ref[...], b_ref[...])
    o_ref[0] = out
    opk_ref[0] = _pack_halves(out)
    first = (pl.program_id(0) == 0) & (pl.program_id(1) == 0)
    _route_tile(out, first, wr_ref, rb_ref, idx_ref, gate_ref, rank_ref, cnt_ref, carry_s)


def _xattn(x_bf, x, wq, kv, wo, g, bias, w_router_t, router_bias):
    b, s, d = x.shape
    tm = _tile(s, 512)
    nt = s // tm
    ml = kv.shape[1]
    t = b * s
    tok = pl.BlockSpec((TOP_K, tm), lambda bi, i: (0, bi * nt + i))
    return pl.pallas_call(
        _xattn_kernel,
        grid=(b, nt),
        in_specs=[
            pl.BlockSpec((1, tm, d), lambda bi, i: (bi, i, 0)),
            pl.BlockSpec((1, tm, d), lambda bi, i: (bi, i, 0)),
            pl.BlockSpec((d, X_W), lambda bi, i: (0, 0)),
            pl.BlockSpec((1, ml, 2 * X_W), lambda bi, i: (bi, 0, 0)),
            pl.BlockSpec((X_W, d), lambda bi, i: (0, 0)),
            pl.BlockSpec((1, d), lambda bi, i: (0, 0)),
            pl.BlockSpec((1, d), lambda bi, i: (0, 0)),
            pl.BlockSpec((N_EXPERTS, d), lambda bi, i: (0, 0)),
            pl.BlockSpec((N_EXPERTS, 1), lambda bi, i: (0, 0)),
        ],
        out_specs=[pl.BlockSpec((1, tm, d), lambda bi, i: (bi, i, 0)),
                   pl.BlockSpec((1, tm, d // 2), lambda bi, i: (bi, i, 0)),
                   tok, tok, tok,
                   pl.BlockSpec((N_EXPERTS, 128), lambda bi, i: (0, 0))],
        out_shape=[jax.ShapeDtypeStruct((b, s, d), F32), jax.ShapeDtypeStruct((b, s, d // 2), jnp.uint32),
                   jax.ShapeDtypeStruct((TOP_K, t), jnp.int32), jax.ShapeDtypeStruct((TOP_K, t), F32),
                   jax.ShapeDtypeStruct((TOP_K, t), jnp.int32), jax.ShapeDtypeStruct((N_EXPERTS, 128), F32)],
        scratch_shapes=[pltpu.VMEM((N_EXPERTS, 1), F32)],
        compiler_params=_params("arbitrary", "arbitrary"),
        name="memory_cross_attention",
    )(x_bf, x, wq, kv, wo, g.reshape(1, d), bias.reshape(1, d), w_router_t, router_bias)


def _first_argmax(vals, iota, n):
    m = jnp.max(vals, 0, keepdims=True)
    idx = jnp.min(jnp.where(vals == m, iota, n), 0, keepdims=True)
    return m, idx


def _route_tile(x, first, wr_ref, bias_ref, idx_ref, gate_ref, rank_ref, cnt_ref, carry_s):
    tm = x.shape[0]

    @pl.when(first)
    def _():
        carry_s[...] = jnp.zeros(carry_s.shape, F32)

    logits = _dot_nt(wr_ref[...].astype(BF16), x.astype(BF16))
    scores = 1.0 / (1.0 + jnp.exp(-logits))
    sel = scores + bias_ref[...]
    epg = EXPERTS_PER_GROUP
    io8 = lax.broadcasted_iota(jnp.int32, (epg, tm), 0)
    best_score = None
    for g in range(N_GROUPS):
        sg = sel[g * epg:(g + 1) * epg, :]
        m1, i1 = _first_argmax(sg, io8, epg)
        m2 = jnp.max(jnp.where(io8 == i1, -jnp.inf, sg), 0, keepdims=True)
        gs = m1 + m2
        if g == 0:
            best_score, best = gs, jnp.zeros((1, tm), jnp.int32)
            in_sel, in_sc = sg, scores[0:epg, :]
        else:
            better = gs > best_score
            best_score = jnp.where(better, gs, best_score)
            best = jnp.where(better, g, best)
            in_sel = jnp.where(better, sg, in_sel)
            in_sc = jnp.where(better, scores[g * epg:(g + 1) * epg, :], in_sc)
    _, l1 = _first_argmax(in_sel, io8, epg)
    _, l2 = _first_argmax(jnp.where(io8 == l1, -jnp.inf, in_sel), io8, epg)
    g1 = jnp.sum(jnp.where(io8 == l1, in_sc, 0.0), 0, keepdims=True)
    g2 = jnp.sum(jnp.where(io8 == l2, in_sc, 0.0), 0, keepdims=True)
    e1 = best * epg + l1
    e2 = best * epg + l2
    idx_ref[0:1, :] = e1
    idx_ref[1:2, :] = e2
    gsum = g1 + g2
    gate_ref[0:1, :] = g1 / gsum
    gate_ref[1:2, :] = g2 / gsum

    ioe = lax.broadcasted_iota(jnp.int32, (N_EXPERTS, tm), 0)
    oh1 = ioe == e1
    oh2 = ioe == e2
    onehot = jnp.where(oh1, 1.0, 0.0) + jnp.where(oh2, 1.0, 0.0)
    rr = lax.broadcasted_iota(jnp.int32, (tm, tm), 0)
    cc = lax.broadcasted_iota(jnp.int32, (tm, tm), 1)
    upper = jnp.where(rr < cc, 1.0, 0.0).astype(BF16)
    tot = carry_s[...] + _dot(onehot.astype(BF16), upper)
    rank_ref[0:1, :] = jnp.sum(jnp.where(oh1, tot, 0.0), 0, keepdims=True).astype(jnp.int32)
    rank_ref[1:2, :] = jnp.sum(jnp.where(oh2, tot, 0.0), 0, keepdims=True).astype(jnp.int32)
    new = carry_s[...] + jnp.sum(onehot, -1, keepdims=True)
    carry_s[...] = new
    cnt_ref[...] = jnp.broadcast_to(new, cnt_ref.shape)


def _slot_kernel(idx_ref, rank_ref, pstart_ref, dest_ref, *, tm):
    ioe = lax.broadcasted_iota(jnp.int32, (N_EXPERTS, tm), 0)
    pstart = pstart_ref[...]
    for k in range(TOP_K):
        start = jnp.sum(jnp.where(ioe == idx_ref[k:k + 1, :], pstart, 0.0), 0, keepdims=True)
        dest_ref[k:k + 1, :] = start.astype(jnp.int32) + rank_ref[k:k + 1, :]


def _slots(idx, rank, pstart):
    t = idx.shape[1]
    tm = _tile(t, 2048)
    spec = pl.BlockSpec((TOP_K, tm), lambda i: (0, i))
    return pl.pallas_call(
        functools.partial(_slot_kernel, tm=tm),
        grid=(t // tm,),
        in_specs=[spec, spec, pl.BlockSpec((N_EXPERTS, 1), lambda i: (0, 0))],
        out_specs=spec,
        out_shape=jax.ShapeDtypeStruct((TOP_K, t), jnp.int32),
        compiler_params=_params("parallel"),
        name="moe_slots",
    )(idx, rank, pstart)


def _dispatch_kernel(dest_ref, x_ref, xs_in_ref, xs_ref, sem, *, tm, t_total):
    del xs_in_ref
    base = pl.program_id(0) * tm

    def issue(t, carry):
        for k in range(TOP_K):
            d = dest_ref[k * t_total + base + t]
            pltpu.make_async_copy(x_ref.at[pl.ds(t, 1), :], xs_ref.at[pl.ds(d, 1), :], sem).start(priority=k % 2)
        return carry

    lax.fori_loop(0, tm, issue, 0, unroll=DMA_ISSUE_UNROLL)

    def drain(t, carry):
        for k in range(TOP_K):
            pltpu.make_async_copy(x_ref.at[pl.ds(0, 1), :], xs_ref.at[pl.ds(0, 1), :], sem).wait()
        return carry

    lax.fori_loop(0, tm, drain, 0, unroll=DMA_ISSUE_UNROLL)


def _dispatch(dest_flat, x, n_slots):
    t, d = x.shape
    tm = _tile(t, 512)
    xs0 = jnp.zeros((n_slots, d), x.dtype)
    return pl.pallas_call(
        functools.partial(_dispatch_kernel, tm=tm, t_total=t),
        grid_spec=pltpu.PrefetchScalarGridSpec(
            num_scalar_prefetch=1,
            grid=(t // tm,),
            in_specs=[pl.BlockSpec((tm, d), lambda i, dest: (i, 0)),
                      pl.BlockSpec(memory_space=pl.ANY)],
            out_specs=pl.BlockSpec(memory_space=pl.ANY),
            scratch_shapes=[pltpu.SemaphoreType.DMA(())],
        ),
        out_shape=jax.ShapeDtypeStruct((n_slots, d), x.dtype),
        input_output_aliases={2: 0},
        compiler_params=_params("arbitrary"),
        name="moe_dispatch",
    )(dest_flat, x, xs0)


def _expert_kernel(be_ref, nu_ref, slot_ref, nx_ref, xs_ref, wg_hbm, wu_hbm, wd_hbm, ys_ref,
                   fg_s, fu_s, fd_s, wg_s, wu_s, wd_s, sems, *, layer):
    i = pl.program_id(0)
    e = be_ref[i]

    def weight_copies(expert, slot):
        return (pltpu.make_async_copy(wg_hbm.at[layer, expert], fg_s.at[slot], sems.at[slot, 0]),
                pltpu.make_async_copy(wu_hbm.at[layer, expert], fu_s.at[slot], sems.at[slot, 1]),
                pltpu.make_async_copy(wd_hbm.at[layer, expert], fd_s.at[slot], sems.at[slot, 2]))

    @pl.when(i == 0)
    def _():
        for cp in weight_copies(e, 0):
            cp.start()

    @pl.when((i == 0) | (e != be_ref[jnp.maximum(i - 1, 0)]))
    def _():
        slot = slot_ref[i]
        for cp in weight_copies(e, slot):
            cp.wait()
        wg_s[...] = fg_s[slot].astype(BF16)
        wu_s[...] = fu_s[slot].astype(BF16)
        wd_s[...] = fd_s[slot].astype(BF16)

        @pl.when(nx_ref[i] >= 0)
        def _():
            for cp in weight_copies(nx_ref[i], 1 - slot):
                cp.start()

    @pl.when(i < nu_ref[0])
    def _():
        lo, hi = _unpack_halves(xs_ref[...])
        xb = jnp.concatenate([lo.astype(BF16), hi.astype(BF16)], -1)
        hg = _dot(xb, wg_s[...])
        hu = _dot(xb, wu_s[...])
        h = _silu(hg) * hu
        ys_ref[...] = _pack_halves(_dot(h.astype(BF16), wd_s[...]))

    @pl.when(i >= nu_ref[0])
    def _():
        ys_ref[...] = jnp.zeros(ys_ref.shape, ys_ref.dtype)


def _experts(block_expert, n_used, xs, wg, wu, wd, layer):
    n_slots, dp = xs.shape
    d = 2 * dp
    blk = MOE_BLOCK
    n_blocks = n_slots // blk
    de = wg.shape[-1]
    pos = jnp.arange(n_blocks, dtype=jnp.int32)
    first = jnp.concatenate([jnp.ones((1,), bool), block_expert[1:] != block_expert[:-1]])
    slot = ((jnp.cumsum(first.astype(jnp.int32)) - 1) % 2).astype(jnp.int32)
    first_pos = jnp.where(first, pos, n_blocks)
    next_first = lax.cummin(jnp.concatenate([first_pos[1:], jnp.full((1,), n_blocks, jnp.int32)]), reverse=True)
    next_expert = jnp.where(next_first < n_blocks, block_expert[jnp.minimum(next_first, n_blocks - 1)], -1)
    hbm = pl.BlockSpec(memory_space=pl.ANY)
    return pl.pallas_call(
        functools.partial(_expert_kernel, layer=layer),
        grid_spec=pltpu.PrefetchScalarGridSpec(
            num_scalar_prefetch=4,
            grid=(n_blocks,),
            in_specs=[
                pl.BlockSpec((blk, dp), lambda i, be, nu, sl, nx: (jnp.minimum(i, nu[0] - 1), 0)),
                hbm, hbm, hbm,
            ],
            out_specs=pl.BlockSpec((blk, dp), lambda i, be, nu, sl, nx: (i, 0)),
            scratch_shapes=[pltpu.VMEM((2, d, de), F32), pltpu.VMEM((2, d, de), F32), pltpu.VMEM((2, de, d), F32),
                            pltpu.VMEM((d, de), BF16), pltpu.VMEM((d, de), BF16), pltpu.VMEM((de, d), BF16),
                            pltpu.SemaphoreType.DMA((2, 3))],
        ),
        out_shape=jax.ShapeDtypeStruct((n_slots, dp), jnp.uint32),
        compiler_params=_params("arbitrary"),
        name="moe_experts",
    )(block_expert, n_used, slot, next_expert.astype(jnp.int32), xs, wg, wu, wd)


def _combine_kernel(dest_ref, ys_ref, x_ref, gate_ref, g_ref, b_ref, o_ref, buf_s, sem, *, tm, t_total):
    base = pl.program_id(0) * tm

    def issue(t, carry):
        for k in range(TOP_K):
            d = dest_ref[k * t_total + base + t]
            pltpu.make_async_copy(ys_ref.at[pl.ds(d, 1), :], buf_s.at[k, pl.ds(t, 1), :], sem).start(priority=k % 2)
        return carry

    lax.fori_loop(0, tm, issue, 0, unroll=DMA_ISSUE_UNROLL)

    def drain(t, carry):
        for k in range(TOP_K):
            pltpu.make_async_copy(ys_ref.at[pl.ds(0, 1), :], buf_s.at[0, pl.ds(0, 1), :], sem).wait()
        return carry

    lax.fori_loop(0, tm, drain, 0, unroll=DMA_ISSUE_UNROLL)
    gate = gate_ref[...]
    lo0, hi0 = _unpack_halves(buf_s[0])
    lo1, hi1 = _unpack_halves(buf_s[1])
    g0, g1 = gate[:, 0:1], gate[:, 1:2]
    y = jnp.concatenate([g0 * lo0 + g1 * lo1, g0 * hi0 + g1 * hi1], -1)
    o_ref[...] = _layer_norm(DN_ALPHA * x_ref[...] + y, g_ref[...], b_ref[...])


def _combine(dest_flat, ys, x, gate_tk, g, bias):
    t, d = x.shape
    tm = _tile(t, 512)
    return pl.pallas_call(
        functools.partial(_combine_kernel, tm=tm, t_total=t),
        grid_spec=pltpu.PrefetchScalarGridSpec(
            num_scalar_prefetch=1,
            grid=(t // tm,),
            in_specs=[
                pl.BlockSpec(memory_space=pl.ANY),
                pl.BlockSpec((tm, d), lambda i, dest: (i, 0)),
                pl.BlockSpec((tm, TOP_K), lambda i, dest: (i, 0)),
                pl.BlockSpec((1, d), lambda i, dest: (0, 0)),
                pl.BlockSpec((1, d), lambda i, dest: (0, 0)),
            ],
            out_specs=pl.BlockSpec((tm, d), lambda i, dest: (i, 0)),
            scratch_shapes=[pltpu.VMEM((TOP_K, tm, d // 2), jnp.uint32), pltpu.SemaphoreType.DMA(())],
        ),
        out_shape=jax.ShapeDtypeStruct((t, d), F32),
        compiler_params=_params("arbitrary"),
        name="moe_combine_ln",
    )(dest_flat, ys, x, gate_tk, g.reshape(1, d), bias.reshape(1, d))


def _moe(x, x_packed, routing, wg, wu, wd, layer, g, bias):
    t, d = x.shape
    idx, gate, rank, cnt = routing
    blk = MOE_BLOCK
    n_blocks = t * TOP_K // blk + N_EXPERTS
    counts = cnt[:, 0].astype(jnp.int32)
    padded = (counts + blk - 1) // blk * blk
    padded_end = jnp.cumsum(padded)
    padded_start = padded_end - padded
    dest = _slots(idx, rank, padded_start.astype(F32).reshape(N_EXPERTS, 1)).reshape(-1)
    n_used = (padded_end[-1] // blk).astype(jnp.int32)
    blocks = jnp.minimum(jnp.arange(n_blocks, dtype=jnp.int32), n_used - 1)
    block_expert = jnp.minimum(jnp.searchsorted(padded_end, blocks * blk, side='right'),
                               N_EXPERTS - 1).astype(jnp.int32)
    xs = _dispatch(dest, x_packed, n_blocks * blk)
    ys = _experts(block_expert, n_used.reshape(1), xs, wg, wu, wd, layer)
    return _combine(dest, ys, x, gate.T, g, bias)


def _in_proj_weights(w_in):
    lat = Q_LORA + KV_LORA
    gdn0 = lat + QK_ROPE
    gate0 = gdn0 + 4 * GDN_W
    mix0 = gate0 + 2 * GDN_HEADS
    half = QK_ROPE // 2
    main = jnp.concatenate([w_in[:, gdn0:gate0], w_in[:, mix0:]], -1)
    used = Q_LORA + KV_LORA + 2 * QK_ROPE + 2 * GDN_HEADS
    small = jnp.concatenate([w_in[:, :gdn0], w_in[:, lat + half:gdn0], w_in[:, lat:lat + half], w_in[:, gate0:mix0],
                             jnp.zeros((w_in.shape[0], SMALL_W - used), w_in.dtype)], -1)
    return main.astype(BF16), small.astype(BF16)


def _mla_weights(w_uq, w_ukv):
    half = QK_ROPE // 2
    wq = w_uq.reshape(Q_LORA, MLA_HEADS, QK_DIM).transpose(1, 0, 2)
    pe = wq[..., QK_NOPE:]
    pe_sw = jnp.concatenate([pe[..., half:], pe[..., :half]], -1)
    wq = jnp.concatenate([wq, pe_sw], -1).astype(BF16)
    wkv = w_ukv.reshape(KV_LORA, MLA_HEADS, QK_NOPE + V_HEAD).transpose(1, 0, 2).astype(BF16)
    return wq, wkv


def _rope_table(positions):
    half = QK_ROPE // 2
    inv_freq = jnp.power(ROPE_BASE, -jnp.arange(half, dtype=F32) / half)
    ang = positions.astype(F32)[..., None] * jnp.tile(inv_freq, 4)
    lane = jnp.arange(4 * half)
    sign = jnp.where((lane >= 2 * half) & (lane < 3 * half), -1.0, 1.0).astype(F32)
    return jnp.where(lane < 2 * half, jnp.cos(ang), jnp.sin(ang) * sign)


def kernel(x, mem, positions, w_in, mla_q_norm, mla_kv_norm, w_uq, w_ukv, gdn_conv, gdn_a_log, gdn_dt_bias, gdn_o_norm, w_mla_proj, w_gdn_proj, w_mix_out, ln_mix_g, ln_mix_b, w_xq, w_xkv, w_xo, ln_x_g, ln_x_b, w_router, router_bias, w_gate, w_up, w_down, ln_moe_g, ln_moe_b):
    b, s, d = x.shape
    t = b * s
    cs = _rope_table(positions)
    mem2 = mem.reshape(-1, d)
    w_router_t = w_router.T.astype(F32)
    rbias = router_bias.reshape(N_EXPERTS, 1).astype(F32)
    for l in range(w_in.shape[0]):
        w_main, w_small = _in_proj_weights(w_in[l])
        wq, wkv = _mla_weights(w_uq[l], w_ukv[l])
        xt = x.reshape(t, d)
        main = _matmul(xt, w_main, F32, 1024, 2048, "in_proj_main").reshape(b, s, MAIN_W)
        small = _matmul(xt, w_small, F32, 1024, SMALL_W, "in_proj_small").reshape(b, s, SMALL_W)

        q, k, v = _mla_prep(small, cs, mla_q_norm[l].reshape(1, -1), mla_kv_norm[l].reshape(1, -1), wq, wkv)
        o_mla = _attention(q, k, v)

        small_t = small[:, :, SMALL_GATE_COL:SMALL_GATE_COL + 2 * GDN_HEADS].transpose(0, 2, 1)
        o_gdn = _gdn(main, gdn_conv[l], small, small_t, gdn_a_log[l], gdn_dt_bias[l], gdn_o_norm[l])

        merged = _merge(o_mla.reshape(t, d), o_gdn.reshape(t, d), main.reshape(t, MAIN_W),
                        w_mla_proj[l].astype(BF16), w_gdn_proj[l].astype(BF16))
        x1, x1_bf = _proj_ln(merged, w_mix_out[l].astype(BF16), x.reshape(t, d), ln_mix_g[l], ln_mix_b[l])

        kv = _matmul(mem2, w_xkv[l].astype(BF16), BF16, 1024, 1024, "mem_kv").reshape(b, -1, 2 * X_W)
        x2, x2_pk, *routing = _xattn(x1_bf.reshape(b, s, d), x1.reshape(b, s, d), w_xq[l].astype(BF16), kv,
                                     w_xo[l].astype(BF16), ln_x_g[l], ln_x_b[l], w_router_t, rbias)

        x3 = _moe(x2.reshape(t, d), x2_pk.reshape(t, d // 2), routing, w_gate, w_up, w_down, l,
                  ln_moe_g[l], ln_moe_b[l])
        x = x3.reshape(b, s, d)
    return x
```
